```python
import jax, jax.numpy as jnp
from jax import lax
import numpy as np

D_MODEL = 1024
BATCH = 8
SEQ = 2048
DEPTH = 2
DEC_BATCH = 128
DEC_SEQ = 8
PAST_LEN = 16384
PAGE_SIZE = 128

HEAD_DIM = 64
ROT_DIM = HEAD_DIM // 4
ROPE_THETA = 500000.0
DIL_PAIRS = ((128, 1), (512, 4), (2048, 16))
N_GROUPS_A = 3
HEADS_PER_GROUP_A = 4
A_OUT = HEADS_PER_GROUP_A * HEAD_DIM
N_HEADS_B = 8
N_KV_B = 2
GQA_B = N_HEADS_B // N_KV_B
WINDOW_B = 128
B_Q = N_HEADS_B * HEAD_DIM
B_KV = N_KV_B * HEAD_DIM
IN_COLS = N_GROUPS_A * 3 * A_OUT + B_Q + 2 * B_KV + 2 * D_MODEL
D_FF = ((8 * D_MODEL // 3 + 127) // 128) * 128
BLK = 128
NORM_EPS = 1e-6
NEG = -1e30
SCALE = HEAD_DIM ** -0.5

kernel_name = "hybrid_dilated_sinkgqa_macaron_decode_step"


def _rms(x, g):
    xf = x.astype(jnp.float32)
    y = xf * lax.rsqrt(jnp.mean(xf * xf, axis=-1, keepdims=True) + NORM_EPS)
    return (y * g.astype(jnp.float32)).astype(x.dtype)


def _swiglu(h, wg, wu, wd):
    return (jax.nn.silu(h @ wg) * (h @ wu)) @ wd


def _rope(x, pos):
    half = ROT_DIM // 2
    inv = ROPE_THETA ** (-(jnp.arange(half, dtype=jnp.float32) * 2.0 / ROT_DIM))
    ang = pos.astype(jnp.float32)[:, None] * inv[None, :]
    cos = jnp.cos(ang)[:, None, :]
    sin = jnp.sin(ang)[:, None, :]
    xr = x[..., :ROT_DIM].astype(jnp.float32)
    x1, x2 = xr[..., :half], xr[..., half:]
    rot = jnp.concatenate([x1 * cos - x2 * sin, x2 * cos + x1 * sin], axis=-1).astype(x.dtype)
    return jnp.concatenate([rot, x[..., ROT_DIM:]], axis=-1)


def _softmax_stats(s, valid):
    s = jnp.where(valid, s, NEG)
    m = jnp.max(s, axis=-1, keepdims=True)
    e = jnp.exp(s - m)
    den = jnp.sum(e, axis=-1, keepdims=True)
    return e / den, (m + jnp.log(den))[..., 0]


def _band_attn(q, k, v, n_back):
    N, L, Hk, G, Dh = q.shape
    nb = -(-L // BLK)
    Lp = nb * BLK
    pad = Lp - L
    qb = jnp.pad(q, ((0, 0), (0, pad), (0, 0), (0, 0), (0, 0))).reshape(N, nb, BLK, Hk, G, Dh)
    kp = jnp.pad(k, ((0, 0), (BLK, pad), (0, 0), (0, 0))).reshape(N, nb + 1, BLK, Hk, Dh)
    vp = jnp.pad(v, ((0, 0), (BLK, pad), (0, 0), (0, 0))).reshape(N, nb + 1, BLK, Hk, Dh)
    kw = jnp.concatenate([kp[:, :-1], kp[:, 1:]], axis=2)
    vw = jnp.concatenate([vp[:, :-1], vp[:, 1:]], axis=2)
    s = jnp.einsum('nbqhgd,nbkhd->nbhgqk', qb, kw, preferred_element_type=jnp.float32)
    b = jnp.arange(nb)[:, None, None]
    r = jnp.arange(BLK)[None, :, None]
    c = jnp.arange(2 * BLK)[None, None, :]
    dist = BLK + r - c
    valid = (dist >= 0) & (dist <= n_back) & ((b - 1) * BLK + c >= 0)
    p, lse = _softmax_stats(s, valid[None, :, None, None])
    o = jnp.einsum('nbhgqk,nbkhd->nbqhgd', p.astype(v.dtype), vw,
                   preferred_element_type=jnp.float32).astype(v.dtype)
    o = o.reshape(N, Lp, Hk, G, Dh)[:, :L]
    lse = lse.transpose(0, 1, 4, 2, 3).reshape(N, Lp, Hk, G)[:, :L]
    return o, lse


def _dilated_prompt(q, k, v, window, dil):
    N, L, H, Dh = q.shape
    Ls = L // dil

    def strided(x):
        return x.reshape(N, Ls, dil, H, Dh).transpose(0, 2, 1, 3, 4).reshape(N * dil, Ls, H, Dh)

    o, lse = _band_attn(strided(q)[:, :, :, None], strided(k), strided(v), window // dil)
    o = o[:, :, :, 0].reshape(N, dil, Ls, H, Dh).transpose(0, 2, 1, 3, 4).reshape(N, L, H, Dh)
    lse = lse[..., 0].reshape(N, dil, Ls, H).transpose(0, 2, 1, 3).reshape(N, L, H)
    return o, lse


def _dilated_sample(q, k_new, v_new, buf, window, dil):
    N, T, H, Dh = q.shape
    Lb = buf.shape[1]
    kc = jnp.concatenate([buf[:, :, 0], k_new], axis=1)
    vc = jnp.concatenate([buf[:, :, 1], v_new], axis=1)
    idx = Lb + jnp.arange(T)[:, None] - dil * jnp.arange(window // dil + 1)[None, :]
    valid = idx >= 0
    idxc = jnp.maximum(idx, 0)
    kg = kc[:, idxc]
    vg = vc[:, idxc]
    s = jnp.einsum('nthd,ntjhd->nhtj', q, kg, preferred_element_type=jnp.float32)
    p, lse = _softmax_stats(s, valid[None, None])
    o = jnp.einsum('nhtj,ntjhd->nthd', p.astype(vg.dtype), vg,
                   preferred_element_type=jnp.float32).astype(vg.dtype)
    keep = min(window, Lb + T)
    new_buf = jnp.concatenate([buf, jnp.stack([k_new, v_new], axis=2)], axis=1)[:, Lb + T - keep:]
    return o, lse.transpose(0, 2, 1), new_buf


def _window_sample(q, k_new, v_new, buf):
    T = q.shape[1]
    Lb = buf.shape[1]
    kc = jnp.concatenate([buf[:, :, 0], k_new], axis=1)
    vc = jnp.concatenate([buf[:, :, 1], v_new], axis=1)
    s = jnp.einsum('nthgd,nkhd->nhgtk', q, kc, preferred_element_type=jnp.float32)
    dist = Lb + jnp.arange(T)[:, None] - jnp.arange(Lb + T)[None, :]
    valid = (dist >= 0) & (dist < WINDOW_B)
    p, lse = _softmax_stats(s, valid[None, None, None])
    o = jnp.einsum('nhgtk,nkhd->nthgd', p.astype(vc.dtype), vc,
                   preferred_element_type=jnp.float32).astype(vc.dtype)
    keep = min(WINDOW_B, Lb + T)
    new_buf = jnp.concatenate([buf, jnp.stack([k_new, v_new], axis=2)], axis=1)[:, Lb + T - keep:]
    return o, lse.transpose(0, 3, 1, 2), new_buf


def _mixer_inputs(h, w_in, pos):
    N, L, _ = h.shape
    sizes = [A_OUT] * (3 * N_GROUPS_A) + [B_Q, B_KV, B_KV, D_MODEL, D_MODEL]
    cuts = [int(c) for c in np.cumsum(sizes)[:-1]]
    parts = jnp.split(h @ w_in, cuts, axis=-1)
    groups = []
    for g in range(N_GROUPS_A):
        q, k, v = (p.reshape(N, L, HEADS_PER_GROUP_A, HEAD_DIM) for p in parts[3 * g:3 * g + 3])
        groups.append((_rope(q, pos) * SCALE, _rope(k, pos), v))
    bq, bk, bv, ga, gb = parts[3 * N_GROUPS_A:]
    bq = (_rope(bq.reshape(N, L, N_HEADS_B, HEAD_DIM), pos) * SCALE).reshape(N, L, N_KV_B, GQA_B, HEAD_DIM)
    bk = _rope(bk.reshape(N, L, N_KV_B, HEAD_DIM), pos)
    bv = bv.reshape(N, L, N_KV_B, HEAD_DIM)
    return groups, bq, bk, bv, ga, gb


def _mixer_output(oa_list, la_list, ob, lb, ga, gb, sinks, w_ba, w_bb, w_out):
    N, L = ga.shape[:2]
    wts = jax.nn.softmax(jnp.stack(la_list, axis=0), axis=0)
    oa = jnp.sum(wts[..., None] * jnp.stack(oa_list, axis=0).astype(jnp.float32), axis=0)
    oa = oa.astype(ga.dtype).reshape(N, L, A_OUT)
    sink_scale = jax.nn.sigmoid(lb - sinks.reshape(N_KV_B, GQA_B).astype(jnp.float32))
    ob = (ob.astype(jnp.float32) * sink_scale[..., None]).astype(ga.dtype).reshape(N, L, B_Q)
    merged = jax.nn.sigmoid(ga) * (oa @ w_ba) + jax.nn.sigmoid(gb) * (ob @ w_bb)
    return merged @ w_out


def _mixer_prompt(h, w_in, sinks, w_ba, w_bb, w_out):
    L = h.shape[1]
    pos = jnp.arange(L)
    groups, bq, bk, bv, ga, gb = _mixer_inputs(h, w_in, pos)
    oa, la, bufs = [], [], []
    for (q, k, v), (window, dil) in zip(groups, DIL_PAIRS):
        o, lse = _dilated_prompt(q, k, v, window, dil)
        oa.append(o)
        la.append(lse)
        keep = min(window, L)
        bufs.append(jnp.stack([k[:, L - keep:], v[:, L - keep:]], axis=2))
    ob, lb = _band_attn(bq, bk, bv, WINDOW_B - 1)
    keep = min(WINDOW_B, L)
    bufs.append(jnp.stack([bk[:, L - keep:], bv[:, L - keep:]], axis=2))
    return _mixer_output(oa, la, ob, lb, ga, gb, sinks, w_ba, w_bb, w_out), bufs


def _mixer_sample(h, bufs_a, buf_b, w_in, sinks, w_ba, w_bb, w_out):
    T = h.shape[1]
    pos = PAST_LEN + jnp.arange(T)
    groups, bq, bk, bv, ga, gb = _mixer_inputs(h, w_in, pos)
    oa, la, new_bufs = [], [], []
    for (q, k, v), buf, (window, dil) in zip(groups, bufs_a, DIL_PAIRS):
        o, lse, nb = _dilated_sample(q, k, v, buf, window, dil)
        oa.append(o)
        la.append(lse)
        new_bufs.append(nb)
    ob, lb, nbb = _window_sample(bq, bk, bv, buf_b)
    new_bufs.append(nbb)
    return _mixer_output(oa, la, ob, lb, ga, gb, sinks, w_ba, w_bb, w_out), new_bufs


def setup_inputs(seed: int = 0) -> dict:
    key = jax.random.key(seed)
    ks = iter(jax.random.split(key, 24))
    f32 = jnp.float32

    def nrm(shape, scale):
        return jax.random.normal(next(ks), shape, f32) * scale

    def gain(shape):
        return 1.0 + 0.05 * jax.random.normal(next(ks), shape, f32)

    inp = {}
    inp["x_prompt"] = nrm((BATCH, SEQ, D_MODEL), 1.0)
    inp["x_sample"] = nrm((DEC_BATCH, DEC_SEQ, D_MODEL), 1.0)
    inp["cache_a1"] = nrm((DEPTH, DEC_BATCH, min(DIL_PAIRS[0][0], PAST_LEN), 2, HEADS_PER_GROUP_A, HEAD_DIM), 1.0)
    inp["cache_a2"] = nrm((DEPTH, DEC_BATCH, min(DIL_PAIRS[1][0], PAST_LEN), 2, HEADS_PER_GROUP_A, HEAD_DIM), 1.0)
    inp["cache_a3"] = nrm((DEPTH, DEC_BATCH, min(DIL_PAIRS[2][0], PAST_LEN), 2, HEADS_PER_GROUP_A, HEAD_DIM), 1.0)
    inp["cache_b"] = nrm((DEPTH, DEC_BATCH, min(WINDOW_B, PAST_LEN), 2, N_KV_B, HEAD_DIM), 1.0)
    inp["norm_ffn1"] = gain((DEPTH, D_MODEL))
    inp["ffn1_gate"] = nrm((DEPTH, D_MODEL, D_FF), D_MODEL ** -0.5)
    inp["ffn1_up"] = nrm((DEPTH, D_MODEL, D_FF), D_MODEL ** -0.5)
    inp["ffn1_down"] = nrm((DEPTH, D_FF, D_MODEL), D_FF ** -0.5)
    inp["norm_mix"] = gain((DEPTH, D_MODEL))
    inp["w_in"] = nrm((DEPTH, D_MODEL, IN_COLS), D_MODEL ** -0.5)
    inp["sinks"] = nrm((DEPTH, N_HEADS_B), 0.5)
    inp["w_branch_a"] = nrm((DEPTH, A_OUT, D_MODEL), A_OUT ** -0.5)
    inp["w_branch_b"] = nrm((DEPTH, B_Q, D_MODEL), B_Q ** -0.5)
    inp["w_out"] = nrm((DEPTH, D_MODEL, D_MODEL), D_MODEL ** -0.5)
    inp["norm_ffn2"] = gain((DEPTH, D_MODEL))
    inp["ffn2_gate"] = nrm((DEPTH, D_MODEL, D_FF), D_MODEL ** -0.5)
    inp["ffn2_up"] = nrm((DEPTH, D_MODEL, D_FF), D_MODEL ** -0.5)
    inp["ffn2_down"] = nrm((DEPTH, D_FF, D_MODEL), D_FF ** -0.5)
    inp["norm_final"] = gain((D_MODEL,))
    return inp


def reference(x_prompt, x_sample, cache_a1, cache_a2, cache_a3, cache_b,
              norm_ffn1, ffn1_gate, ffn1_up, ffn1_down,
              norm_mix, w_in, sinks, w_branch_a, w_branch_b, w_out,
              norm_ffn2, ffn2_gate, ffn2_up, ffn2_down, norm_final):
    xp, xs = x_prompt, x_sample
    pa1, pa2, pa3, pb = [], [], [], []
    sa1, sa2, sa3, sb = [], [], [], []
    for l in range(DEPTH):
        f1 = (ffn1_gate[l], ffn1_up[l], ffn1_down[l])
        f2 = (ffn2_gate[l], ffn2_up[l], ffn2_down[l])
        mw = (w_in[l], sinks[l], w_branch_a[l], w_branch_b[l], w_out[l])
        xp = xp + 0.5 * _swiglu(_rms(xp, norm_ffn1[l]), *f1)
        xs = xs + 0.5 * _swiglu(_rms(xs, norm_ffn1[l]), *f1)
        yp, bp = _mixer_prompt(_rms(xp, norm_mix[l]), *mw)
        ys, bs = _mixer_sample(_rms(xs, norm_mix[l]), (cache_a1[l], cache_a2[l], cache_a3[l]), cache_b[l], *mw)
        xp = xp + yp
        xs = xs + ys
        xp = xp + 0.5 * _swiglu(_rms(xp, norm_ffn2[l]), *f2)
        xs = xs + 0.5 * _swiglu(_rms(xs, norm_ffn2[l]), *f2)
        pa1.append(bp[0]); pa2.append(bp[1]); pa3.append(bp[2]); pb.append(bp[3])
        sa1.append(bs[0]); sa2.append(bs[1]); sa3.append(bs[2]); sb.append(bs[3])
    y_prompt = _rms(xp, norm_final)
    y_sample = _rms(xs, norm_final)
    return (y_prompt, y_sample,
            jnp.stack(pa1), jnp.stack(pa2), jnp.stack(pa3), jnp.stack(pb),
            jnp.stack(sa1), jnp.stack(sa2), jnp.stack(sa3), jnp.stack(sb))
```

```python
import functools

import jax
import jax.numpy as jnp
from jax import lax
from jax.experimental import pallas as pl
from jax.experimental.pallas import tpu as pltpu

PAST_LEN = 16384
HEAD_DIM = 64
ROT_DIM = HEAD_DIM // 4
ROT_HALF = ROT_DIM // 2
ROPE_THETA = 500000.0
DIL_PAIRS = ((128, 1), (512, 4), (2048, 16))
N_GROUPS_A = 3
HEADS_A = 4
A_OUT = HEADS_A * HEAD_DIM
N_HEADS_B = 8
N_KV_B = 2
WINDOW_B = 128
B_Q = N_HEADS_B * HEAD_DIM
B_KV = N_KV_B * HEAD_DIM
QKV_COLS = N_GROUPS_A * 3 * A_OUT + B_Q + 2 * B_KV
KV_ROWS_A = 2 * A_OUT
KV_ROWS_B = 2 * B_KV
KVT_ROWS = N_GROUPS_A * KV_ROWS_A + KV_ROWS_B
NORM_EPS = 1e-6
NEG = -1e30
SCALE = HEAD_DIM ** -0.5
BLK = 128
LANES = 128
TM = 512
B_HEAD_PERM = (0, 4, 1, 5, 2, 6, 3, 7)
V7X_SCOPED_VMEM_BYTES = 60000 * 1024

_BF16 = jnp.bfloat16
_F32 = jnp.float32


def _cparams(n_axes):
    return pltpu.CompilerParams(
        dimension_semantics=("arbitrary",) * n_axes,
        vmem_limit_bytes=V7X_SCOPED_VMEM_BYTES,
    )


def _resident(shape):
    nd = len(shape)
    return pl.BlockSpec(shape, lambda *_: (0,) * nd, pipeline_mode=pl.Buffered(1))


def _rms(x, g):
    return x * lax.rsqrt(jnp.mean(x * x, axis=-1, keepdims=True) + NORM_EPS) * g


def _sigmoid(x):
    return 1.0 / (1.0 + jnp.exp(-x))


def _dot(a, b):
    return jnp.dot(a, b, preferred_element_type=_F32)


def _dot_nt(a, b):
    return lax.dot_general(a, b, (((1,), (1,)), ((), ())), preferred_element_type=_F32)


def _ffn_kernel(x_ref, g_ref, wg_ref, wu_ref, wd_ref, gf_ref, o_ref, *, n_chunks, final_norm):
    x = x_ref[...]
    h = _rms(x, g_ref[...]).astype(_BF16)
    fc = wg_ref.shape[1] // n_chunks
    acc = jnp.zeros(x.shape, _F32)
    for c in range(n_chunks):
        sl = slice(c * fc, (c + 1) * fc)
        gate = _dot(h, wg_ref[:, sl])
        up = _dot(h, wu_ref[:, sl])
        act = (gate * _sigmoid(gate) * up).astype(_BF16)
        acc = acc + _dot(act, wd_ref[sl, :])
    y = x + 0.5 * acc
    if final_norm:
        y = _rms(y, gf_ref[...])
    o_ref[...] = y


def _ffn(x, g, wg, wu, wd, gf, *, final_norm):
    n_tok, d = x.shape
    f = wg.shape[1]
    n_chunks = 2 if f % (2 * LANES) == 0 else 1
    kern = functools.partial(_ffn_kernel, n_chunks=n_chunks, final_norm=final_norm)
    return pl.pallas_call(
        kern,
        out_shape=jax.ShapeDtypeStruct((n_tok, d), _F32),
        grid=(n_tok // TM,),
        in_specs=[
            pl.BlockSpec((TM, d), lambda i: (i, 0)),
            _resident((1, d)),
            _resident((d, f)),
            _resident((d, f)),
            _resident((f, d)),
            _resident((1, d)),
        ],
        out_specs=pl.BlockSpec((TM, d), lambda i: (i, 0)),
        compiler_params=_cparams(1),
        name="ffn_final" if final_norm else "ffn",
    )(x, g, wg, wu, wd, gf)


def _rope_nat(x, c, s_lo, s_hi):
    return x * c + pltpu.roll(x, ROT_HALF, 1) * s_hi + pltpu.roll(x, LANES - ROT_HALF, 1) * s_lo


def _rope_t(k_t, cos_t, sin_t, n_heads):
    parts = []
    for hh in range(n_heads):
        b = hh * HEAD_DIM
        x1 = k_t[b:b + ROT_HALF]
        x2 = k_t[b + ROT_HALF:b + ROT_DIM]
        parts += [x1 * cos_t - x2 * sin_t, x2 * cos_t + x1 * sin_t, k_t[b + ROT_DIM:b + HEAD_DIM]]
    return jnp.concatenate(parts, axis=0)


def _rope_tables(pos):
    inv = ROPE_THETA ** (-(jnp.arange(ROT_HALF, dtype=_F32) * 2.0 / ROT_DIM))
    ang = pos.astype(_F32)[:, None] * inv[None, :]
    return jnp.cos(ang), jnp.sin(ang)


def _nat_tables(cos, sin):
    p = cos.shape[0]
    one = jnp.ones((p, HEAD_DIM - ROT_DIM), _F32)
    zero = jnp.zeros((p, HEAD_DIM - ROT_DIM), _F32)
    zh = jnp.zeros((p, ROT_HALF), _F32)
    c = jnp.concatenate([cos, cos, one], axis=1)
    s_hi = jnp.concatenate([zh, sin, zero], axis=1)
    s_lo = jnp.concatenate([-sin, zh, zero], axis=1)
    rep = LANES // HEAD_DIM
    return tuple(jnp.tile(t, (1, rep)) for t in (c, s_lo, s_hi))


def _mix_in_prompt_kernel(x_ref, g_ref, w_ref, wt_ref, c_ref, slo_ref, shi_ref, ct_ref, st_ref,
                          q1_ref, k1_ref, v1_ref, q2_ref, k2_ref, v2_ref, q3_ref, k3_ref, v3_ref,
                          qb_ref, kb_ref, vb_ref, c1_ref, c2_ref, c3_ref, cb_ref, zs_ref):
    j = pl.program_id(1)
    tm = x_ref.shape[0]
    h = _rms(x_ref[...], g_ref[...]).astype(_BF16)
    c, s_lo, s_hi = c_ref[...], slo_ref[...], shi_ref[...]

    def proj(col0, width, kind):
        z = _dot(h, w_ref[:, col0:col0 + width])
        if kind == "v":
            return z
        slabs = []
        for s in range(width // LANES):
            zz = _rope_nat(z[:, s * LANES:(s + 1) * LANES], c, s_lo, s_hi)
            slabs.append(zz * SCALE if kind == "q" else zz)
        return jnp.concatenate(slabs, axis=1)

    for i, (ref, kind) in enumerate(((q1_ref, "q"), (k1_ref, "k"), (v1_ref, "v"))):
        ref[...] = proj(i * A_OUT, A_OUT, kind).astype(_BF16)
    for g, refs in ((1, (q2_ref, k2_ref, v2_ref)), (2, (q3_ref, k3_ref, v3_ref))):
        dil = DIL_PAIRS[g][1]
        rows = tm // dil
        for i, (ref, kind) in enumerate(zip(refs, ("q", "k", "v"))):
            z = proj((3 * g + i) * A_OUT, A_OUT, kind)
            for s in range(A_OUT // LANES):
                zs_ref[s] = z[:, s * LANES:(s + 1) * LANES]
            for r in range(dil):
                for s in range(A_OUT // LANES):
                    ref[r, :, s * LANES:(s + 1) * LANES] = (
                        zs_ref[s, pl.ds(r, rows, stride=dil), :].astype(_BF16))
    col_b = 3 * N_GROUPS_A * A_OUT
    qb_ref[...] = proj(col_b, B_Q, "q").astype(_BF16)
    kb_ref[...] = proj(col_b + B_Q, B_KV, "k").astype(_BF16)
    vb_ref[...] = proj(col_b + B_Q + B_KV, B_KV, "v").astype(_BF16)

    def kv_t(row0, n_rows, h_tok, cos_t, sin_t, n_heads):
        z = _dot_nt(wt_ref[row0:row0 + n_rows, :], h_tok)
        half = n_rows // 2
        return jnp.concatenate([_rope_t(z[:half], cos_t, sin_t, n_heads), z[half:]], axis=0)

    ct, st = ct_ref[...], st_ref[...]
    c3_ref[...] = kv_t(2 * KV_ROWS_A, KV_ROWS_A, h, ct, st, HEADS_A)

    @pl.when(j == pl.num_programs(1) - 1)
    def _():
        c2_ref[...] = kv_t(KV_ROWS_A, KV_ROWS_A, h, ct, st, HEADS_A)
        tail = tm - BLK
        c1_ref[...] = kv_t(0, KV_ROWS_A, h[tail:], ct[:, tail:], st[:, tail:], HEADS_A)
        cb_ref[...] = kv_t(3 * KV_ROWS_A, KV_ROWS_B, h[tail:], ct[:, tail:], st[:, tail:], N_KV_B)


def _mix_in_prompt(x, g, w_qkv, w_kvt, nat_tabs, t_tabs, n_batch, seq):
    d = x.shape[1]
    n_j = seq // TM
    row = lambda n, j: (n * n_j + j, 0)
    tok_spec = lambda w: pl.BlockSpec((TM, w), row)
    cls_spec = lambda dil: pl.BlockSpec((None, dil, TM // dil, A_OUT), lambda n, j: (n, 0, j, 0))
    nat = lambda w: jax.ShapeDtypeStruct((n_batch * seq, w), _BF16)
    cls = lambda dil: jax.ShapeDtypeStruct((n_batch, dil, seq // dil, A_OUT), _BF16)
    keep = [min(w, seq) for w, _ in DIL_PAIRS]
    out_shape = (
        [nat(A_OUT)] * 3 + [cls(4)] * 3 + [cls(16)] * 3 + [nat(B_Q), nat(B_KV), nat(B_KV)]
        + [jax.ShapeDtypeStruct((n_batch, KV_ROWS_A, keep[0]), _F32),
           jax.ShapeDtypeStruct((n_batch, KV_ROWS_A, keep[1]), _F32),
           jax.ShapeDtypeStruct((n_batch, KV_ROWS_A, keep[2]), _F32),
           jax.ShapeDtypeStruct((n_batch, KV_ROWS_B, min(WINDOW_B, seq)), _F32)])
    assert keep == [BLK, TM, seq] and seq % TM == 0
    out_specs = (
        [tok_spec(A_OUT)] * 3 + [cls_spec(4)] * 3 + [cls_spec(16)] * 3
        + [tok_spec(B_Q), tok_spec(B_KV), tok_spec(B_KV)]
        + [pl.BlockSpec((None, KV_ROWS_A, BLK), lambda n, j: (n, 0, 0)),
           pl.BlockSpec((None, KV_ROWS_A, TM), lambda n, j: (n, 0, 0)),
           pl.BlockSpec((None, KV_ROWS_A, TM), lambda n, j: (n, 0, j)),
           pl.BlockSpec((None, KV_ROWS_B, BLK), lambda n, j: (n, 0, 0))])
    tab_spec = pl.BlockSpec((TM, LANES), lambda n, j: (j, 0))
    ttab_spec = pl.BlockSpec((ROT_HALF, TM), lambda n, j: (0, j))
    return pl.pallas_call(
        _mix_in_prompt_kernel,
        out_shape=out_shape,
        grid=(n_batch, n_j),
        in_specs=[tok_spec(d), _resident((1, d)), _resident(w_qkv.shape), _resident(w_kvt.shape),
                  tab_spec, tab_spec, tab_spec, ttab_spec, ttab_spec],
        out_specs=out_specs,
        scratch_shapes=[pltpu.VMEM((A_OUT // LANES, TM, LANES), _F32)],
        compiler_params=_cparams(2),
        name="mix_in_prompt",
    )(x, g, w_qkv, w_kvt, *nat_tabs, *t_tabs)


def _mix_in_sample_kernel(x_ref, g_ref, w_ref, wt_ref, c_ref, slo_ref, shi_ref, ct_ref, st_ref,
                          q_ref, kvt_ref):
    h = _rms(x_ref[...], g_ref[...]).astype(_BF16)
    c, s_lo, s_hi = c_ref[...], slo_ref[...], shi_ref[...]
    z = _dot(h, w_ref[...])
    for s in range(z.shape[1] // LANES):
        sl = slice(s * LANES, (s + 1) * LANES)
        q_ref[:, sl] = _rope_nat(z[:, sl], c, s_lo, s_hi) * SCALE
    ct, st = ct_ref[...], st_ref[...]
    for g in range(N_GROUPS_A):
        r0 = g * KV_ROWS_A
        zt = _dot_nt(wt_ref[r0:r0 + KV_ROWS_A, :], h)
        kvt_ref[r0:r0 + A_OUT, :] = _rope_t(zt[:A_OUT], ct, st, HEADS_A)
        kvt_ref[r0 + A_OUT:r0 + KV_ROWS_A, :] = zt[A_OUT:]
    r0 = N_GROUPS_A * KV_ROWS_A
    zt = _dot_nt(wt_ref[r0:r0 + KV_ROWS_B, :], h)
    kvt_ref[r0:r0 + B_KV, :] = _rope_t(zt[:B_KV], ct, st, N_KV_B)
    kvt_ref[r0 + B_KV:r0 + KV_ROWS_B, :] = zt[B_KV:]


def _mix_in_sample(x, g, w_q, w_kvt, nat_tabs, t_tabs):
    n_tok, d = x.shape
    qw = w_q.shape[1]
    tab_spec = pl.BlockSpec((TM, LANES), lambda i: (0, 0))
    ttab_spec = pl.BlockSpec((ROT_HALF, TM), lambda i: (0, 0))
    return pl.pallas_call(
        _mix_in_sample_kernel,
        out_shape=[jax.ShapeDtypeStruct((n_tok, qw), _F32),
                   jax.ShapeDtypeStruct((KVT_ROWS, n_tok), _F32)],
        grid=(n_tok // TM,),
        in_specs=[pl.BlockSpec((TM, d), lambda i: (i, 0)), _resident((1, d)),
                  _resident(w_q.shape), _resident(w_kvt.shape),
                  tab_spec, tab_spec, tab_spec, ttab_spec, ttab_spec],
        out_specs=[pl.BlockSpec((TM, qw), lambda i: (i, 0)),
                   pl.BlockSpec((KVT_ROWS, TM), lambda i: (0, i))],
        compiler_params=_cparams(1),
        name="mix_in_sample",
    )(x, g, w_q, w_kvt, *nat_tabs, *t_tabs)


def _softmax_rows(s, valid):
    s = jnp.where(valid, s, NEG)
    m = jnp.max(s, axis=-1, keepdims=True)
    e = jnp.exp(s - m)
    return e, jnp.sum(e, axis=-1, keepdims=True), m


def _lane_head(t, width):
    return lax.shift_right_logical(lax.broadcasted_iota(jnp.int32, (t, width), 1),
                                   HEAD_DIM.bit_length() - 1)


def _head_stack_a(q):
    lane_head = _lane_head(q.shape[0], A_OUT)
    return jnp.concatenate([jnp.where(lane_head == hh, q, 0.0) for hh in range(HEADS_A)], axis=0)


def _head_unstack_a(o_st, col_st, t):
    lane_head = _lane_head(t, A_OUT)
    out = jnp.zeros((t, A_OUT), _F32)
    for hh in range(HEADS_A):
        col = col_st[hh * t:(hh + 1) * t]
        blk = jnp.broadcast_to(col, (t, A_OUT)) if o_st is None else o_st[hh * t:(hh + 1) * t] * col
        out = jnp.where(lane_head == hh, blk, out)
    return out


def _head_stack_b(q):
    t = q.shape[0]
    lo = lax.broadcasted_iota(jnp.int32, (t, LANES), 1) < HEAD_DIM
    parts = []
    for c in range(B_Q // LANES):
        qc = q[:, c * LANES:(c + 1) * LANES]
        parts += [jnp.where(lo, qc, 0.0), jnp.where(lo, 0.0, qc)]
    return jnp.concatenate(parts, axis=0)


def _head_unstack_b(o_st, col_st, t):
    lo = lax.broadcasted_iota(jnp.int32, (t, LANES), 1) < HEAD_DIM
    cols = []
    for c in range(B_Q // LANES):
        r0 = 2 * c * t
        a = o_st[r0:r0 + t] * col_st[r0:r0 + t]
        b = o_st[r0 + t:r0 + 2 * t] * col_st[r0 + t:r0 + 2 * t]
        cols.append(jnp.where(lo, a, b))
    return jnp.concatenate(cols, axis=1)


def _sink_rows(sink_ref, t):
    return jnp.concatenate(
        [jnp.full((t, 1), sink_ref[p], _F32) for p in range(N_HEADS_B)], axis=0)


def _band_valid(n_keys, n_back, has_prev):
    r = lax.broadcasted_iota(jnp.int32, (BLK, n_keys), 0)
    c = lax.broadcasted_iota(jnp.int32, (BLK, n_keys), 1)
    dist = (BLK + r - c) if has_prev else (r - c)
    return (dist >= 0) & (dist <= n_back)


def _attn_prompt_kernel(sink_ref, q1_ref, k1_ref, v1_ref, q2_ref, k2_ref, v2_ref,
                        q3_ref, k3_ref, v3_ref, qb_ref, kb_ref, vb_ref,
                        oa_ref, ob_ref, o_s, l_s):
    seq = q1_ref.shape[0]
    n_blk = seq // BLK

    def block_a(q, k, v, valid):
        qs = _head_stack_a(q.astype(_F32)).astype(_BF16)
        s = _dot_nt(qs, k)
        n_keys = k.shape[0]
        e, den, m = _softmax_rows(s.reshape(HEADS_A, BLK, n_keys), valid[None])
        e = e.reshape(HEADS_A * BLK, n_keys)
        den = den.reshape(HEADS_A * BLK, 1)
        m = m.reshape(HEADS_A * BLK, 1)
        o = _dot(e.astype(_BF16), v)
        out = _head_unstack_a(o, 1.0 / den, BLK)
        lse = _head_unstack_a(None, m + jnp.log(den), BLK)
        return out, lse

    def store_a(g, start, stride, out, lse):
        for s in range(A_OUT // LANES):
            sl = slice(s * LANES, (s + 1) * LANES)
            if stride == 1:
                idx = pl.ds(start, BLK)
            else:
                idx = pl.ds(start, BLK, stride=stride)
            o_s[g, s, idx, :] = out[:, sl]
            l_s[g, s, idx, :] = lse[:, sl]

    valid_first = _band_valid(BLK, BLK, False)
    valid_band = _band_valid(2 * BLK, BLK, True)

    out, lse = block_a(q1_ref[0:BLK], k1_ref[0:BLK], v1_ref[0:BLK], valid_first)
    store_a(0, 0, 1, out, lse)

    def g1_body(b, carry):
        r0 = pl.multiple_of(b * BLK, BLK)
        out, lse = block_a(q1_ref[pl.ds(r0, BLK)], k1_ref[pl.ds(r0 - BLK, 2 * BLK)],
                           v1_ref[pl.ds(r0 - BLK, 2 * BLK)], valid_band)
        store_a(0, r0, 1, out, lse)
        return carry

    lax.fori_loop(1, n_blk, g1_body, 0)

    dil = DIL_PAIRS[1][1]
    per_class = seq // dil // BLK

    def g2_body(i, carry):
        r = i // per_class
        sb = i % per_class
        s0 = pl.multiple_of(sb * BLK, BLK)
        tok0 = r + dil * s0

        @pl.when(sb == 0)
        def _():
            out, lse = block_a(q2_ref[r, pl.ds(0, BLK)], k2_ref[r, pl.ds(0, BLK)],
                               v2_ref[r, pl.ds(0, BLK)], valid_first)
            store_a(1, tok0, dil, out, lse)

        @pl.when(sb > 0)
        def _():
            out, lse = block_a(q2_ref[r, pl.ds(s0, BLK)], k2_ref[r, pl.ds(s0 - BLK, 2 * BLK)],
                               v2_ref[r, pl.ds(s0 - BLK, 2 * BLK)], valid_band)
            store_a(1, tok0, dil, out, lse)
        return carry

    lax.fori_loop(0, dil * per_class, g2_body, 0)

    dil3 = DIL_PAIRS[2][1]

    def g3_body(r, carry):
        out, lse = block_a(q3_ref[r], k3_ref[r], v3_ref[r], valid_first)
        store_a(2, r, dil3, out, lse)
        return carry

    lax.fori_loop(0, dil3, g3_body, 0)

    def comb_body(b, carry):
        r0 = pl.multiple_of(b * BLK, BLK)
        cols = []
        for s in range(A_OUT // LANES):
            ls = [l_s[g, s, pl.ds(r0, BLK), :] for g in range(N_GROUPS_A)]
            os_ = [o_s[g, s, pl.ds(r0, BLK), :] for g in range(N_GROUPS_A)]
            m = jnp.maximum(jnp.maximum(ls[0], ls[1]), ls[2])
            ws = [jnp.exp(l - m) for l in ls]
            num = ws[0] * os_[0] + ws[1] * os_[1] + ws[2] * os_[2]
            cols.append(num / (ws[0] + ws[1] + ws[2]))
        oa_ref[pl.ds(r0, BLK), :] = jnp.concatenate(cols, axis=1).astype(oa_ref.dtype)
        return carry

    lax.fori_loop(0, n_blk, comb_body, 0)

    sink_col = _sink_rows(sink_ref, BLK)
    validb_first = _band_valid(BLK, WINDOW_B - 1, False)
    validb_band = _band_valid(2 * BLK, WINDOW_B - 1, True)

    def block_b(q, k, v, valid):
        qs = _head_stack_b(q.astype(_F32)).astype(_BF16)
        s = _dot_nt(qs, k)
        n_keys = k.shape[0]
        e, den, m = _softmax_rows(s.reshape(N_HEADS_B, BLK, n_keys), valid[None])
        e = e.reshape(N_HEADS_B * BLK, n_keys)
        den = den.reshape(N_HEADS_B * BLK, 1)
        m = m.reshape(N_HEADS_B * BLK, 1)
        o = _dot(e.astype(_BF16), v)
        scale = _sigmoid(m + jnp.log(den) - sink_col) / den
        return _head_unstack_b(o, scale, BLK)

    ob_ref[0:BLK, :] = block_b(qb_ref[0:BLK], kb_ref[0:BLK], vb_ref[0:BLK],
                               validb_first).astype(ob_ref.dtype)

    def b_body(b, carry):
        r0 = pl.multiple_of(b * BLK, BLK)
        out = block_b(qb_ref[pl.ds(r0, BLK)], kb_ref[pl.ds(r0 - BLK, 2 * BLK)],
                      vb_ref[pl.ds(r0 - BLK, 2 * BLK)], validb_band)
        ob_ref[pl.ds(r0, BLK), :] = out.astype(ob_ref.dtype)
        return carry

    lax.fori_loop(1, n_blk, b_body, 0)


def _attn_prompt(sinks_p, qkv, n_batch, seq):
    (q1, k1, v1, q2, k2, v2, q3, k3, v3, qb, kb, vb) = qkv
    tok = lambda w: pl.BlockSpec((seq, w), lambda n: (n, 0))
    cls = lambda dil: pl.BlockSpec((None, dil, seq // dil, A_OUT), lambda n: (n, 0, 0, 0))
    return pl.pallas_call(
        _attn_prompt_kernel,
        out_shape=[jax.ShapeDtypeStruct((n_batch * seq, A_OUT), _BF16),
                   jax.ShapeDtypeStruct((n_batch * seq, B_Q), _BF16)],
        grid=(n_batch,),
        in_specs=[pl.BlockSpec(memory_space=pltpu.SMEM)]
        + [tok(A_OUT)] * 3 + [cls(4)] * 3 + [cls(16)] * 3 + [tok(B_Q), tok(B_KV), tok(B_KV)],
        out_specs=[tok(A_OUT), tok(B_Q)],
        scratch_shapes=[pltpu.VMEM((N_GROUPS_A, A_OUT // LANES, seq, LANES), _F32),
                        pltpu.VMEM((N_GROUPS_A, A_OUT // LANES, seq, LANES), _F32)],
        compiler_params=_cparams(1),
        name="attn_prompt",
    )(sinks_p, q1, k1, v1, q2, k2, v2, q3, k3, v3, qb, kb, vb)


def _attn_sample_kernel(sink_ref, q_ref, kvt_ref, c1_ref, c2_ref, c3_ref, cb_ref, *rest, t_new):
    oa_ref, ob_ref, n1_ref, n2_ref, n3_ref, nb_ref = rest[-6:]
    n = pl.program_id(0)
    per_tile = LANES // t_new
    first_new = LANES - t_new
    shift = first_new - (n & (per_tile - 1)) * t_new
    new_t = pltpu.roll(kvt_ref[...], shift, 1)
    q = q_ref[...]

    def new_cache(old_ref, new_rows, out_ref):
        lb = old_ref.shape[1]
        rolled = pltpu.roll(old_ref[...], lb - t_new, 1)
        if lb > LANES:
            out_ref[:, :lb - LANES] = rolled[:, :lb - LANES]
        lane = lax.broadcasted_iota(jnp.int32, (old_ref.shape[0], LANES), 1)
        out_ref[:, lb - LANES:] = jnp.where(lane >= first_new, new_rows, rolled[:, lb - LANES:])

    def masks(rows, lb, window, dil, strict):
        t_old = lax.broadcasted_iota(jnp.int32, (rows, lb), 0) & (t_new - 1)
        dist = lb + t_old - lax.broadcasted_iota(jnp.int32, (rows, lb), 1)
        hi = (dist < window) if strict else (dist <= window)
        valid_old = (dist >= 0) & hi & ((dist & (dil - 1)) == 0)
        t_nw = lax.broadcasted_iota(jnp.int32, (rows, LANES), 0) & (t_new - 1)
        c_nw = lax.broadcasted_iota(jnp.int32, (rows, LANES), 1) - first_new
        dn = t_nw - c_nw
        hin = (dn < window) if strict else (dn <= window)
        valid_new = (c_nw >= 0) & (dn >= 0) & hin & ((dn & (dil - 1)) == 0)
        return valid_old, valid_new

    def attend(qs, k_old, v_old, k_new, v_new, valid_old, valid_new):
        s_old = jnp.where(valid_old, _dot(qs, k_old), NEG)
        s_new = jnp.where(valid_new, _dot(qs, k_new), NEG)
        m = jnp.maximum(jnp.max(s_old, axis=-1, keepdims=True),
                        jnp.max(s_new, axis=-1, keepdims=True))
        e_old = jnp.exp(s_old - m)
        e_new = jnp.exp(s_new - m)
        den = jnp.sum(e_old, axis=-1, keepdims=True) + jnp.sum(e_new, axis=-1, keepdims=True)
        o = _dot_nt(e_old.astype(_BF16), v_old) + _dot_nt(e_new.astype(_BF16), v_new)
        return o, den, m

    outs, lses = [], []
    for g, (old_ref, out_ref) in enumerate(((c1_ref, n1_ref), (c2_ref, n2_ref), (c3_ref, n3_ref))):
        window, dil = DIL_PAIRS[g]
        lb = old_ref.shape[1]
        new_g = new_t[g * KV_ROWS_A:(g + 1) * KV_ROWS_A]
        qs = _head_stack_a(q[:, g * A_OUT:(g + 1) * A_OUT]).astype(_BF16)
        valid_old, valid_new = masks(HEADS_A * t_new, lb, window, dil, False)
        o, den, m = attend(qs, old_ref[0:A_OUT, :].astype(_BF16), old_ref[A_OUT:, :].astype(_BF16),
                           new_g[:A_OUT].astype(_BF16), new_g[A_OUT:].astype(_BF16),
                           valid_old, valid_new)
        outs.append(_head_unstack_a(o, 1.0 / den, t_new))
        lses.append(_head_unstack_a(None, m + jnp.log(den), t_new))
        new_cache(old_ref, new_g, out_ref)
    m = jnp.maximum(jnp.maximum(lses[0], lses[1]), lses[2])
    ws = [jnp.exp(l - m) for l in lses]
    oa_ref[...] = (ws[0] * outs[0] + ws[1] * outs[1] + ws[2] * outs[2]) / (ws[0] + ws[1] + ws[2])

    lb = cb_ref.shape[1]
    new_b = new_t[N_GROUPS_A * KV_ROWS_A:]
    qs = _head_stack_b(q[:, N_GROUPS_A * A_OUT:]).astype(_BF16)
    valid_old, valid_new = masks(N_HEADS_B * t_new, lb, WINDOW_B, 1, True)
    o, den, m = attend(qs, cb_ref[0:B_KV, :].astype(_BF16), cb_ref[B_KV:, :].astype(_BF16),
                       new_b[:B_KV].astype(_BF16), new_b[B_KV:].astype(_BF16),
                       valid_old, valid_new)
    scale = _sigmoid(m + jnp.log(den) - _sink_rows(sink_ref, t_new)) / den
    ob_ref[...] = _head_unstack_b(o, scale, t_new)
    new_cache(cb_ref, new_b, nb_ref)


def _attn_sample(layer, sinks_p, q_nat, kvt_new, caches, prev_out, t_new):
    n_batch = caches[0].shape[1]
    per_tile = LANES // t_new
    cache_spec = lambda c: pl.BlockSpec((None, None) + c.shape[2:], lambda n: (layer, n, 0, 0))
    any_spec = pl.BlockSpec(memory_space=pl.ANY)
    prev_out = [] if prev_out is None else list(prev_out)
    aliases = {3 + len(caches) + i: 2 + i for i in range(len(prev_out))}
    kern = functools.partial(_attn_sample_kernel, t_new=t_new)
    return pl.pallas_call(
        kern,
        out_shape=[jax.ShapeDtypeStruct((n_batch * t_new, A_OUT), _F32),
                   jax.ShapeDtypeStruct((n_batch * t_new, B_Q), _F32)]
        + [jax.ShapeDtypeStruct(c.shape, c.dtype) for c in caches],
        grid=(n_batch,),
        in_specs=[pl.BlockSpec(memory_space=pltpu.SMEM),
                  pl.BlockSpec((t_new, q_nat.shape[1]), lambda n: (n, 0)),
                  pl.BlockSpec((KVT_ROWS, LANES), lambda n: (0, n // per_tile))]
        + [cache_spec(c) for c in caches] + [any_spec] * len(prev_out),
        out_specs=[pl.BlockSpec((t_new, A_OUT), lambda n: (n, 0)),
                   pl.BlockSpec((t_new, B_Q), lambda n: (n, 0))]
        + [cache_spec(c) for c in caches],
        input_output_aliases=aliases,
        compiler_params=_cparams(1),
        name="attn_sample",
    )(sinks_p, q_nat, kvt_new, *caches, *prev_out)


def _mix_out_kernel(x_ref, g_ref, oa_ref, ob_ref, wgate_ref, wba_ref, wbb_ref, wout_ref, o_ref):
    x = x_ref[...]
    d = x.shape[1]
    h = _rms(x, g_ref[...]).astype(_BF16)
    gate_a = _dot(h, wgate_ref[:, :d])
    gate_b = _dot(h, wgate_ref[:, d:])
    ya = _dot(oa_ref[...].astype(_BF16), wba_ref[...])
    yb = _dot(ob_ref[...].astype(_BF16), wbb_ref[...])
    merged = _sigmoid(gate_a) * ya + _sigmoid(gate_b) * yb
    o_ref[...] = x + _dot(merged.astype(_BF16), wout_ref[...])


def _mix_out(x, g, oa, ob, w_gate, w_ba, w_bb, w_out):
    n_tok, d = x.shape
    tok = lambda w: pl.BlockSpec((TM, w), lambda i: (i, 0))
    return pl.pallas_call(
        _mix_out_kernel,
        out_shape=jax.ShapeDtypeStruct((n_tok, d), _F32),
        grid=(n_tok // TM,),
        in_specs=[tok(d), _resident((1, d)), tok(A_OUT), tok(B_Q), _resident(w_gate.shape),
                  _resident(w_ba.shape), _resident(w_bb.shape), _resident(w_out.shape)],
        out_specs=tok(d),
        compiler_params=_cparams(1),
        name="mix_out",
    )(x, g, oa, ob, w_gate, w_ba, w_bb, w_out)


def _cache_view(c):
    dep, nb, rows = c.shape[:3]
    return jnp.transpose(c, (0, 1, 3, 4, 5, 2)).reshape(dep, nb, -1, rows)


def _cache_unview(c, heads):
    dep, nb, _, rows = c.shape
    return jnp.transpose(c.reshape(dep, nb, 2, heads, HEAD_DIM, rows), (0, 1, 5, 2, 3, 4))


def kernel(x_prompt, x_sample, cache_a1, cache_a2, cache_a3, cache_b, norm_ffn1, ffn1_gate, ffn1_up, ffn1_down, norm_mix, w_in, sinks, w_branch_a, w_branch_b, w_out, norm_ffn2, ffn2_gate, ffn2_up, ffn2_down, norm_final):
    n_batch, seq, d = x_prompt.shape
    dec_batch, t_new, _ = x_sample.shape
    depth = w_in.shape[0]
    assert seq == DIL_PAIRS[2][0] and LANES % t_new == 0 and (dec_batch * t_new) % TM == 0
    assert TM % t_new == 0 and dec_batch % (LANES // t_new) == 0

    perm = jnp.array(B_HEAD_PERM)
    col_b = 3 * N_GROUPS_A * A_OUT

    cos_p, sin_p = _rope_tables(jnp.arange(seq))
    cos_s, sin_s = _rope_tables(PAST_LEN + jnp.arange(t_new))
    cos_s, sin_s = jnp.tile(cos_s, (TM // t_new, 1)), jnp.tile(sin_s, (TM // t_new, 1))
    nat_p, nat_s = _nat_tables(cos_p, sin_p), _nat_tables(cos_s, sin_s)
    t_p, t_s = (cos_p.T, sin_p.T), (cos_s.T, sin_s.T)

    caches = [_cache_view(c) for c in (cache_a1, cache_a2, cache_a3, cache_b)]

    xp = x_prompt.reshape(n_batch * seq, d)
    xs = x_sample.reshape(dec_batch * t_new, d)
    row = lambda v: v.reshape(1, d)
    gf = row(norm_final)
    prompt_bufs = []
    sample_bufs = None
    for l in range(depth):
        last = l == depth - 1
        bf = lambda w: w.astype(_BF16)
        w_l = w_in[l]
        bq = w_l[:, col_b:col_b + B_Q].reshape(d, N_HEADS_B, HEAD_DIM)[:, perm].reshape(d, B_Q)
        w_qkv = bf(jnp.concatenate([w_l[:, :col_b], bq, w_l[:, col_b + B_Q:QKV_COLS]], axis=1))
        kv_cols = [w_l[:, (3 * g + 1) * A_OUT:(3 * g + 3) * A_OUT] for g in range(N_GROUPS_A)]
        kv_cols.append(w_l[:, col_b + B_Q:QKV_COLS])
        w_kvt = bf(jnp.concatenate(kv_cols, axis=1).T)
        w_q = bf(jnp.concatenate([w_l[:, 3 * g * A_OUT:(3 * g + 1) * A_OUT]
                                  for g in range(N_GROUPS_A)] + [bq], axis=1))
        w_gate = bf(w_l[:, QKV_COLS:])
        w_bb = bf(w_branch_b[l].reshape(N_HEADS_B, HEAD_DIM, d)[perm].reshape(B_Q, d))
        w_ba, w_o = bf(w_branch_a[l]), bf(w_out[l])
        sinks_p = sinks[l][perm]
        f1 = (row(norm_ffn1[l]), bf(ffn1_gate[l]), bf(ffn1_up[l]), bf(ffn1_down[l]))
        f2 = (row(norm_ffn2[l]), bf(ffn2_gate[l]), bf(ffn2_up[l]), bf(ffn2_down[l]))
        g_mix = row(norm_mix[l])

        xp = _ffn(xp, *f1, gf, final_norm=False)
        xs = _ffn(xs, *f1, gf, final_norm=False)

        outs = _mix_in_prompt(xp, g_mix, w_qkv, w_kvt, nat_p, t_p, n_batch, seq)
        prompt_bufs.append(outs[12:])
        oa_p, ob_p = _attn_prompt(sinks_p, outs[:12], n_batch, seq)

        q_nat, kvt_new = _mix_in_sample(xs, g_mix, w_q, w_kvt, nat_s, t_s)
        s_outs = _attn_sample(l, sinks_p, q_nat, kvt_new, caches, sample_bufs, t_new)
        oa_s, ob_s, sample_bufs = s_outs[0], s_outs[1], list(s_outs[2:])

        xp = _mix_out(xp, g_mix, oa_p, ob_p, w_gate, w_ba, w_bb, w_o)
        xs = _mix_out(xs, g_mix, oa_s, ob_s, w_gate, w_ba, w_bb, w_o)
        xp = _ffn(xp, *f2, gf, final_norm=last)
        xs = _ffn(xs, *f2, gf, final_norm=last)

    heads = (HEADS_A, HEADS_A, HEADS_A, N_KV_B)
    new_p = [_cache_unview(jnp.stack([prompt_bufs[l][i] for l in range(depth)]), heads[i])
             for i in range(4)]
    new_s = [_cache_unview(sample_bufs[i], heads[i]) for i in range(4)]
    return (xp.reshape(n_batch, seq, d), xs.reshape(dec_batch, t_new, d), *new_p, *new_s)
```

```python
import functools

import jax
import jax.numpy as jnp
from jax import lax
from jax.experimental import pallas as pl
from jax.experimental.pallas import tpu as pltpu

PAST_LEN = 16384
HEAD_DIM = 64
ROT_DIM = HEAD_DIM // 4
ROT_HALF = ROT_DIM // 2
ROPE_THETA = 500000.0
DIL_PAIRS = ((128, 1), (512, 4), (2048, 16))
N_GROUPS_A = 3
HEADS_A = 4
A_OUT = HEADS_A * HEAD_DIM
N_HEADS_B = 8
N_KV_B = 2
WINDOW_B = 128
B_Q = N_HEADS_B * HEAD_DIM
B_KV = N_KV_B * HEAD_DIM
QKV_COLS = N_GROUPS_A * 3 * A_OUT + B_Q + 2 * B_KV
KV_ROWS_A = 2 * A_OUT
KV_ROWS_B = 2 * B_KV
KVT_ROWS = N_GROUPS_A * KV_ROWS_A + KV_ROWS_B
NORM_EPS = 1e-6
NEG = -1e30
SCALE = HEAD_DIM ** -0.5
BLK = 128
LANES = 128
TM = 512
B_HEAD_PERM = (0, 4, 1, 5, 2, 6, 3, 7)
V7X_SCOPED_VMEM_BYTES = 60000 * 1024

_BF16 = jnp.bfloat16
_F32 = jnp.float32


def _cparams(n_axes):
    return pltpu.CompilerParams(
        dimension_semantics=("arbitrary",) * n_axes,
        vmem_limit_bytes=V7X_SCOPED_VMEM_BYTES,
    )


def _resident(shape):
    nd = len(shape)
    return pl.BlockSpec(shape, lambda *_: (0,) * nd, pipeline_mode=pl.Buffered(1))


def _rms(x, g):
    return x * lax.rsqrt(jnp.mean(x * x, axis=-1, keepdims=True) + NORM_EPS) * g


def _sigmoid(x):
    return 1.0 / (1.0 + jnp.exp(-x))


def _dot(a, b):
    return jnp.dot(a, b, preferred_element_type=_F32)


def _dot_nt(a, b):
    return lax.dot_general(a, b, (((1,), (1,)), ((), ())), preferred_element_type=_F32)


def _ffn_kernel(x_ref, g_ref, wg_ref, wu_ref, wd_ref, gf_ref, o_ref, *, n_chunks, final_norm):
    x = x_ref[...]
    h = _rms(x, g_ref[...]).astype(_BF16)
    fc = wg_ref.shape[1] // n_chunks
    acc = jnp.zeros(x.shape, _F32)
    for c in range(n_chunks):
        sl = slice(c * fc, (c + 1) * fc)
        gate = _dot(h, wg_ref[:, sl])
        up = _dot(h, wu_ref[:, sl])
        act = (gate * _sigmoid(gate) * up).astype(_BF16)
        acc = acc + _dot(act, wd_ref[sl, :])
    y = x + 0.5 * acc
    if final_norm:
        y = _rms(y, gf_ref[...])
    o_ref[...] = y


def _ffn(x, g, wg, wu, wd, gf, *, final_norm):
    n_tok, d = x.shape
    f = wg.shape[1]
    n_chunks = 2 if f % (2 * LANES) == 0 else 1
    kern = functools.partial(_ffn_kernel, n_chunks=n_chunks, final_norm=final_norm)
    return pl.pallas_call(
        kern,
        out_shape=jax.ShapeDtypeStruct((n_tok, d), _F32),
        grid=(n_tok // TM,),
        in_specs=[
            pl.BlockSpec((TM, d), lambda i: (i, 0)),
            _resident((1, d)),
            _resident((d, f)),
            _resident((d, f)),
            _resident((f, d)),
            _resident((1, d)),
        ],
        out_specs=pl.BlockSpec((TM, d), lambda i: (i, 0)),
        compiler_params=_cparams(1),
        name="ffn_final" if final_norm else "ffn",
    )(x, g, wg, wu, wd, gf)


def _rope_nat(x, c, s_lo, s_hi):
    return x * c + pltpu.roll(x, ROT_HALF, 1) * s_hi + pltpu.roll(x, LANES - ROT_HALF, 1) * s_lo


def _rope_t(k_t, cos_t, sin_t, n_heads):
    parts = []
    for hh in range(n_heads):
        b = hh * HEAD_DIM
        x1 = k_t[b:b + ROT_HALF]
        x2 = k_t[b + ROT_HALF:b + ROT_DIM]
        parts += [x1 * cos_t - x2 * sin_t, x2 * cos_t + x1 * sin_t, k_t[b + ROT_DIM:b + HEAD_DIM]]
    return jnp.concatenate(parts, axis=0)


def _rope_tables(pos):
    inv = ROPE_THETA ** (-(jnp.arange(ROT_HALF, dtype=_F32) * 2.0 / ROT_DIM))
    ang = pos.astype(_F32)[:, None] * inv[None, :]
    return jnp.cos(ang), jnp.sin(ang)


def _nat_tables(cos, sin):
    p = cos.shape[0]
    one = jnp.ones((p, HEAD_DIM - ROT_DIM), _F32)
    zero = jnp.zeros((p, HEAD_DIM - ROT_DIM), _F32)
    zh = jnp.zeros((p, ROT_HALF), _F32)
    c = jnp.concatenate([cos, cos, one], axis=1)
    s_hi = jnp.concatenate([zh, sin, zero], axis=1)
    s_lo = jnp.concatenate([-sin, zh, zero], axis=1)
    rep = LANES // HEAD_DIM
    return tuple(jnp.tile(t, (1, rep)) for t in (c, s_lo, s_hi))


def _mix_in_prompt_kernel(x_ref, g_ref, w_ref, wt_ref, c_ref, slo_ref, shi_ref, ct_ref, st_ref,
                          q1_ref, k1_ref, v1_ref, q2_ref, k2_ref, v2_ref, q3_ref, k3_ref, v3_ref,
                          qb_ref, kb_ref, vb_ref, c1_ref, c2_ref, c3_ref, cb_ref, zs_ref):
    j = pl.program_id(1)
    tm = x_ref.shape[0]
    h = _rms(x_ref[...], g_ref[...]).astype(_BF16)
    c, s_lo, s_hi = c_ref[...], slo_ref[...], shi_ref[...]

    def proj(col0, width, kind):
        z = _dot(h, w_ref[:, col0:col0 + width])
        if kind == "v":
            return z
        slabs = []
        for s in range(width // LANES):
            zz = _rope_nat(z[:, s * LANES:(s + 1) * LANES], c, s_lo, s_hi)
            slabs.append(zz * SCALE if kind == "q" else zz)
        return jnp.concatenate(slabs, axis=1)

    for i, (ref, kind) in enumerate(((q1_ref, "q"), (k1_ref, "k"), (v1_ref, "v"))):
        ref[...] = proj(i * A_OUT, A_OUT, kind).astype(_BF16)
    for g, refs in ((1, (q2_ref, k2_ref, v2_ref)), (2, (q3_ref, k3_ref, v3_ref))):
        dil = DIL_PAIRS[g][1]
        rows = tm // dil
        for i, (ref, kind) in enumerate(zip(refs, ("q", "k", "v"))):
            z = proj((3 * g + i) * A_OUT, A_OUT, kind)
            for s in range(A_OUT // LANES):
                zs_ref[s] = z[:, s * LANES:(s + 1) * LANES]
            for r in range(dil):
                for s in range(A_OUT // LANES):
                    ref[r, :, s * LANES:(s + 1) * LANES] = (
                        zs_ref[s, pl.ds(r, rows, stride=dil), :].astype(_BF16))
    col_b = 3 * N_GROUPS_A * A_OUT
    qb_ref[...] = proj(col_b, B_Q, "q").astype(_BF16)
    kb_ref[...] = proj(col_b + B_Q, B_KV, "k").astype(_BF16)
    vb_ref[...] = proj(col_b + B_Q + B_KV, B_KV, "v").astype(_BF16)

    def kv_t(row0, n_rows, h_tok, cos_t, sin_t, n_heads):
        z = _dot_nt(wt_ref[row0:row0 + n_rows, :], h_tok)
        half = n_rows // 2
        return jnp.concatenate([_rope_t(z[:half], cos_t, sin_t, n_heads), z[half:]], axis=0)

    ct, st = ct_ref[...], st_ref[...]
    c3_ref[...] = kv_t(2 * KV_ROWS_A, KV_ROWS_A, h, ct, st, HEADS_A)

    @pl.when(j == pl.num_programs(1) - 1)
    def _():
        c2_ref[...] = kv_t(KV_ROWS_A, KV_ROWS_A, h, ct, st, HEADS_A)
        tail = tm - BLK
        c1_ref[...] = kv_t(0, KV_ROWS_A, h[tail:], ct[:, tail:], st[:, tail:], HEADS_A)
        cb_ref[...] = kv_t(3 * KV_ROWS_A, KV_ROWS_B, h[tail:], ct[:, tail:], st[:, tail:], N_KV_B)


def _mix_in_prompt(x, g, w_qkv, w_kvt, nat_tabs, t_tabs, n_batch, seq):
    d = x.shape[1]
    n_j = seq // TM
    row = lambda n, j: (n * n_j + j, 0)
    tok_spec = lambda w: pl.BlockSpec((TM, w), row)
    cls_spec = lambda dil: pl.BlockSpec((None, dil, TM // dil, A_OUT), lambda n, j: (n, 0, j, 0))
    nat = lambda w: jax.ShapeDtypeStruct((n_batch * seq, w), _BF16)
    cls = lambda dil: jax.ShapeDtypeStruct((n_batch, dil, seq // dil, A_OUT), _BF16)
    keep = [min(w, seq) for w, _ in DIL_PAIRS]
    out_shape = (
        [nat(A_OUT)] * 3 + [cls(4)] * 3 + [cls(16)] * 3 + [nat(B_Q), nat(B_KV), nat(B_KV)]
        + [jax.ShapeDtypeStruct((n_batch, KV_ROWS_A, keep[0]), _F32),
           jax.ShapeDtypeStruct((n_batch, KV_ROWS_A, keep[1]), _F32),
           jax.ShapeDtypeStruct((n_batch, KV_ROWS_A, keep[2]), _F32),
           jax.ShapeDtypeStruct((n_batch, KV_ROWS_B, min(WINDOW_B, seq)), _F32)])
    assert keep == [BLK, TM, seq] and seq % TM == 0
    out_specs = (
        [tok_spec(A_OUT)] * 3 + [cls_spec(4)] * 3 + [cls_spec(16)] * 3
        + [tok_spec(B_Q), tok_spec(B_KV), tok_spec(B_KV)]
        + [pl.BlockSpec((None, KV_ROWS_A, BLK), lambda n, j: (n, 0, 0)),
           pl.BlockSpec((None, KV_ROWS_A, TM), lambda n, j: (n, 0, 0)),
           pl.BlockSpec((None, KV_ROWS_A, TM), lambda n, j: (n, 0, j)),
           pl.BlockSpec((None, KV_ROWS_B, BLK), lambda n, j: (n, 0, 0))])
    tab_spec = pl.BlockSpec((TM, LANES), lambda n, j: (j, 0))
    ttab_spec = pl.BlockSpec((ROT_HALF, TM), lambda n, j: (0, j))
    return pl.pallas_call(
        _mix_in_prompt_kernel,
        out_shape=out_shape,
        grid=(n_batch, n_j),
        in_specs=[tok_spec(d), _resident((1, d)), _resident(w_qkv.shape), _resident(w_kvt.shape),
                  tab_spec, tab_spec, tab_spec, ttab_spec, ttab_spec],
        out_specs=out_specs,
        scratch_shapes=[pltpu.VMEM((A_OUT // LANES, TM, LANES), _F32)],
        compiler_params=_cparams(2),
        name="mix_in_prompt",
    )(x, g, w_qkv, w_kvt, *nat_tabs, *t_tabs)


def _mix_in_sample_kernel(x_ref, g_ref, w_ref, wt_ref, c_ref, slo_ref, shi_ref, ct_ref, st_ref,
                          q_ref, kvt_ref):
    h = _rms(x_ref[...], g_ref[...]).astype(_BF16)
    c, s_lo, s_hi = c_ref[...], slo_ref[...], shi_ref[...]
    z = _dot(h, w_ref[...])
    for s in range(z.shape[1] // LANES):
        sl = slice(s * LANES, (s + 1) * LANES)
        q_ref[:, sl] = _rope_nat(z[:, sl], c, s_lo, s_hi) * SCALE
    ct, st = ct_ref[...], st_ref[...]
    for g in range(N_GROUPS_A):
        r0 = g * KV_ROWS_A
        zt = _dot_nt(wt_ref[r0:r0 + KV_ROWS_A, :], h)
        kvt_ref[r0:r0 + A_OUT, :] = _rope_t(zt[:A_OUT], ct, st, HEADS_A)
        kvt_ref[r0 + A_OUT:r0 + KV_ROWS_A, :] = zt[A_OUT:]
    r0 = N_GROUPS_A * KV_ROWS_A
    zt = _dot_nt(wt_ref[r0:r0 + KV_ROWS_B, :], h)
    kvt_ref[r0:r0 + B_KV, :] = _rope_t(zt[:B_KV], ct, st, N_KV_B)
    kvt_ref[r0 + B_KV:r0 + KV_ROWS_B, :] = zt[B_KV:]


def _mix_in_sample(x, g, w_q, w_kvt, nat_tabs, t_tabs):
    n_tok, d = x.shape
    qw = w_q.shape[1]
    tab_spec = pl.BlockSpec((TM, LANES), lambda i: (0, 0))
    ttab_spec = pl.BlockSpec((ROT_HALF, TM), lambda i: (0, 0))
    return pl.pallas_call(
        _mix_in_sample_kernel,
        out_shape=[jax.ShapeDtypeStruct((n_tok, qw), _F32),
                   jax.ShapeDtypeStruct((KVT_ROWS, n_tok), _F32)],
        grid=(n_tok // TM,),
        in_specs=[pl.BlockSpec((TM, d), lambda i: (i, 0)), _resident((1, d)),
                  _resident(w_q.shape), _resident(w_kvt.shape),
                  tab_spec, tab_spec, tab_spec, ttab_spec, ttab_spec],
        out_specs=[pl.BlockSpec((TM, qw), lambda i: (i, 0)),
                   pl.BlockSpec((KVT_ROWS, TM), lambda i: (0, i))],
        compiler_params=_cparams(1),
        name="mix_in_sample",
    )(x, g, w_q, w_kvt, *nat_tabs, *t_tabs)


def _softmax_rows(s, valid):
    s = jnp.where(valid, s, NEG)
    m = jnp.max(s, axis=-1, keepdims=True)
    e = jnp.exp(s - m)
    return e, jnp.sum(e, axis=-1, keepdims=True), m


def _lane_head(t, width):
    return lax.shift_right_logical(lax.broadcasted_iota(jnp.int32, (t, width), 1),
                                   HEAD_DIM.bit_length() - 1)


def _head_stack_a(q):
    lane_head = _lane_head(q.shape[0], A_OUT)
    return jnp.concatenate([jnp.where(lane_head == hh, q, 0.0) for hh in range(HEADS_A)], axis=0)


def _head_unstack_a(o_st, col_st, t):
    lane_head = _lane_head(t, A_OUT)
    out = jnp.zeros((t, A_OUT), _F32)
    for hh in range(HEADS_A):
        col = col_st[hh * t:(hh + 1) * t]
        blk = jnp.broadcast_to(col, (t, A_OUT)) if o_st is None else o_st[hh * t:(hh + 1) * t] * col
        out = jnp.where(lane_head == hh, blk, out)
    return out


def _head_stack_b(q):
    t = q.shape[0]
    lo = lax.broadcasted_iota(jnp.int32, (t, LANES), 1) < HEAD_DIM
    parts = []
    for c in range(B_Q // LANES):
        qc = q[:, c * LANES:(c + 1) * LANES]
        parts += [jnp.where(lo, qc, 0.0), jnp.where(lo, 0.0, qc)]
    return jnp.concatenate(parts, axis=0)


def _head_unstack_b(o_st, col_st, t):
    lo = lax.broadcasted_iota(jnp.int32, (t, LANES), 1) < HEAD_DIM
    cols = []
    for c in range(B_Q // LANES):
        r0 = 2 * c * t
        a = o_st[r0:r0 + t] * col_st[r0:r0 + t]
        b = o_st[r0 + t:r0 + 2 * t] * col_st[r0 + t:r0 + 2 * t]
        cols.append(jnp.where(lo, a, b))
    return jnp.concatenate(cols, axis=1)


def _sink_rows(sink_ref, t):
    return jnp.concatenate(
        [jnp.full((t, 1), sink_ref[p], _F32) for p in range(N_HEADS_B)], axis=0)


def _band_valid(n_keys, n_back, has_prev):
    r = lax.broadcasted_iota(jnp.int32, (BLK, n_keys), 0)
    c = lax.broadcasted_iota(jnp.int32, (BLK, n_keys), 1)
    dist = (BLK + r - c) if has_prev else (r - c)
    return (dist >= 0) & (dist <= n_back)


def _attn_prompt_kernel(sinkl_ref, q1_ref, k1_ref, v1_ref, q2_ref, k2_ref, v2_ref,
                        q3_ref, k3_ref, v3_ref, qb_ref, kb_ref, vb_ref,
                        oa_ref, ob_ref, acc_s, den_s, m_s):
    seq = q1_ref.shape[0]
    n_blk = seq // BLK
    dil2, dil3 = DIL_PAIRS[1][1], DIL_PAIRS[2][1]
    per_class2 = seq // dil2 // BLK
    assert dil2 * per_class2 == dil3 == n_blk and seq // dil3 == BLK

    lane_head_a = _lane_head(BLK, A_OUT)
    head_a_bf = lane_head_a.astype(_F32).astype(_BF16)
    lane_head_b = _lane_head(BLK, LANES)
    head_b_bf = lane_head_b.astype(_F32).astype(_BF16)
    r_i = lax.broadcasted_iota(jnp.int32, (BLK, 2 * BLK), 0)
    c_i = lax.broadcasted_iota(jnp.int32, (BLK, 2 * BLK), 1)
    dist = BLK + r_i - c_i
    band_a = (dist >= 0) & (dist <= BLK)
    band_b = (dist >= 0) & (dist <= WINDOW_B - 1)
    causal = (lax.broadcasted_iota(jnp.int32, (BLK, BLK), 0)
              >= lax.broadcasted_iota(jnp.int32, (BLK, BLK), 1))

    def with_prev(band, has_prev):
        return band & (c_i >= jnp.where(has_prev, 0, BLK))

    def window(ref, lead, r0):
        p0 = pl.multiple_of(jnp.maximum(r0 - BLK, 0), BLK)
        return jnp.concatenate([ref[lead + (pl.ds(p0, BLK),)], ref[lead + (pl.ds(r0, BLK),)]],
                               axis=0)

    def softmax_block(qs, kw, vw, valid, n_heads):
        n_keys = kw.shape[0]
        s = _dot_nt(qs, kw).reshape(n_heads, BLK, n_keys)
        s = jnp.where(valid[None], s, NEG)
        m = jnp.max(s, axis=-1, keepdims=True)
        e = jnp.exp(s - m)
        den = jnp.sum(e, axis=-1, keepdims=True)
        o = _dot(e.reshape(n_heads * BLK, n_keys).astype(_BF16), vw)
        return o, den.reshape(n_heads * BLK, 1), m.reshape(n_heads * BLK, 1)

    def block_a(q, kw, vw, valid):
        zero = jnp.zeros_like(q)
        qs = jnp.concatenate([jnp.where(head_a_bf == hh, q, zero) for hh in range(HEADS_A)], axis=0)
        o, den, m = softmax_block(qs, kw, vw, valid, HEADS_A)
        acc = o[:BLK]
        den_u = jnp.broadcast_to(den[:BLK], (BLK, A_OUT))
        m_u = jnp.broadcast_to(m[:BLK], (BLK, A_OUT))
        for hh in range(1, HEADS_A):
            sel = lane_head_a == hh
            rows = slice(hh * BLK, (hh + 1) * BLK)
            acc = jnp.where(sel, o[rows], acc)
            den_u = jnp.where(sel, jnp.broadcast_to(den[rows], (BLK, A_OUT)), den_u)
            m_u = jnp.where(sel, jnp.broadcast_to(m[rows], (BLK, A_OUT)), m_u)
        return acc, den_u, m_u

    def store(g, start, stride, acc, den_u, m_u):
        idx = pl.ds(start, BLK, stride=stride)
        for s in range(A_OUT // LANES):
            sl = slice(s * LANES, (s + 1) * LANES)
            acc_s[g, s, idx, :] = acc[:, sl]
            den_s[g, s, idx, :] = den_u[:, sl]
            m_s[g, s, idx, :] = m_u[:, sl]

    def classes_body(i, carry):
        r = lax.div(i, per_class2)
        sb = i - r * per_class2
        s0 = pl.multiple_of(sb * BLK, BLK)
        res = block_a(q2_ref[r, pl.ds(s0, BLK)], window(k2_ref, (r,), s0), window(v2_ref, (r,), s0),
                      with_prev(band_a, sb > 0))
        store(0, r + dil2 * s0, dil2, *res)
        res = block_a(q3_ref[i], k3_ref[i], v3_ref[i], causal)
        store(1, i, dil3, *res)
        return carry

    lax.fori_loop(0, n_blk, classes_body, 0, unroll=4)

    lo = lane_head_b == 0
    sink_lanes = sinkl_ref[...]

    def tokens_body(b, carry):
        r0 = pl.multiple_of(b * BLK, BLK)
        acc1, den1, m1 = block_a(q1_ref[pl.ds(r0, BLK)], window(k1_ref, (), r0),
                                 window(v1_ref, (), r0), with_prev(band_a, b > 0))
        cols = []
        for s in range(A_OUT // LANES):
            sl = slice(s * LANES, (s + 1) * LANES)
            rows = pl.ds(r0, BLK)
            ms = [m1[:, sl], m_s[0, s, rows, :], m_s[1, s, rows, :]]
            dens = [den1[:, sl], den_s[0, s, rows, :], den_s[1, s, rows, :]]
            accs = [acc1[:, sl], acc_s[0, s, rows, :], acc_s[1, s, rows, :]]
            top = jnp.maximum(jnp.maximum(ms[0], ms[1]), ms[2])
            ws = [jnp.exp(mm - top) for mm in ms]
            num = ws[0] * accs[0] + ws[1] * accs[1] + ws[2] * accs[2]
            dn = ws[0] * dens[0] + ws[1] * dens[1] + ws[2] * dens[2]
            cols.append(num / dn)
        oa_ref[pl.ds(r0, BLK), :] = jnp.concatenate(cols, axis=1).astype(oa_ref.dtype)

        q = qb_ref[pl.ds(r0, BLK)]
        zero = jnp.zeros((BLK, LANES), q.dtype)
        parts = []
        for c in range(B_Q // LANES):
            qc = q[:, c * LANES:(c + 1) * LANES]
            parts += [jnp.where(head_b_bf == 0, qc, zero), jnp.where(head_b_bf == 1, qc, zero)]
        o, den, m = softmax_block(jnp.concatenate(parts, axis=0), window(kb_ref, (), r0),
                                  window(vb_ref, (), r0), with_prev(band_b, b > 0), N_HEADS_B)
        cols = []
        for c in range(B_Q // LANES):
            ra = slice(2 * c * BLK, (2 * c + 1) * BLK)
            rb = slice((2 * c + 1) * BLK, (2 * c + 2) * BLK)
            o_c = jnp.where(lo, o[ra], o[rb])
            den_c = jnp.where(lo, jnp.broadcast_to(den[ra], (BLK, LANES)),
                              jnp.broadcast_to(den[rb], (BLK, LANES)))
            m_c = jnp.where(lo, jnp.broadcast_to(m[ra], (BLK, LANES)),
                            jnp.broadcast_to(m[rb], (BLK, LANES)))
            sink_c = sink_lanes[:, c * LANES:(c + 1) * LANES]
            cols.append(o_c / (den_c + jnp.exp(sink_c - m_c)))
        ob_ref[pl.ds(r0, BLK), :] = jnp.concatenate(cols, axis=1).astype(ob_ref.dtype)
        return carry

    lax.fori_loop(0, n_blk, tokens_body, 0, unroll=2)


def _attn_prompt(sinks_p, qkv, n_batch, seq):
    (q1, k1, v1, q2, k2, v2, q3, k3, v3, qb, kb, vb) = qkv
    tok = lambda w: pl.BlockSpec((seq, w), lambda n: (n, 0))
    cls = lambda dil: pl.BlockSpec((None, dil, seq // dil, A_OUT), lambda n: (n, 0, 0, 0))
    return pl.pallas_call(
        _attn_prompt_kernel,
        out_shape=[jax.ShapeDtypeStruct((n_batch * seq, A_OUT), _BF16),
                   jax.ShapeDtypeStruct((n_batch * seq, B_Q), _BF16)],
        grid=(n_batch,),
        in_specs=[_resident((1, B_Q))]
        + [tok(A_OUT)] * 3 + [cls(4)] * 3 + [cls(16)] * 3 + [tok(B_Q), tok(B_KV), tok(B_KV)],
        out_specs=[tok(A_OUT), tok(B_Q)],
        scratch_shapes=[pltpu.VMEM((N_GROUPS_A - 1, A_OUT // LANES, seq, LANES), _F32)] * 3,
        compiler_params=_cparams(1),
        name="attn_prompt",
    )(jnp.repeat(sinks_p, HEAD_DIM)[None, :], q1, k1, v1, q2, k2, v2, q3, k3, v3, qb, kb, vb)


def _attn_sample_kernel(sink_ref, q_ref, kvt_ref, c1_ref, c2_ref, c3_ref, cb_ref, *rest, t_new):
    oa_ref, ob_ref, n1_ref, n2_ref, n3_ref, nb_ref = rest[-6:]
    n = pl.program_id(0)
    per_tile = LANES // t_new
    first_new = LANES - t_new
    shift = first_new - (n & (per_tile - 1)) * t_new
    new_t = pltpu.roll(kvt_ref[...], shift, 1)
    q = q_ref[...]

    def new_cache(old_ref, new_rows, out_ref):
        lb = old_ref.shape[1]
        rolled = pltpu.roll(old_ref[...], lb - t_new, 1)
        if lb > LANES:
            out_ref[:, :lb - LANES] = rolled[:, :lb - LANES]
        lane = lax.broadcasted_iota(jnp.int32, (old_ref.shape[0], LANES), 1)
        out_ref[:, lb - LANES:] = jnp.where(lane >= first_new, new_rows, rolled[:, lb - LANES:])

    def masks(rows, lb, window, dil, strict):
        t_old = lax.broadcasted_iota(jnp.int32, (rows, lb), 0) & (t_new - 1)
        dist = lb + t_old - lax.broadcasted_iota(jnp.int32, (rows, lb), 1)
        hi = (dist < window) if strict else (dist <= window)
        valid_old = (dist >= 0) & hi & ((dist & (dil - 1)) == 0)
        t_nw = lax.broadcasted_iota(jnp.int32, (rows, LANES), 0) & (t_new - 1)
        c_nw = lax.broadcasted_iota(jnp.int32, (rows, LANES), 1) - first_new
        dn = t_nw - c_nw
        hin = (dn < window) if strict else (dn <= window)
        valid_new = (c_nw >= 0) & (dn >= 0) & hin & ((dn & (dil - 1)) == 0)
        return valid_old, valid_new

    def attend(qs, k_old, v_old, k_new, v_new, valid_old, valid_new):
        s_old = jnp.where(valid_old, _dot(qs, k_old), NEG)
        s_new = jnp.where(valid_new, _dot(qs, k_new), NEG)
        m = jnp.maximum(jnp.max(s_old, axis=-1, keepdims=True),
                        jnp.max(s_new, axis=-1, keepdims=True))
        e_old = jnp.exp(s_old - m)
        e_new = jnp.exp(s_new - m)
        den = jnp.sum(e_old, axis=-1, keepdims=True) + jnp.sum(e_new, axis=-1, keepdims=True)
        o = _dot_nt(e_old.astype(_BF16), v_old) + _dot_nt(e_new.astype(_BF16), v_new)
        return o, den, m

    outs, lses = [], []
    for g, (old_ref, out_ref) in enumerate(((c1_ref, n1_ref), (c2_ref, n2_ref), (c3_ref, n3_ref))):
        window, dil = DIL_PAIRS[g]
        lb = old_ref.shape[1]
        new_g = new_t[g * KV_ROWS_A:(g + 1) * KV_ROWS_A]
        qs = _head_stack_a(q[:, g * A_OUT:(g + 1) * A_OUT]).astype(_BF16)
        valid_old, valid_new = masks(HEADS_A * t_new, lb, window, dil, False)
        o, den, m = attend(qs, old_ref[0:A_OUT, :].astype(_BF16), old_ref[A_OUT:, :].astype(_BF16),
                           new_g[:A_OUT].astype(_BF16), new_g[A_OUT:].astype(_BF16),
                           valid_old, valid_new)
        outs.append(_head_unstack_a(o, 1.0 / den, t_new))
        lses.append(_head_unstack_a(None, m + jnp.log(den), t_new))
        new_cache(old_ref, new_g, out_ref)
    m = jnp.maximum(jnp.maximum(lses[0], lses[1]), lses[2])
    ws = [jnp.exp(l - m) for l in lses]
    oa_ref[...] = (ws[0] * outs[0] + ws[1] * outs[1] + ws[2] * outs[2]) / (ws[0] + ws[1] + ws[2])

    lb = cb_ref.shape[1]
    new_b = new_t[N_GROUPS_A * KV_ROWS_A:]
    qs = _head_stack_b(q[:, N_GROUPS_A * A_OUT:]).astype(_BF16)
    valid_old, valid_new = masks(N_HEADS_B * t_new, lb, WINDOW_B, 1, True)
    o, den, m = attend(qs, cb_ref[0:B_KV, :].astype(_BF16), cb_ref[B_KV:, :].astype(_BF16),
                       new_b[:B_KV].astype(_BF16), new_b[B_KV:].astype(_BF16),
                       valid_old, valid_new)
    scale = _sigmoid(m + jnp.log(den) - _sink_rows(sink_ref, t_new)) / den
    ob_ref[...] = _head_unstack_b(o, scale, t_new)
    new_cache(cb_ref, new_b, nb_ref)


def _attn_sample(layer, sinks_p, q_nat, kvt_new, caches, prev_out, t_new):
    n_batch = caches[0].shape[1]
    per_tile = LANES // t_new
    cache_spec = lambda c: pl.BlockSpec((None, None) + c.shape[2:], lambda n: (layer, n, 0, 0))
    any_spec = pl.BlockSpec(memory_space=pl.ANY)
    prev_out = [] if prev_out is None else list(prev_out)
    aliases = {3 + len(caches) + i: 2 + i for i in range(len(prev_out))}
    kern = functools.partial(_attn_sample_kernel, t_new=t_new)
    return pl.pallas_call(
        kern,
        out_shape=[jax.ShapeDtypeStruct((n_batch * t_new, A_OUT), _F32),
                   jax.ShapeDtypeStruct((n_batch * t_new, B_Q), _F32)]
        + [jax.ShapeDtypeStruct(c.shape, c.dtype) for c in caches],
        grid=(n_batch,),
        in_specs=[pl.BlockSpec(memory_space=pltpu.SMEM),
                  pl.BlockSpec((t_new, q_nat.shape[1]), lambda n: (n, 0)),
                  pl.BlockSpec((KVT_ROWS, LANES), lambda n: (0, n // per_tile))]
        + [cache_spec(c) for c in caches] + [any_spec] * len(prev_out),
        out_specs=[pl.BlockSpec((t_new, A_OUT), lambda n: (n, 0)),
                   pl.BlockSpec((t_new, B_Q), lambda n: (n, 0))]
        + [cache_spec(c) for c in caches],
        input_output_aliases=aliases,
        compiler_params=_cparams(1),
        name="attn_sample",
    )(sinks_p, q_nat, kvt_new, *caches, *prev_out)


def _mix_out_kernel(x_ref, g_ref, oa_ref, ob_ref, wgate_ref, wba_ref, wbb_ref, wout_ref, o_ref):
    x = x_ref[...]
    d = x.shape[1]
    h = _rms(x, g_ref[...]).astype(_BF16)
    gate_a = _dot(h, wgate_ref[:, :d])
    gate_b = _dot(h, wgate_ref[:, d:])
    ya = _dot(oa_ref[...].astype(_BF16), wba_ref[...])
    yb = _dot(ob_ref[...].astype(_BF16), wbb_ref[...])
    merged = _sigmoid(gate_a) * ya + _sigmoid(gate_b) * yb
    o_ref[...] = x + _dot(merged.astype(_BF16), wout_ref[...])


def _mix_out(x, g, oa, ob, w_gate, w_ba, w_bb, w_out):
    n_tok, d = x.shape
    tok = lambda w: pl.BlockSpec((TM, w), lambda i: (i, 0))
    return pl.pallas_call(
        _mix_out_kernel,
        out_shape=jax.ShapeDtypeStruct((n_tok, d), _F32),
        grid=(n_tok // TM,),
        in_specs=[tok(d), _resident((1, d)), tok(A_OUT), tok(B_Q), _resident(w_gate.shape),
                  _resident(w_ba.shape), _resident(w_bb.shape), _resident(w_out.shape)],
        out_specs=tok(d),
        compiler_params=_cparams(1),
        name="mix_out",
    )(x, g, oa, ob, w_gate, w_ba, w_bb, w_out)


def _cache_view(c):
    dep, nb, rows = c.shape[:3]
    return jnp.transpose(c, (0, 1, 3, 4, 5, 2)).reshape(dep, nb, -1, rows)


def _cache_unview(c, heads):
    dep, nb, _, rows = c.shape
    return jnp.transpose(c.reshape(dep, nb, 2, heads, HEAD_DIM, rows), (0, 1, 5, 2, 3, 4))


def kernel(x_prompt, x_sample, cache_a1, cache_a2, cache_a3, cache_b, norm_ffn1, ffn1_gate, ffn1_up, ffn1_down, norm_mix, w_in, sinks, w_branch_a, w_branch_b, w_out, norm_ffn2, ffn2_gate, ffn2_up, ffn2_down, norm_final):
    n_batch, seq, d = x_prompt.shape
    dec_batch, t_new, _ = x_sample.shape
    depth = w_in.shape[0]
    assert seq == DIL_PAIRS[2][0] and LANES % t_new == 0 and (dec_batch * t_new) % TM == 0
    assert TM % t_new == 0 and dec_batch % (LANES // t_new) == 0

    perm = jnp.array(B_HEAD_PERM)
    col_b = 3 * N_GROUPS_A * A_OUT

    cos_p, sin_p = _rope_tables(jnp.arange(seq))
    cos_s, sin_s = _rope_tables(PAST_LEN + jnp.arange(t_new))
    cos_s, sin_s = jnp.tile(cos_s, (TM // t_new, 1)), jnp.tile(sin_s, (TM // t_new, 1))
    nat_p, nat_s = _nat_tables(cos_p, sin_p), _nat_tables(cos_s, sin_s)
    t_p, t_s = (cos_p.T, sin_p.T), (cos_s.T, sin_s.T)

    caches = [_cache_view(c) for c in (cache_a1, cache_a2, cache_a3, cache_b)]

    xp = x_prompt.reshape(n_batch * seq, d)
    xs = x_sample.reshape(dec_batch * t_new, d)
    row = lambda v: v.reshape(1, d)
    gf = row(norm_final)
    prompt_bufs = []
    sample_bufs = None
    for l in range(depth):
        last = l == depth - 1
        bf = lambda w: w.astype(_BF16)
        w_l = w_in[l]
        bq = w_l[:, col_b:col_b + B_Q].reshape(d, N_HEADS_B, HEAD_DIM)[:, perm].reshape(d, B_Q)
        w_qkv = bf(jnp.concatenate([w_l[:, :col_b], bq, w_l[:, col_b + B_Q:QKV_COLS]], axis=1))
        kv_cols = [w_l[:, (3 * g + 1) * A_OUT:(3 * g + 3) * A_OUT] for g in range(N_GROUPS_A)]
        kv_cols.append(w_l[:, col_b + B_Q:QKV_COLS])
        w_kvt = bf(jnp.concatenate(kv_cols, axis=1).T)
        w_q = bf(jnp.concatenate([w_l[:, 3 * g * A_OUT:(3 * g + 1) * A_OUT]
                                  for g in range(N_GROUPS_A)] + [bq], axis=1))
        w_gate = bf(w_l[:, QKV_COLS:])
        w_bb = bf(w_branch_b[l].reshape(N_HEADS_B, HEAD_DIM, d)[perm].reshape(B_Q, d))
        w_ba, w_o = bf(w_branch_a[l]), bf(w_out[l])
        sinks_p = sinks[l][perm]
        f1 = (row(norm_ffn1[l]), bf(ffn1_gate[l]), bf(ffn1_up[l]), bf(ffn1_down[l]))
        f2 = (row(norm_ffn2[l]), bf(ffn2_gate[l]), bf(ffn2_up[l]), bf(ffn2_down[l]))
        g_mix = row(norm_mix[l])

        xp = _ffn(xp, *f1, gf, final_norm=False)
        xs = _ffn(xs, *f1, gf, final_norm=False)

        outs = _mix_in_prompt(xp, g_mix, w_qkv, w_kvt, nat_p, t_p, n_batch, seq)
        prompt_bufs.append(outs[12:])
        oa_p, ob_p = _attn_prompt(sinks_p, outs[:12], n_batch, seq)

        q_nat, kvt_new = _mix_in_sample(xs, g_mix, w_q, w_kvt, nat_s, t_s)
        s_outs = _attn_sample(l, sinks_p, q_nat, kvt_new, caches, sample_bufs, t_new)
        oa_s, ob_s, sample_bufs = s_outs[0], s_outs[1], list(s_outs[2:])

        xp = _mix_out(xp, g_mix, oa_p, ob_p, w_gate, w_ba, w_bb, w_o)
        xs = _mix_out(xs, g_mix, oa_s, ob_s, w_gate, w_ba, w_bb, w_o)
        xp = _ffn(xp, *f2, gf, final_norm=last)
        xs = _ffn(xs, *f2, gf, final_norm=last)

    heads = (HEADS_A, HEADS_A, HEADS_A, N_KV_B)
    new_p = [_cache_unview(jnp.stack([prompt_bufs[l][i] for l in range(depth)]), heads[i])
             for i in range(4)]
    new_s = [_cache_unview(sample_bufs[i], heads[i]) for i in range(4)]
    return (xp.reshape(n_batch, seq, d), xs.reshape(dec_batch, t_new, d), *new_p, *new_s)
```

```python
import functools

import jax
import jax.numpy as jnp
from jax import lax
from jax.experimental import pallas as pl
from jax.experimental.pallas import tpu as pltpu

PAST_LEN = 16384
HEAD_DIM = 64
ROT_DIM = HEAD_DIM // 4
ROT_HALF = ROT_DIM // 2
ROPE_THETA = 500000.0
DIL_PAIRS = ((128, 1), (512, 4), (2048, 16))
N_GROUPS_A = 3
HEADS_A = 4
A_OUT = HEADS_A * HEAD_DIM
N_HEADS_B = 8
N_KV_B = 2
WINDOW_B = 128
B_Q = N_HEADS_B * HEAD_DIM
B_KV = N_KV_B * HEAD_DIM
QKV_COLS = N_GROUPS_A * 3 * A_OUT + B_Q + 2 * B_KV
KV_ROWS_A = 2 * A_OUT
KV_ROWS_B = 2 * B_KV
KVT_ROWS = N_GROUPS_A * KV_ROWS_A + KV_ROWS_B
NORM_EPS = 1e-6
NEG = -1e30
SCALE = HEAD_DIM ** -0.5
BLK = 128
LANES = 128
MXU_TILE = 256
TM = 512
FFN_CHUNKS = 1
B_HEAD_PERM = (0, 4, 1, 5, 2, 6, 3, 7)
V7X_SCOPED_VMEM_BYTES = 60000 * 1024

_BF16 = jnp.bfloat16
_F32 = jnp.float32


def _cparams(n_axes):
    return pltpu.CompilerParams(
        dimension_semantics=("arbitrary",) * n_axes,
        vmem_limit_bytes=V7X_SCOPED_VMEM_BYTES,
    )


def _resident(shape):
    nd = len(shape)
    return pl.BlockSpec(shape, lambda *_: (0,) * nd, pipeline_mode=pl.Buffered(1))


def _rms(x, g):
    return x * lax.rsqrt(jnp.mean(x * x, axis=-1, keepdims=True) + NORM_EPS) * g


def _sigmoid(x):
    return 1.0 / (1.0 + jnp.exp(-x))


def _dot(a, b):
    return jnp.dot(a, b, preferred_element_type=_F32)


def _dot_nt(a, b):
    return lax.dot_general(a, b, (((1,), (1,)), ((), ())), preferred_element_type=_F32)


def _ffn_kernel(x_ref, g_ref, wg_ref, wu_ref, wd_ref, gf_ref, *rest, bounds, final_norm, host):
    o_ref = rest[-1] if host is None else rest[-7]
    if host is not None:
        n_base, t_new = host
        _attn_sample_body(rest[0], rest[1], rest[2], rest[3:7], rest[-6], rest[-5], rest[-4:],
                          n_base + pl.program_id(0), t_new)
    x = x_ref[...]
    h = _rms(x, g_ref[...]).astype(_BF16)
    acc = None
    for lo, hi in zip(bounds[:-1], bounds[1:]):
        gate = _dot(h, wg_ref[:, lo:hi])
        up = _dot(h, wu_ref[:, lo:hi])
        act = (gate * _sigmoid(gate) * up).astype(_BF16)
        part = _dot(act, wd_ref[lo:hi, :])
        acc = part if acc is None else acc + part
    y = x + 0.5 * acc
    if final_norm:
        y = _rms(y, gf_ref[...])
    o_ref[...] = y


def _ffn(x, g, wg, wu, wd, gf, *, final_norm, tm, host=None):
    n_tok, d = x.shape
    f = wg.shape[1]
    steps = n_tok // tm
    n_tiles = f // MXU_TILE
    cuts = [0, (n_tiles + 1) // 2 * MXU_TILE, f] if FFN_CHUNKS == 2 and n_tiles > 1 else [0, f]
    tok_spec = pl.BlockSpec((tm, d), lambda i: (i, 0))
    in_specs = [tok_spec, _resident((1, d)), _resident((d, f)), _resident((d, f)),
                _resident((f, d)), _resident((1, d))]
    out_shape = [jax.ShapeDtypeStruct((n_tok, d), _F32)]
    out_specs = [tok_spec]
    operands = [x, g, wg, wu, wd, gf]
    aliases = {}
    host_static = None
    if host is not None:
        layer, n_base, t_new, sinks_p, q_nat, kvt_new, caches, prev_out = host
        per_tile = LANES // t_new
        host_static = (n_base, t_new)
        cache_spec = lambda c: pl.BlockSpec((None, None) + c.shape[2:],
                                            lambda i: (layer, n_base + i, 0, 0))
        prev_out = [] if prev_out is None else list(prev_out)
        in_specs += ([pl.BlockSpec(memory_space=pltpu.SMEM),
                      pl.BlockSpec((t_new, q_nat.shape[1]), lambda i: (n_base + i, 0)),
                      pl.BlockSpec((KVT_ROWS, LANES), lambda i: (0, (n_base + i) // per_tile))]
                     + [cache_spec(c) for c in caches]
                     + [pl.BlockSpec(memory_space=pl.ANY)] * len(prev_out))
        aliases = {len(operands) + 3 + len(caches) + k: 3 + k for k in range(len(prev_out))}
        operands += [sinks_p, q_nat, kvt_new, *caches, *prev_out]
        out_shape += ([jax.ShapeDtypeStruct((steps * t_new, A_OUT), _F32),
                       jax.ShapeDtypeStruct((steps * t_new, B_Q), _F32)]
                      + [jax.ShapeDtypeStruct(c.shape, c.dtype) for c in caches])
        out_specs += ([pl.BlockSpec((t_new, A_OUT), lambda i: (i, 0)),
                       pl.BlockSpec((t_new, B_Q), lambda i: (i, 0))]
                      + [cache_spec(c) for c in caches])
    kern = functools.partial(_ffn_kernel, bounds=tuple(cuts), final_norm=final_norm,
                             host=host_static)
    name = ("ffn_final" if final_norm else "ffn") + ("" if host is None else "_attn_sample")
    return pl.pallas_call(
        kern,
        out_shape=out_shape,
        grid=(steps,),
        in_specs=in_specs,
        out_specs=out_specs,
        input_output_aliases=aliases,
        compiler_params=_cparams(1),
        name=name,
    )(*operands)


def _rope_nat(x, c, s_lo, s_hi):
    return x * c + pltpu.roll(x, ROT_HALF, 1) * s_hi + pltpu.roll(x, LANES - ROT_HALF, 1) * s_lo


def _rope_t(k_t, cos_t, sin_t, n_heads):
    parts = []
    for hh in range(n_heads):
        b = hh * HEAD_DIM
        x1 = k_t[b:b + ROT_HALF]
        x2 = k_t[b + ROT_HALF:b + ROT_DIM]
        parts += [x1 * cos_t - x2 * sin_t, x2 * cos_t + x1 * sin_t, k_t[b + ROT_DIM:b + HEAD_DIM]]
    return jnp.concatenate(parts, axis=0)


def _rope_tables(pos):
    inv = ROPE_THETA ** (-(jnp.arange(ROT_HALF, dtype=_F32) * 2.0 / ROT_DIM))
    ang = pos.astype(_F32)[:, None] * inv[None, :]
    return jnp.cos(ang), jnp.sin(ang)


def _nat_tables(cos, sin):
    p = cos.shape[0]
    one = jnp.ones((p, HEAD_DIM - ROT_DIM), _F32)
    zero = jnp.zeros((p, HEAD_DIM - ROT_DIM), _F32)
    zh = jnp.zeros((p, ROT_HALF), _F32)
    c = jnp.concatenate([cos, cos, one], axis=1)
    s_hi = jnp.concatenate([zh, sin, zero], axis=1)
    s_lo = jnp.concatenate([-sin, zh, zero], axis=1)
    rep = LANES // HEAD_DIM
    return tuple(jnp.tile(t, (1, rep)) for t in (c, s_lo, s_hi))


def _mix_in_prompt_kernel(x_ref, g_ref, w_ref, wt_ref, c_ref, slo_ref, shi_ref, ct_ref, st_ref,
                          *rest, n_prev):
    prev = rest[:4] if n_prev else None
    (q1_ref, k1_ref, v1_ref, q2_ref, k2_ref, v2_ref, q3_ref, k3_ref, v3_ref,
     qb_ref, kb_ref, vb_ref, c1_ref, c2_ref, c3_ref, cb_ref, zs_ref) = rest[-17:]
    j = pl.program_id(1)
    tm = x_ref.shape[0]
    h = _rms(x_ref[...], g_ref[...]).astype(_BF16)
    c, s_lo, s_hi = c_ref[...], slo_ref[...], shi_ref[...]

    def proj(col0, width, kind):
        z = _dot(h, w_ref[:, col0:col0 + width])
        if kind == "v":
            return z
        slabs = []
        for s in range(width // LANES):
            zz = _rope_nat(z[:, s * LANES:(s + 1) * LANES], c, s_lo, s_hi)
            slabs.append(zz * SCALE if kind == "q" else zz)
        return jnp.concatenate(slabs, axis=1)

    for i, (ref, kind) in enumerate(((q1_ref, "q"), (k1_ref, "k"), (v1_ref, "v"))):
        ref[...] = proj(i * A_OUT, A_OUT, kind).astype(_BF16)
    for g, refs in ((1, (q2_ref, k2_ref, v2_ref)), (2, (q3_ref, k3_ref, v3_ref))):
        dil = DIL_PAIRS[g][1]
        rows = tm // dil
        for i, (ref, kind) in enumerate(zip(refs, ("q", "k", "v"))):
            z = proj((3 * g + i) * A_OUT, A_OUT, kind)
            for s in range(A_OUT // LANES):
                zs_ref[s] = z[:, s * LANES:(s + 1) * LANES]
            for r in range(dil):
                for s in range(A_OUT // LANES):
                    ref[r, :, s * LANES:(s + 1) * LANES] = (
                        zs_ref[s, pl.ds(r, rows, stride=dil), :].astype(_BF16))
    col_b = 3 * N_GROUPS_A * A_OUT
    qb_ref[...] = proj(col_b, B_Q, "q").astype(_BF16)
    z = _dot(h, w_ref[:, col_b + B_Q:col_b + B_Q + 2 * B_KV])
    kb_ref[...] = _rope_nat(z[:, :B_KV], c, s_lo, s_hi).astype(_BF16)
    vb_ref[...] = z[:, B_KV:].astype(_BF16)

    def kv_t(row0, n_rows, h_tok, cos_t, sin_t, n_heads):
        z = _dot_nt(wt_ref[row0:row0 + n_rows, :], h_tok)
        half = n_rows // 2
        return jnp.concatenate([_rope_t(z[:half], cos_t, sin_t, n_heads), z[half:]], axis=0)

    def put(i, out_ref, new):
        if n_prev:
            out_ref[:n_prev] = prev[i][...]
        out_ref[n_prev] = new

    ct, st = ct_ref[...], st_ref[...]
    put(2, c3_ref, kv_t(2 * KV_ROWS_A, KV_ROWS_A, h, ct, st, HEADS_A))

    @pl.when(j == pl.num_programs(1) - 1)
    def _():
        put(1, c2_ref, kv_t(KV_ROWS_A, KV_ROWS_A, h, ct, st, HEADS_A))
        tail = tm - BLK
        put(0, c1_ref, kv_t(0, KV_ROWS_A, h[tail:], ct[:, tail:], st[:, tail:], HEADS_A))
        put(3, cb_ref, kv_t(3 * KV_ROWS_A, KV_ROWS_B, h[tail:], ct[:, tail:], st[:, tail:], N_KV_B))


def _mix_in_prompt(x, g, w_qkv, w_kvt, nat_tabs, t_tabs, n_batch, seq, prev_bufs):
    d = x.shape[1]
    n_j = seq // TM
    n_prev = prev_bufs[0].shape[0] if prev_bufs else 0
    row = lambda n, j: (n * n_j + j, 0)
    tok_spec = lambda w: pl.BlockSpec((TM, w), row)
    cls_spec = lambda dil: pl.BlockSpec((None, dil, TM // dil, A_OUT), lambda n, j: (n, 0, j, 0))
    nat = lambda w: jax.ShapeDtypeStruct((n_batch * seq, w), _BF16)
    cls = lambda dil: jax.ShapeDtypeStruct((n_batch, dil, seq // dil, A_OUT), _BF16)
    keep = [min(w, seq) for w, _ in DIL_PAIRS]
    assert keep == [BLK, TM, seq] and seq % TM == 0
    bufs = ((KV_ROWS_A, keep[0], BLK, lambda j: 0), (KV_ROWS_A, keep[1], TM, lambda j: 0),
            (KV_ROWS_A, keep[2], TM, lambda j: j), (KV_ROWS_B, min(WINDOW_B, seq), BLK, lambda j: 0))

    def buf_spec(layers, rows, width, jmap):
        return pl.BlockSpec((layers, None, rows, width), lambda n, j: (0, n, 0, jmap(j)))

    def buf_shape(layers, rows, kept):
        return jax.ShapeDtypeStruct((layers, n_batch, rows, kept), _F32)

    n_out = n_prev + 1
    out_shape = ([nat(A_OUT)] * 3 + [cls(4)] * 3 + [cls(16)] * 3 + [nat(B_Q), nat(B_KV), nat(B_KV)]
                 + [buf_shape(n_out, r, k) for r, k, _, _ in bufs])
    out_specs = ([tok_spec(A_OUT)] * 3 + [cls_spec(4)] * 3 + [cls_spec(16)] * 3
                 + [tok_spec(B_Q), tok_spec(B_KV), tok_spec(B_KV)]
                 + [buf_spec(n_out, r, w, jm) for r, _, w, jm in bufs])
    prev_specs = [buf_spec(n_prev, r, w, jm) for r, _, w, jm in bufs] if n_prev else []
    tab_spec = pl.BlockSpec((TM, LANES), lambda n, j: (j, 0))
    ttab_spec = pl.BlockSpec((ROT_HALF, TM), lambda n, j: (0, j))
    return pl.pallas_call(
        functools.partial(_mix_in_prompt_kernel, n_prev=n_prev),
        out_shape=out_shape,
        grid=(n_batch, n_j),
        in_specs=[tok_spec(d), _resident((1, d)), _resident(w_qkv.shape), _resident(w_kvt.shape),
                  tab_spec, tab_spec, tab_spec, ttab_spec, ttab_spec] + prev_specs,
        out_specs=out_specs,
        scratch_shapes=[pltpu.VMEM((A_OUT // LANES, TM, LANES), _F32)],
        compiler_params=_cparams(2),
        name="mix_in_prompt",
    )(x, g, w_qkv, w_kvt, *nat_tabs, *t_tabs, *prev_bufs)


def _mix_in_sample_kernel(x_ref, g_ref, w_ref, wt_ref, c_ref, slo_ref, shi_ref, ct_ref, st_ref,
                          q_ref, kvt_ref):
    h = _rms(x_ref[...], g_ref[...]).astype(_BF16)
    c, s_lo, s_hi = c_ref[...], slo_ref[...], shi_ref[...]
    z = _dot(h, w_ref[...])
    for s in range(z.shape[1] // LANES):
        sl = slice(s * LANES, (s + 1) * LANES)
        q_ref[:, sl] = _rope_nat(z[:, sl], c, s_lo, s_hi) * SCALE
    ct, st = ct_ref[...], st_ref[...]
    for g in range(N_GROUPS_A):
        r0 = g * KV_ROWS_A
        zt = _dot_nt(wt_ref[r0:r0 + KV_ROWS_A, :], h)
        kvt_ref[r0:r0 + A_OUT, :] = _rope_t(zt[:A_OUT], ct, st, HEADS_A)
        kvt_ref[r0 + A_OUT:r0 + KV_ROWS_A, :] = zt[A_OUT:]
    r0 = N_GROUPS_A * KV_ROWS_A
    zt = _dot_nt(wt_ref[r0:r0 + KV_ROWS_B, :], h)
    kvt_ref[r0:r0 + B_KV, :] = _rope_t(zt[:B_KV], ct, st, N_KV_B)
    kvt_ref[r0 + B_KV:r0 + KV_ROWS_B, :] = zt[B_KV:]


def _mix_in_sample(x, g, w_q, w_kvt, nat_tabs, t_tabs, tm):
    n_tok, d = x.shape
    qw = w_q.shape[1]
    tab_spec = pl.BlockSpec((tm, LANES), lambda i: (0, 0))
    ttab_spec = pl.BlockSpec((ROT_HALF, tm), lambda i: (0, 0))
    return pl.pallas_call(
        _mix_in_sample_kernel,
        out_shape=[jax.ShapeDtypeStruct((n_tok, qw), _F32),
                   jax.ShapeDtypeStruct((KVT_ROWS, n_tok), _F32)],
        grid=(n_tok // tm,),
        in_specs=[pl.BlockSpec((tm, d), lambda i: (i, 0)), _resident((1, d)),
                  _resident(w_q.shape), _resident(w_kvt.shape),
                  tab_spec, tab_spec, tab_spec, ttab_spec, ttab_spec],
        out_specs=[pl.BlockSpec((tm, qw), lambda i: (i, 0)),
                   pl.BlockSpec((KVT_ROWS, tm), lambda i: (0, i))],
        compiler_params=_cparams(1),
        name="mix_in_sample",
    )(x, g, w_q, w_kvt, *nat_tabs, *t_tabs)


def _softmax_rows(s, valid):
    s = jnp.where(valid, s, NEG)
    m = jnp.max(s, axis=-1, keepdims=True)
    e = jnp.exp(s - m)
    return e, jnp.sum(e, axis=-1, keepdims=True), m


def _lane_head(t, width):
    return lax.shift_right_logical(lax.broadcasted_iota(jnp.int32, (t, width), 1),
                                   HEAD_DIM.bit_length() - 1)


def _head_stack_a(q):
    lane_head = _lane_head(q.shape[0], A_OUT)
    return jnp.concatenate([jnp.where(lane_head == hh, q, 0.0) for hh in range(HEADS_A)], axis=0)


def _head_unstack_a(o_st, col_st, t):
    lane_head = _lane_head(t, A_OUT)
    out = jnp.zeros((t, A_OUT), _F32)
    for hh in range(HEADS_A):
        col = col_st[hh * t:(hh + 1) * t]
        blk = jnp.broadcast_to(col, (t, A_OUT)) if o_st is None else o_st[hh * t:(hh + 1) * t] * col
        out = jnp.where(lane_head == hh, blk, out)
    return out


def _head_stack_b(q):
    t = q.shape[0]
    lo = lax.broadcasted_iota(jnp.int32, (t, LANES), 1) < HEAD_DIM
    parts = []
    for c in range(B_Q // LANES):
        qc = q[:, c * LANES:(c + 1) * LANES]
        parts += [jnp.where(lo, qc, 0.0), jnp.where(lo, 0.0, qc)]
    return jnp.concatenate(parts, axis=0)


def _head_unstack_b(o_st, col_st, t):
    lo = lax.broadcasted_iota(jnp.int32, (t, LANES), 1) < HEAD_DIM
    cols = []
    for c in range(B_Q // LANES):
        r0 = 2 * c * t
        a = o_st[r0:r0 + t] * col_st[r0:r0 + t]
        b = o_st[r0 + t:r0 + 2 * t] * col_st[r0 + t:r0 + 2 * t]
        cols.append(jnp.where(lo, a, b))
    return jnp.concatenate(cols, axis=1)


def _sink_rows(sink_ref, t):
    return jnp.concatenate(
        [jnp.full((t, 1), sink_ref[p], _F32) for p in range(N_HEADS_B)], axis=0)


def _band_valid(n_keys, n_back, has_prev):
    r = lax.broadcasted_iota(jnp.int32, (BLK, n_keys), 0)
    c = lax.broadcasted_iota(jnp.int32, (BLK, n_keys), 1)
    dist = (BLK + r - c) if has_prev else (r - c)
    return (dist >= 0) & (dist <= n_back)


def _attn_prompt_kernel(sinkl_ref, q1_ref, k1_ref, v1_ref, q2_ref, k2_ref, v2_ref,
                        q3_ref, k3_ref, v3_ref, qb_ref, kb_ref, vb_ref,
                        oa_ref, ob_ref, acc_s, den_s, m_s):
    seq = q1_ref.shape[0]
    n_blk = seq // BLK
    dil2, dil3 = DIL_PAIRS[1][1], DIL_PAIRS[2][1]
    per_class2 = seq // dil2 // BLK
    assert dil2 * per_class2 == dil3 == n_blk and seq // dil3 == BLK

    lane_head_a = _lane_head(BLK, A_OUT)
    head_a_bf = lane_head_a.astype(_F32).astype(_BF16)
    lane_head_b = _lane_head(BLK, LANES)
    head_b_bf = lane_head_b.astype(_F32).astype(_BF16)
    r_i = lax.broadcasted_iota(jnp.int32, (BLK, 2 * BLK), 0)
    c_i = lax.broadcasted_iota(jnp.int32, (BLK, 2 * BLK), 1)
    dist = BLK + r_i - c_i
    band_a = (dist >= 0) & (dist <= BLK)
    band_b = (dist >= 0) & (dist <= WINDOW_B - 1)
    causal = (lax.broadcasted_iota(jnp.int32, (BLK, BLK), 0)
              >= lax.broadcasted_iota(jnp.int32, (BLK, BLK), 1))

    def with_prev(band, has_prev):
        return band & (c_i >= jnp.where(has_prev, 0, BLK))

    def window(ref, lead, r0):
        p0 = pl.multiple_of(jnp.maximum(r0 - BLK, 0), BLK)
        return jnp.concatenate([ref[lead + (pl.ds(p0, BLK),)], ref[lead + (pl.ds(r0, BLK),)]],
                               axis=0)

    def softmax_block(qs, kw, vw, valid, n_heads):
        n_keys = kw.shape[0]
        s = _dot_nt(qs, kw).reshape(n_heads, BLK, n_keys)
        s = jnp.where(valid[None], s, NEG)
        m = jnp.max(s, axis=-1, keepdims=True)
        e = jnp.exp(s - m)
        den = jnp.sum(e, axis=-1, keepdims=True)
        o = _dot(e.reshape(n_heads * BLK, n_keys).astype(_BF16), vw)
        return o, den.reshape(n_heads * BLK, 1), m.reshape(n_heads * BLK, 1)

    def block_a(q, kw, vw, valid):
        zero = jnp.zeros_like(q)
        qs = jnp.concatenate([jnp.where(head_a_bf == hh, q, zero) for hh in range(HEADS_A)], axis=0)
        o, den, m = softmax_block(qs, kw, vw, valid, HEADS_A)
        acc = o[:BLK]
        den_u = jnp.broadcast_to(den[:BLK], (BLK, A_OUT))
        m_u = jnp.broadcast_to(m[:BLK], (BLK, A_OUT))
        for hh in range(1, HEADS_A):
            sel = lane_head_a == hh
            rows = slice(hh * BLK, (hh + 1) * BLK)
            acc = jnp.where(sel, o[rows], acc)
            den_u = jnp.where(sel, jnp.broadcast_to(den[rows], (BLK, A_OUT)), den_u)
            m_u = jnp.where(sel, jnp.broadcast_to(m[rows], (BLK, A_OUT)), m_u)
        return acc, den_u, m_u

    def store(g, start, stride, acc, den_u, m_u):
        idx = pl.ds(start, BLK, stride=stride)
        for s in range(A_OUT // LANES):
            sl = slice(s * LANES, (s + 1) * LANES)
            acc_s[g, s, idx, :] = acc[:, sl]
            den_s[g, s, idx, :] = den_u[:, sl]
            m_s[g, s, idx, :] = m_u[:, sl]

    def classes_body(i, carry):
        r = lax.div(i, per_class2)
        sb = i - r * per_class2
        s0 = pl.multiple_of(sb * BLK, BLK)
        res = block_a(q2_ref[r, pl.ds(s0, BLK)], window(k2_ref, (r,), s0), window(v2_ref, (r,), s0),
                      with_prev(band_a, sb > 0))
        store(0, r + dil2 * s0, dil2, *res)
        res = block_a(q3_ref[i], k3_ref[i], v3_ref[i], causal)
        store(1, i, dil3, *res)
        return carry

    lax.fori_loop(0, n_blk, classes_body, 0, unroll=4)

    lo = lane_head_b == 0
    sink_lanes = sinkl_ref[...]

    def tokens_body(b, carry):
        r0 = pl.multiple_of(b * BLK, BLK)
        acc1, den1, m1 = block_a(q1_ref[pl.ds(r0, BLK)], window(k1_ref, (), r0),
                                 window(v1_ref, (), r0), with_prev(band_a, b > 0))
        cols = []
        for s in range(A_OUT // LANES):
            sl = slice(s * LANES, (s + 1) * LANES)
            rows = pl.ds(r0, BLK)
            ms = [m1[:, sl], m_s[0, s, rows, :], m_s[1, s, rows, :]]
            dens = [den1[:, sl], den_s[0, s, rows, :], den_s[1, s, rows, :]]
            accs = [acc1[:, sl], acc_s[0, s, rows, :], acc_s[1, s, rows, :]]
            top = jnp.maximum(jnp.maximum(ms[0], ms[1]), ms[2])
            ws = [jnp.exp(mm - top) for mm in ms]
            num = ws[0] * accs[0] + ws[1] * accs[1] + ws[2] * accs[2]
            dn = ws[0] * dens[0] + ws[1] * dens[1] + ws[2] * dens[2]
            cols.append(num / dn)
        oa_ref[pl.ds(r0, BLK), :] = jnp.concatenate(cols, axis=1).astype(oa_ref.dtype)

        q = qb_ref[pl.ds(r0, BLK)]
        zero = jnp.zeros((BLK, LANES), q.dtype)
        parts = []
        for c in range(B_Q // LANES):
            qc = q[:, c * LANES:(c + 1) * LANES]
            parts += [jnp.where(head_b_bf == 0, qc, zero), jnp.where(head_b_bf == 1, qc, zero)]
        o, den, m = softmax_block(jnp.concatenate(parts, axis=0), window(kb_ref, (), r0),
                                  window(vb_ref, (), r0), with_prev(band_b, b > 0), N_HEADS_B)
        cols = []
        for c in range(B_Q // LANES):
            ra = slice(2 * c * BLK, (2 * c + 1) * BLK)
            rb = slice((2 * c + 1) * BLK, (2 * c + 2) * BLK)
            o_c = jnp.where(lo, o[ra], o[rb])
            den_c = jnp.where(lo, jnp.broadcast_to(den[ra], (BLK, LANES)),
                              jnp.broadcast_to(den[rb], (BLK, LANES)))
            m_c = jnp.where(lo, jnp.broadcast_to(m[ra], (BLK, LANES)),
                            jnp.broadcast_to(m[rb], (BLK, LANES)))
            sink_c = sink_lanes[:, c * LANES:(c + 1) * LANES]
            cols.append(o_c / (den_c + jnp.exp(sink_c - m_c)))
        ob_ref[pl.ds(r0, BLK), :] = jnp.concatenate(cols, axis=1).astype(ob_ref.dtype)
        return carry

    lax.fori_loop(0, n_blk, tokens_body, 0, unroll=2)


def _attn_prompt(sinks_p, qkv, n_batch, seq):
    (q1, k1, v1, q2, k2, v2, q3, k3, v3, qb, kb, vb) = qkv
    tok = lambda w: pl.BlockSpec((seq, w), lambda n: (n, 0))
    cls = lambda dil: pl.BlockSpec((None, dil, seq // dil, A_OUT), lambda n: (n, 0, 0, 0))
    return pl.pallas_call(
        _attn_prompt_kernel,
        out_shape=[jax.ShapeDtypeStruct((n_batch * seq, A_OUT), _BF16),
                   jax.ShapeDtypeStruct((n_batch * seq, B_Q), _BF16)],
        grid=(n_batch,),
        in_specs=[_resident((1, B_Q))]
        + [tok(A_OUT)] * 3 + [cls(4)] * 3 + [cls(16)] * 3 + [tok(B_Q), tok(B_KV), tok(B_KV)],
        out_specs=[tok(A_OUT), tok(B_Q)],
        scratch_shapes=[pltpu.VMEM((N_GROUPS_A - 1, A_OUT // LANES, seq, LANES), _F32)] * 3,
        compiler_params=_cparams(1),
        name="attn_prompt",
    )(jnp.repeat(sinks_p, HEAD_DIM)[None, :], q1, k1, v1, q2, k2, v2, q3, k3, v3, qb, kb, vb)


def _attn_sample_body(sink_ref, q_ref, kvt_ref, cache_refs, oa_ref, ob_ref, new_refs, n, t_new):
    c1_ref, c2_ref, c3_ref, cb_ref = cache_refs
    n1_ref, n2_ref, n3_ref, nb_ref = new_refs
    per_tile = LANES // t_new
    first_new = LANES - t_new
    shift = first_new - (n & (per_tile - 1)) * t_new
    new_t = pltpu.roll(kvt_ref[...], shift, 1)
    q = q_ref[...]

    def new_cache(old_ref, new_rows, out_ref):
        lb = old_ref.shape[1]
        rolled = pltpu.roll(old_ref[...], lb - t_new, 1)
        if lb > LANES:
            out_ref[:, :lb - LANES] = rolled[:, :lb - LANES]
        lane = lax.broadcasted_iota(jnp.int32, (old_ref.shape[0], LANES), 1)
        out_ref[:, lb - LANES:] = jnp.where(lane >= first_new, new_rows, rolled[:, lb - LANES:])

    def masks(rows, lb, window, dil, strict):
        t_old = lax.broadcasted_iota(jnp.int32, (rows, lb), 0) & (t_new - 1)
        dist = lb + t_old - lax.broadcasted_iota(jnp.int32, (rows, lb), 1)
        hi = (dist < window) if strict else (dist <= window)
        valid_old = (dist >= 0) & hi & ((dist & (dil - 1)) == 0)
        t_nw = lax.broadcasted_iota(jnp.int32, (rows, LANES), 0) & (t_new - 1)
        c_nw = lax.broadcasted_iota(jnp.int32, (rows, LANES), 1) - first_new
        dn = t_nw - c_nw
        hin = (dn < window) if strict else (dn <= window)
        valid_new = (c_nw >= 0) & (dn >= 0) & hin & ((dn & (dil - 1)) == 0)
        return valid_old, valid_new

    def attend(qs, k_old, v_old, k_new, v_new, valid_old, valid_new):
        s_old = jnp.where(valid_old, _dot(qs, k_old), NEG)
        s_new = jnp.where(valid_new, _dot(qs, k_new), NEG)
        m = jnp.maximum(jnp.max(s_old, axis=-1, keepdims=True),
                        jnp.max(s_new, axis=-1, keepdims=True))
        e_old = jnp.exp(s_old - m)
        e_new = jnp.exp(s_new - m)
        den = jnp.sum(e_old, axis=-1, keepdims=True) + jnp.sum(e_new, axis=-1, keepdims=True)
        o = _dot_nt(e_old.astype(_BF16), v_old) + _dot_nt(e_new.astype(_BF16), v_new)
        return o, den, m

    outs, lses = [], []
    for g, (old_ref, out_ref) in enumerate(((c1_ref, n1_ref), (c2_ref, n2_ref), (c3_ref, n3_ref))):
        window, dil = DIL_PAIRS[g]
        lb = old_ref.shape[1]
        new_g = new_t[g * KV_ROWS_A:(g + 1) * KV_ROWS_A]
        qs = _head_stack_a(q[:, g * A_OUT:(g + 1) * A_OUT]).astype(_BF16)
        valid_old, valid_new = masks(HEADS_A * t_new, lb, window, dil, False)
        o, den, m = attend(qs, old_ref[0:A_OUT, :].astype(_BF16), old_ref[A_OUT:, :].astype(_BF16),
                           new_g[:A_OUT].astype(_BF16), new_g[A_OUT:].astype(_BF16),
                           valid_old, valid_new)
        outs.append(_head_unstack_a(o, 1.0 / den, t_new))
        lses.append(_head_unstack_a(None, m + jnp.log(den), t_new))
        new_cache(old_ref, new_g, out_ref)
    m = jnp.maximum(jnp.maximum(lses[0], lses[1]), lses[2])
    ws = [jnp.exp(l - m) for l in lses]
    oa_ref[...] = (ws[0] * outs[0] + ws[1] * outs[1] + ws[2] * outs[2]) / (ws[0] + ws[1] + ws[2])

    lb = cb_ref.shape[1]
    new_b = new_t[N_GROUPS_A * KV_ROWS_A:]
    qs = _head_stack_b(q[:, N_GROUPS_A * A_OUT:]).astype(_BF16)
    valid_old, valid_new = masks(N_HEADS_B * t_new, lb, WINDOW_B, 1, True)
    o, den, m = attend(qs, cb_ref[0:B_KV, :].astype(_BF16), cb_ref[B_KV:, :].astype(_BF16),
                       new_b[:B_KV].astype(_BF16), new_b[B_KV:].astype(_BF16),
                       valid_old, valid_new)
    scale = _sigmoid(m + jnp.log(den) - _sink_rows(sink_ref, t_new)) / den
    ob_ref[...] = _head_unstack_b(o, scale, t_new)
    new_cache(cb_ref, new_b, nb_ref)


def _mix_out_kernel(x_ref, g_ref, oa_ref, ob_ref, wgate_ref, wba_ref, wbb_ref, wout_ref, o_ref):
    x = x_ref[...]
    d = x.shape[1]
    h = _rms(x, g_ref[...]).astype(_BF16)
    gate_a = _dot(h, wgate_ref[:, :d])
    gate_b = _dot(h, wgate_ref[:, d:])
    ya = _dot(oa_ref[...].astype(_BF16), wba_ref[...])
    yb = _dot(ob_ref[...].astype(_BF16), wbb_ref[...])
    merged = _sigmoid(gate_a) * ya + _sigmoid(gate_b) * yb
    o_ref[...] = x + _dot(merged.astype(_BF16), wout_ref[...])


def _mix_out(x, g, oa, ob, w_gate, w_ba, w_bb, w_out, tm):
    n_tok, d = x.shape
    tok = lambda w: pl.BlockSpec((tm, w), lambda i: (i, 0))
    return pl.pallas_call(
        _mix_out_kernel,
        out_shape=jax.ShapeDtypeStruct((n_tok, d), _F32),
        grid=(n_tok // tm,),
        in_specs=[tok(d), _resident((1, d)), tok(A_OUT), tok(B_Q), _resident(w_gate.shape),
                  _resident(w_ba.shape), _resident(w_bb.shape), _resident(w_out.shape)],
        out_specs=tok(d),
        compiler_params=_cparams(1),
        name="mix_out",
    )(x, g, oa, ob, w_gate, w_ba, w_bb, w_out)


def _cache_view(c):
    dep, nb, rows = c.shape[:3]
    return jnp.transpose(c, (0, 1, 3, 4, 5, 2)).reshape(dep, nb, -1, rows)


def _cache_unview(c, heads):
    dep, nb, _, rows = c.shape
    return jnp.transpose(c.reshape(dep, nb, 2, heads, HEAD_DIM, rows), (0, 1, 5, 2, 3, 4))


def kernel(x_prompt, x_sample, cache_a1, cache_a2, cache_a3, cache_b, norm_ffn1, ffn1_gate, ffn1_up, ffn1_down, norm_mix, w_in, sinks, w_branch_a, w_branch_b, w_out, norm_ffn2, ffn2_gate, ffn2_up, ffn2_down, norm_final):
    n_batch, seq, d = x_prompt.shape
    dec_batch, t_new, _ = x_sample.shape
    depth = w_in.shape[0]
    n_sample = dec_batch * t_new
    tm_s = min(TM, n_sample)
    hosts_per_layer = 2
    n_hosted = dec_batch // hosts_per_layer
    tm_p = n_batch * seq // n_hosted
    assert seq == DIL_PAIRS[2][0] and LANES % t_new == 0 and n_sample % tm_s == 0
    assert tm_s % t_new == 0 and dec_batch % (LANES // t_new) == 0
    assert dec_batch % hosts_per_layer == 0 and (n_batch * seq) % n_hosted == 0 and tm_p % 8 == 0

    perm = jnp.array(B_HEAD_PERM)
    col_b = 3 * N_GROUPS_A * A_OUT

    cos_p, sin_p = _rope_tables(jnp.arange(seq))
    cos_s, sin_s = _rope_tables(PAST_LEN + jnp.arange(t_new))
    cos_s, sin_s = jnp.tile(cos_s, (tm_s // t_new, 1)), jnp.tile(sin_s, (tm_s // t_new, 1))
    nat_p, nat_s = _nat_tables(cos_p, sin_p), _nat_tables(cos_s, sin_s)
    t_p, t_s = (cos_p.T, sin_p.T), (cos_s.T, sin_s.T)

    caches = [_cache_view(c) for c in (cache_a1, cache_a2, cache_a3, cache_b)]

    xp = x_prompt.reshape(n_batch * seq, d)
    xs = x_sample.reshape(dec_batch * t_new, d)
    row = lambda v: v.reshape(1, d)
    gf = row(norm_final)
    prompt_bufs = []
    sample_bufs = None
    for l in range(depth):
        last = l == depth - 1
        bf = lambda w: w.astype(_BF16)
        w_l = w_in[l]
        bq = w_l[:, col_b:col_b + B_Q].reshape(d, N_HEADS_B, HEAD_DIM)[:, perm].reshape(d, B_Q)
        w_qkv = bf(jnp.concatenate([w_l[:, :col_b], bq, w_l[:, col_b + B_Q:QKV_COLS]], axis=1))
        kv_cols = [w_l[:, (3 * g + 1) * A_OUT:(3 * g + 3) * A_OUT] for g in range(N_GROUPS_A)]
        kv_cols.append(w_l[:, col_b + B_Q:QKV_COLS])
        w_kvt = bf(jnp.concatenate(kv_cols, axis=1).T)
        w_q = bf(jnp.concatenate([w_l[:, 3 * g * A_OUT:(3 * g + 1) * A_OUT]
                                  for g in range(N_GROUPS_A)] + [bq], axis=1))
        w_gate = bf(w_l[:, QKV_COLS:])
        w_bb = bf(w_branch_b[l].reshape(N_HEADS_B, HEAD_DIM, d)[perm].reshape(B_Q, d))
        w_ba, w_o = bf(w_branch_a[l]), bf(w_out[l])
        sinks_p = sinks[l][perm]
        f1 = (row(norm_ffn1[l]), bf(ffn1_gate[l]), bf(ffn1_up[l]), bf(ffn1_down[l]))
        f2 = (row(norm_ffn2[l]), bf(ffn2_gate[l]), bf(ffn2_up[l]), bf(ffn2_down[l]))
        g_mix = row(norm_mix[l])

        xs = _ffn(xs, *f1, gf, final_norm=False, tm=tm_s)[0]
        q_nat, kvt_new = _mix_in_sample(xs, g_mix, w_q, w_kvt, nat_s, t_s, tm_s)
        host = lambda n_base: (l, n_base, t_new, sinks_p, q_nat, kvt_new, caches, sample_bufs)

        xp, oa_0, ob_0, *sample_bufs = _ffn(xp, *f1, gf, final_norm=False, tm=tm_p, host=host(0))
        outs = _mix_in_prompt(xp, g_mix, w_qkv, w_kvt, nat_p, t_p, n_batch, seq, prompt_bufs)
        prompt_bufs = list(outs[12:])
        oa_p, ob_p = _attn_prompt(sinks_p, outs[:12], n_batch, seq)
        xp = _mix_out(xp, g_mix, oa_p, ob_p, w_gate, w_ba, w_bb, w_o, TM)
        xp, oa_1, ob_1, *sample_bufs = _ffn(xp, *f2, gf, final_norm=last, tm=tm_p,
                                            host=host(n_hosted))

        oa_s = jnp.concatenate([oa_0, oa_1], axis=0)
        ob_s = jnp.concatenate([ob_0, ob_1], axis=0)
        xs = _mix_out(xs, g_mix, oa_s, ob_s, w_gate, w_ba, w_bb, w_o, tm_s)
        xs = _ffn(xs, *f2, gf, final_norm=last, tm=tm_s)[0]

    heads = (HEADS_A, HEADS_A, HEADS_A, N_KV_B)
    new_p = [_cache_unview(prompt_bufs[i], heads[i]) for i in range(4)]
    new_s = [_cache_unview(sample_bufs[i], heads[i]) for i in range(4)]
    return (xp.reshape(n_batch, seq, d), xs.reshape(dec_batch, t_new, d), *new_p, *new_s)
```

```python
import functools

import jax
import jax.numpy as jnp
from jax import lax
from jax.experimental import pallas as pl
from jax.experimental.pallas import tpu as pltpu

PAST_LEN = 16384
HEAD_DIM = 64
ROT_DIM = HEAD_DIM // 4
ROT_HALF = ROT_DIM // 2
ROPE_THETA = 500000.0
DIL_PAIRS = ((128, 1), (512, 4), (2048, 16))
N_GROUPS_A = 3
HEADS_A = 4
A_OUT = HEADS_A * HEAD_DIM
N_HEADS_B = 8
N_KV_B = 2
WINDOW_B = 128
B_Q = N_HEADS_B * HEAD_DIM
B_KV = N_KV_B * HEAD_DIM
QKV_COLS = N_GROUPS_A * 3 * A_OUT + B_Q + 2 * B_KV
KV_ROWS_A = 2 * A_OUT
KV_ROWS_B = 2 * B_KV
KVT_ROWS = N_GROUPS_A * KV_ROWS_A + KV_ROWS_B
KVT_ROWS_Q = KVT_ROWS + A_OUT
NORM_EPS = 1e-6
NEG = -1e30
SCALE = HEAD_DIM ** -0.5
BLK = 128
LANES = 128
MXU_TILE = 256
TM = 512
FFN_CHUNKS = 1
B_HEAD_PERM = (0, 4, 1, 5, 2, 6, 3, 7)
V7X_SCOPED_VMEM_BYTES = 60000 * 1024

_BF16 = jnp.bfloat16
_F32 = jnp.float32


def _cparams(n_axes):
    return pltpu.CompilerParams(
        dimension_semantics=("arbitrary",) * n_axes,
        vmem_limit_bytes=V7X_SCOPED_VMEM_BYTES,
    )


def _resident(shape):
    nd = len(shape)
    return pl.BlockSpec(shape, lambda *_: (0,) * nd, pipeline_mode=pl.Buffered(1))


def _layer_block(arr, layer, block=None, index=None):
    block = tuple(arr.shape[1:]) if block is None else tuple(block)
    index = (0,) * len(block) if index is None else tuple(index)
    return pl.BlockSpec((None,) + block, lambda *_: (layer,) + index,
                        pipeline_mode=pl.Buffered(1))


def _rms(x, g):
    return x * lax.rsqrt(jnp.mean(x * x, axis=-1, keepdims=True) + NORM_EPS) * g


def _sigmoid(x):
    return 1.0 / (1.0 + jnp.exp(-x))


def _dot(a, b):
    return jnp.dot(a, b, preferred_element_type=_F32)


def _dot_nt(a, b):
    return lax.dot_general(a, b, (((1,), (1,)), ((), ())), preferred_element_type=_F32)


def _ffn_kernel(x_ref, g_ref, wg_ref, wu_ref, wd_ref, gf_ref, *rest, bounds, final_norm, host):
    o_ref = rest[-1] if host is None else rest[-7]
    if host is not None:
        n_base, t_new = host
        _attn_sample_body(rest[0], rest[1], rest[2], rest[3:7], rest[-6], rest[-5], rest[-4:],
                          n_base + pl.program_id(0), t_new)
    x = x_ref[...]
    h = _rms(x, g_ref[...]).astype(_BF16)
    acc = None
    for lo, hi in zip(bounds[:-1], bounds[1:]):
        gate = _dot(h, wg_ref[:, lo:hi])
        up = _dot(h, wu_ref[:, lo:hi])
        act = (gate * _sigmoid(gate) * up).astype(_BF16)
        part = _dot(act, wd_ref[lo:hi, :])
        acc = part if acc is None else acc + part
    y = x + 0.5 * acc
    if final_norm:
        y = _rms(y, gf_ref[...])
    o_ref[...] = y


def _ffn(x, layer, g, wg, wu, wd, gf, *, final_norm, tm, host=None):
    n_tok, d = x.shape
    f = wg.shape[-1]
    steps = n_tok // tm
    n_tiles = f // MXU_TILE
    cuts = [0, (n_tiles + 1) // 2 * MXU_TILE, f] if FFN_CHUNKS == 2 and n_tiles > 1 else [0, f]
    tok_spec = pl.BlockSpec((tm, d), lambda i: (i, 0))
    in_specs = ([tok_spec] + [_layer_block(a, layer) for a in (g, wg, wu, wd)]
                + [_resident((1, d))])
    out_shape = [jax.ShapeDtypeStruct((n_tok, d), _F32)]
    out_specs = [tok_spec]
    operands = [x, g, wg, wu, wd, gf]
    aliases = {}
    host_static = None
    if host is not None:
        n_base, t_new, sinks_p, q_nat, kvt_new, caches, prev_out = host
        per_tile = LANES // t_new
        host_static = (n_base, t_new)
        cache_spec = lambda c: pl.BlockSpec((None, None) + c.shape[2:],
                                            lambda i: (layer, n_base + i, 0, 0))
        prev_out = [] if prev_out is None else list(prev_out)
        in_specs += ([pl.BlockSpec(memory_space=pltpu.SMEM),
                      pl.BlockSpec((t_new, q_nat.shape[1]), lambda i: (n_base + i, 0)),
                      pl.BlockSpec((KVT_ROWS_Q, LANES), lambda i: (0, (n_base + i) // per_tile))]
                     + [cache_spec(c) for c in caches]
                     + [pl.BlockSpec(memory_space=pl.ANY)] * len(prev_out))
        aliases = {len(operands) + 3 + len(caches) + k: 3 + k for k in range(len(prev_out))}
        operands += [sinks_p, q_nat, kvt_new, *caches, *prev_out]
        out_shape += ([jax.ShapeDtypeStruct((steps * t_new, A_OUT), _F32),
                       jax.ShapeDtypeStruct((steps * t_new, B_Q), _F32)]
                      + [jax.ShapeDtypeStruct(c.shape, c.dtype) for c in caches])
        out_specs += ([pl.BlockSpec((t_new, A_OUT), lambda i: (i, 0)),
                       pl.BlockSpec((t_new, B_Q), lambda i: (i, 0))]
                      + [cache_spec(c) for c in caches])
    kern = functools.partial(_ffn_kernel, bounds=tuple(cuts), final_norm=final_norm,
                             host=host_static)
    name = ("ffn_final" if final_norm else "ffn") + ("" if host is None else "_attn_sample")
    return pl.pallas_call(
        kern,
        out_shape=out_shape,
        grid=(steps,),
        in_specs=in_specs,
        out_specs=out_specs,
        input_output_aliases=aliases,
        compiler_params=_cparams(1),
        name=name,
    )(*operands)


def _rope_nat(x, c, s_lo, s_hi):
    return x * c + pltpu.roll(x, ROT_HALF, 1) * s_hi + pltpu.roll(x, LANES - ROT_HALF, 1) * s_lo


def _rope_t(k_t, cos_t, sin_t, n_heads):
    parts = []
    for hh in range(n_heads):
        b = hh * HEAD_DIM
        x1 = k_t[b:b + ROT_HALF]
        x2 = k_t[b + ROT_HALF:b + ROT_DIM]
        parts += [x1 * cos_t - x2 * sin_t, x2 * cos_t + x1 * sin_t, k_t[b + ROT_DIM:b + HEAD_DIM]]
    return jnp.concatenate(parts, axis=0)


def _rope_tables(pos):
    inv = ROPE_THETA ** (-(jnp.arange(ROT_HALF, dtype=_F32) * 2.0 / ROT_DIM))
    ang = pos.astype(_F32)[:, None] * inv[None, :]
    return jnp.cos(ang), jnp.sin(ang)


def _nat_tables(cos, sin):
    p = cos.shape[0]
    one = jnp.ones((p, HEAD_DIM - ROT_DIM), _F32)
    zero = jnp.zeros((p, HEAD_DIM - ROT_DIM), _F32)
    zh = jnp.zeros((p, ROT_HALF), _F32)
    c = jnp.concatenate([cos, cos, one], axis=1)
    s_hi = jnp.concatenate([zh, sin, zero], axis=1)
    s_lo = jnp.concatenate([-sin, zh, zero], axis=1)
    rep = LANES // HEAD_DIM
    return tuple(jnp.tile(t, (1, rep)) for t in (c, s_lo, s_hi))


def _mix_in_prompt_kernel(x_ref, g_ref, w_ref, wt_ref, c_ref, slo_ref, shi_ref, ct_ref, st_ref,
                          *rest, n_prev):
    prev = rest[:4] if n_prev else None
    (q1_ref, k1_ref, v1_ref, q2_ref, k2_ref, v2_ref, q3_ref, k3_ref, v3_ref,
     qb_ref, kb_ref, vb_ref, c1_ref, c2_ref, c3_ref, cb_ref, zs_ref) = rest[-17:]
    j = pl.program_id(1)
    tm = x_ref.shape[0]
    h = _rms(x_ref[...], g_ref[...]).astype(_BF16)
    c, s_lo, s_hi = c_ref[...], slo_ref[...], shi_ref[...]

    def proj(col0, width, kind):
        z = _dot(h, w_ref[:, col0:col0 + width])
        if kind == "v":
            return z
        slabs = []
        for s in range(width // LANES):
            zz = _rope_nat(z[:, s * LANES:(s + 1) * LANES], c, s_lo, s_hi)
            slabs.append(zz * SCALE if kind == "q" else zz)
        return jnp.concatenate(slabs, axis=1)

    for i, (ref, kind) in enumerate(((q1_ref, "q"), (k1_ref, "k"), (v1_ref, "v"))):
        ref[...] = proj(i * A_OUT, A_OUT, kind).astype(_BF16)
    for g, refs in ((1, (q2_ref, k2_ref, v2_ref)), (2, (q3_ref, k3_ref, v3_ref))):
        dil = DIL_PAIRS[g][1]
        rows = tm // dil
        for i, (ref, kind) in enumerate(zip(refs, ("q", "k", "v"))):
            z = proj((3 * g + i) * A_OUT, A_OUT, kind)
            for s in range(A_OUT // LANES):
                zs_ref[s] = z[:, s * LANES:(s + 1) * LANES]
            for r in range(dil):
                for s in range(A_OUT // LANES):
                    ref[r, :, s * LANES:(s + 1) * LANES] = (
                        zs_ref[s, pl.ds(r, rows, stride=dil), :].astype(_BF16))
    col_b = 3 * N_GROUPS_A * A_OUT
    qb_ref[...] = proj(col_b, B_Q, "q").astype(_BF16)
    z = _dot(h, w_ref[:, col_b + B_Q:col_b + B_Q + 2 * B_KV])
    kb_ref[...] = _rope_nat(z[:, :B_KV], c, s_lo, s_hi).astype(_BF16)
    vb_ref[...] = z[:, B_KV:].astype(_BF16)

    def kv_t(row0, n_rows, h_tok, cos_t, sin_t, n_heads):
        z = _dot_nt(wt_ref[row0:row0 + n_rows, :], h_tok)
        half = n_rows // 2
        return jnp.concatenate([_rope_t(z[:half], cos_t, sin_t, n_heads), z[half:]], axis=0)

    def put(i, out_ref, new):
        if n_prev:
            out_ref[:n_prev] = prev[i][...]
        out_ref[n_prev] = new

    ct, st = ct_ref[...], st_ref[...]
    put(2, c3_ref, kv_t(2 * KV_ROWS_A, KV_ROWS_A, h, ct, st, HEADS_A))

    @pl.when(j == pl.num_programs(1) - 1)
    def _():
        put(1, c2_ref, kv_t(KV_ROWS_A, KV_ROWS_A, h, ct, st, HEADS_A))
        tail = tm - BLK
        put(0, c1_ref, kv_t(0, KV_ROWS_A, h[tail:], ct[:, tail:], st[:, tail:], HEADS_A))
        put(3, cb_ref, kv_t(3 * KV_ROWS_A, KV_ROWS_B, h[tail:], ct[:, tail:], st[:, tail:], N_KV_B))


def _mix_in_prompt(x, layer, g, w_qkv, w_kvt, nat_tabs, t_tabs, n_batch, seq, prev_bufs):
    d = x.shape[1]
    n_j = seq // TM
    n_prev = prev_bufs[0].shape[0] if prev_bufs else 0
    row = lambda n, j: (n * n_j + j, 0)
    tok_spec = lambda w: pl.BlockSpec((TM, w), row)
    cls_spec = lambda dil: pl.BlockSpec((None, dil, TM // dil, A_OUT), lambda n, j: (n, 0, j, 0))
    nat = lambda w: jax.ShapeDtypeStruct((n_batch * seq, w), _BF16)
    cls = lambda dil: jax.ShapeDtypeStruct((n_batch, dil, seq // dil, A_OUT), _BF16)
    keep = [min(w, seq) for w, _ in DIL_PAIRS]
    assert keep == [BLK, TM, seq] and seq % TM == 0
    bufs = ((KV_ROWS_A, keep[0], BLK, lambda j: 0), (KV_ROWS_A, keep[1], TM, lambda j: 0),
            (KV_ROWS_A, keep[2], TM, lambda j: j), (KV_ROWS_B, min(WINDOW_B, seq), BLK, lambda j: 0))

    def buf_spec(layers, rows, width, jmap):
        return pl.BlockSpec((layers, None, rows, width), lambda n, j: (0, n, 0, jmap(j)))

    def buf_shape(layers, rows, kept):
        return jax.ShapeDtypeStruct((layers, n_batch, rows, kept), _F32)

    n_out = n_prev + 1
    out_shape = ([nat(A_OUT)] * 3 + [cls(4)] * 3 + [cls(16)] * 3 + [nat(B_Q), nat(B_KV), nat(B_KV)]
                 + [buf_shape(n_out, r, k) for r, k, _, _ in bufs])
    out_specs = ([tok_spec(A_OUT)] * 3 + [cls_spec(4)] * 3 + [cls_spec(16)] * 3
                 + [tok_spec(B_Q), tok_spec(B_KV), tok_spec(B_KV)]
                 + [buf_spec(n_out, r, w, jm) for r, _, w, jm in bufs])
    prev_specs = [buf_spec(n_prev, r, w, jm) for r, _, w, jm in bufs] if n_prev else []
    tab_spec = pl.BlockSpec((TM, LANES), lambda n, j: (j, 0))
    ttab_spec = pl.BlockSpec((ROT_HALF, TM), lambda n, j: (0, j))
    return pl.pallas_call(
        functools.partial(_mix_in_prompt_kernel, n_prev=n_prev),
        out_shape=out_shape,
        grid=(n_batch, n_j),
        in_specs=[tok_spec(d)] + [_layer_block(a, layer) for a in (g, w_qkv, w_kvt)]
        + [tab_spec, tab_spec, tab_spec, ttab_spec, ttab_spec] + prev_specs,
        out_specs=out_specs,
        scratch_shapes=[pltpu.VMEM((A_OUT // LANES, TM, LANES), _F32)],
        compiler_params=_cparams(2),
        name="mix_in_prompt",
    )(x, g, w_qkv, w_kvt, *nat_tabs, *t_tabs, *prev_bufs)


def _mix_in_sample_kernel(x_ref, g_ref, w_ref, wt_ref, c_ref, slo_ref, shi_ref, ct_ref, st_ref,
                          q_ref, kvt_ref):
    h = _rms(x_ref[...], g_ref[...]).astype(_BF16)
    c, s_lo, s_hi = c_ref[...], slo_ref[...], shi_ref[...]
    z = _dot(h, w_ref[...])
    for s in range(z.shape[1] // LANES):
        sl = slice(s * LANES, (s + 1) * LANES)
        q_ref[:, sl] = _rope_nat(z[:, sl], c, s_lo, s_hi) * SCALE
    ct, st = ct_ref[...], st_ref[...]
    for g in range(N_GROUPS_A):
        r0 = g * KV_ROWS_A
        zt = _dot_nt(wt_ref[r0:r0 + KV_ROWS_A, :], h)
        kvt_ref[r0:r0 + A_OUT, :] = _rope_t(zt[:A_OUT], ct, st, HEADS_A)
        kvt_ref[r0 + A_OUT:r0 + KV_ROWS_A, :] = zt[A_OUT:]
    r0 = N_GROUPS_A * KV_ROWS_A
    zt = _dot_nt(wt_ref[r0:r0 + KV_ROWS_B, :], h)
    kvt_ref[r0:r0 + B_KV, :] = _rope_t(zt[:B_KV], ct, st, N_KV_B)
    kvt_ref[r0 + B_KV:r0 + KV_ROWS_B, :] = zt[B_KV:]
    zt = _dot_nt(wt_ref[KVT_ROWS:KVT_ROWS + A_OUT, :], h)
    kvt_ref[KVT_ROWS:KVT_ROWS + A_OUT, :] = _rope_t(zt, ct, st, HEADS_A) * SCALE


def _mix_in_sample(x, layer, g, w_q, w_kvt, nat_tabs, t_tabs, tm):
    n_tok, d = x.shape
    qw = w_q.shape[-1]
    tab_spec = pl.BlockSpec((tm, LANES), lambda i: (0, 0))
    ttab_spec = pl.BlockSpec((ROT_HALF, tm), lambda i: (0, 0))
    assert w_kvt.shape[1] == KVT_ROWS_Q
    return pl.pallas_call(
        _mix_in_sample_kernel,
        out_shape=[jax.ShapeDtypeStruct((n_tok, qw), _F32),
                   jax.ShapeDtypeStruct((KVT_ROWS_Q, n_tok), _F32)],
        grid=(n_tok // tm,),
        in_specs=[pl.BlockSpec((tm, d), lambda i: (i, 0))]
        + [_layer_block(a, layer) for a in (g, w_q, w_kvt)]
        + [tab_spec, tab_spec, tab_spec, ttab_spec, ttab_spec],
        out_specs=[pl.BlockSpec((tm, qw), lambda i: (i, 0)),
                   pl.BlockSpec((KVT_ROWS_Q, tm), lambda i: (0, i))],
        compiler_params=_cparams(1),
        name="mix_in_sample",
    )(x, g, w_q, w_kvt, *nat_tabs, *t_tabs)


def _softmax_rows(s, valid):
    s = jnp.where(valid, s, NEG)
    m = jnp.max(s, axis=-1, keepdims=True)
    e = jnp.exp(s - m)
    return e, jnp.sum(e, axis=-1, keepdims=True), m


def _lane_head(t, width):
    return lax.shift_right_logical(lax.broadcasted_iota(jnp.int32, (t, width), 1),
                                   HEAD_DIM.bit_length() - 1)


def _head_stack_a(q):
    lane_head = _lane_head(q.shape[0], A_OUT)
    return jnp.concatenate([jnp.where(lane_head == hh, q, 0.0) for hh in range(HEADS_A)], axis=0)


def _head_unstack_a(o_st, col_st, t):
    lane_head = _lane_head(t, A_OUT)
    out = jnp.zeros((t, A_OUT), _F32)
    for hh in range(HEADS_A):
        col = col_st[hh * t:(hh + 1) * t]
        blk = jnp.broadcast_to(col, (t, A_OUT)) if o_st is None else o_st[hh * t:(hh + 1) * t] * col
        out = jnp.where(lane_head == hh, blk, out)
    return out


def _head_stack_b(q):
    t = q.shape[0]
    lo = lax.broadcasted_iota(jnp.int32, (t, LANES), 1) < HEAD_DIM
    parts = []
    for c in range(B_Q // LANES):
        qc = q[:, c * LANES:(c + 1) * LANES]
        parts += [jnp.where(lo, qc, 0.0), jnp.where(lo, 0.0, qc)]
    return jnp.concatenate(parts, axis=0)


def _head_unstack_b(o_st, col_st, t):
    lo = lax.broadcasted_iota(jnp.int32, (t, LANES), 1) < HEAD_DIM
    cols = []
    for c in range(B_Q // LANES):
        r0 = 2 * c * t
        a = o_st[r0:r0 + t] * col_st[r0:r0 + t]
        b = o_st[r0 + t:r0 + 2 * t] * col_st[r0 + t:r0 + 2 * t]
        cols.append(jnp.where(lo, a, b))
    return jnp.concatenate(cols, axis=1)


def _sink_rows(sink_ref, t):
    return jnp.concatenate(
        [jnp.full((t, 1), sink_ref[p], _F32) for p in range(N_HEADS_B)], axis=0)


def _band_valid(n_keys, n_back, has_prev):
    r = lax.broadcasted_iota(jnp.int32, (BLK, n_keys), 0)
    c = lax.broadcasted_iota(jnp.int32, (BLK, n_keys), 1)
    dist = (BLK + r - c) if has_prev else (r - c)
    return (dist >= 0) & (dist <= n_back)


def _attn_prompt_kernel(sinkl_ref, q1_ref, k1_ref, v1_ref, q2_ref, k2_ref, v2_ref,
                        q3_ref, k3_ref, v3_ref, qb_ref, kb_ref, vb_ref,
                        oa_ref, ob_ref, acc_s, den_s, m_s):
    seq = q1_ref.shape[0]
    n_blk = seq // BLK
    dil2, dil3 = DIL_PAIRS[1][1], DIL_PAIRS[2][1]
    per_class2 = seq // dil2 // BLK
    assert dil2 * per_class2 == dil3 == n_blk and seq // dil3 == BLK

    lane_head_a = _lane_head(BLK, A_OUT)
    head_a_bf = lane_head_a.astype(_F32).astype(_BF16)
    lane_head_b = _lane_head(BLK, LANES)
    head_b_bf = lane_head_b.astype(_F32).astype(_BF16)
    r_i = lax.broadcasted_iota(jnp.int32, (BLK, 2 * BLK), 0)
    c_i = lax.broadcasted_iota(jnp.int32, (BLK, 2 * BLK), 1)
    dist = BLK + r_i - c_i
    band_a = (dist >= 0) & (dist <= BLK)
    band_b = (dist >= 0) & (dist <= WINDOW_B - 1)
    causal = (lax.broadcasted_iota(jnp.int32, (BLK, BLK), 0)
              >= lax.broadcasted_iota(jnp.int32, (BLK, BLK), 1))

    def with_prev(band, has_prev):
        return band & (c_i >= jnp.where(has_prev, 0, BLK))

    def window(ref, lead, r0):
        p0 = pl.multiple_of(jnp.maximum(r0 - BLK, 0), BLK)
        return jnp.concatenate([ref[lead + (pl.ds(p0, BLK),)], ref[lead + (pl.ds(r0, BLK),)]],
                               axis=0)

    def softmax_block(qs, kw, vw, valid, n_heads):
        n_keys = kw.shape[0]
        s = _dot_nt(qs, kw).reshape(n_heads, BLK, n_keys)
        s = jnp.where(valid[None], s, NEG)
        m = jnp.max(s, axis=-1, keepdims=True)
        e = jnp.exp(s - m)
        den = jnp.sum(e, axis=-1, keepdims=True)
        o = _dot(e.reshape(n_heads * BLK, n_keys).astype(_BF16), vw)
        return o, den.reshape(n_heads * BLK, 1), m.reshape(n_heads * BLK, 1)

    def block_a(q, kw, vw, valid):
        zero = jnp.zeros_like(q)
        qs = jnp.concatenate([jnp.where(head_a_bf == hh, q, zero) for hh in range(HEADS_A)], axis=0)
        o, den, m = softmax_block(qs, kw, vw, valid, HEADS_A)
        acc = o[:BLK]
        den_u = jnp.broadcast_to(den[:BLK], (BLK, A_OUT))
        m_u = jnp.broadcast_to(m[:BLK], (BLK, A_OUT))
        for hh in range(1, HEADS_A):
            sel = lane_head_a == hh
            rows = slice(hh * BLK, (hh + 1) * BLK)
            acc = jnp.where(sel, o[rows], acc)
            den_u = jnp.where(sel, jnp.broadcast_to(den[rows], (BLK, A_OUT)), den_u)
            m_u = jnp.where(sel, jnp.broadcast_to(m[rows], (BLK, A_OUT)), m_u)
        return acc, den_u, m_u

    def store(g, start, stride, acc, den_u, m_u):
        idx = pl.ds(start, BLK, stride=stride)
        for s in range(A_OUT // LANES):
            sl = slice(s * LANES, (s + 1) * LANES)
            acc_s[g, s, idx, :] = acc[:, sl]
            den_s[g, s, idx, :] = den_u[:, sl]
            m_s[g, s, idx, :] = m_u[:, sl]

    def classes_body(i, carry):
        r = lax.div(i, per_class2)
        sb = i - r * per_class2
        s0 = pl.multiple_of(sb * BLK, BLK)
        res = block_a(q2_ref[r, pl.ds(s0, BLK)], window(k2_ref, (r,), s0), window(v2_ref, (r,), s0),
                      with_prev(band_a, sb > 0))
        store(0, r + dil2 * s0, dil2, *res)
        res = block_a(q3_ref[i], k3_ref[i], v3_ref[i], causal)
        store(1, i, dil3, *res)
        return carry

    lax.fori_loop(0, n_blk, classes_body, 0, unroll=4)

    lo = lane_head_b == 0
    sink_lanes = sinkl_ref[...]

    def tokens_body(b, carry):
        r0 = pl.multiple_of(b * BLK, BLK)
        acc1, den1, m1 = block_a(q1_ref[pl.ds(r0, BLK)], window(k1_ref, (), r0),
                                 window(v1_ref, (), r0), with_prev(band_a, b > 0))
        cols = []
        for s in range(A_OUT // LANES):
            sl = slice(s * LANES, (s + 1) * LANES)
            rows = pl.ds(r0, BLK)
            ms = [m1[:, sl], m_s[0, s, rows, :], m_s[1, s, rows, :]]
            dens = [den1[:, sl], den_s[0, s, rows, :], den_s[1, s, rows, :]]
            accs = [acc1[:, sl], acc_s[0, s, rows, :], acc_s[1, s, rows, :]]
            top = jnp.maximum(jnp.maximum(ms[0], ms[1]), ms[2])
            ws = [jnp.exp(mm - top) for mm in ms]
            num = ws[0] * accs[0] + ws[1] * accs[1] + ws[2] * accs[2]
            dn = ws[0] * dens[0] + ws[1] * dens[1] + ws[2] * dens[2]
            cols.append(num / dn)
        oa_ref[pl.ds(r0, BLK), :] = jnp.concatenate(cols, axis=1).astype(oa_ref.dtype)

        q = qb_ref[pl.ds(r0, BLK)]
        zero = jnp.zeros((BLK, LANES), q.dtype)
        parts = []
        for c in range(B_Q // LANES):
            qc = q[:, c * LANES:(c + 1) * LANES]
            parts += [jnp.where(head_b_bf == 0, qc, zero), jnp.where(head_b_bf == 1, qc, zero)]
        o, den, m = softmax_block(jnp.concatenate(parts, axis=0), window(kb_ref, (), r0),
                                  window(vb_ref, (), r0), with_prev(band_b, b > 0), N_HEADS_B)
        cols = []
        for c in range(B_Q // LANES):
            ra = slice(2 * c * BLK, (2 * c + 1) * BLK)
            rb = slice((2 * c + 1) * BLK, (2 * c + 2) * BLK)
            o_c = jnp.where(lo, o[ra], o[rb])
            den_c = jnp.where(lo, jnp.broadcast_to(den[ra], (BLK, LANES)),
                              jnp.broadcast_to(den[rb], (BLK, LANES)))
            m_c = jnp.where(lo, jnp.broadcast_to(m[ra], (BLK, LANES)),
                            jnp.broadcast_to(m[rb], (BLK, LANES)))
            sink_c = sink_lanes[:, c * LANES:(c + 1) * LANES]
            cols.append(o_c / (den_c + jnp.exp(sink_c - m_c)))
        ob_ref[pl.ds(r0, BLK), :] = jnp.concatenate(cols, axis=1).astype(ob_ref.dtype)
        return carry

    lax.fori_loop(0, n_blk, tokens_body, 0, unroll=4)


def _attn_prompt(sinks_p, qkv, n_batch, seq):
    (q1, k1, v1, q2, k2, v2, q3, k3, v3, qb, kb, vb) = qkv
    tok = lambda w: pl.BlockSpec((seq, w), lambda n: (n, 0))
    cls = lambda dil: pl.BlockSpec((None, dil, seq // dil, A_OUT), lambda n: (n, 0, 0, 0))
    return pl.pallas_call(
        _attn_prompt_kernel,
        out_shape=[jax.ShapeDtypeStruct((n_batch * seq, A_OUT), _BF16),
                   jax.ShapeDtypeStruct((n_batch * seq, B_Q), _BF16)],
        grid=(n_batch,),
        in_specs=[_resident((1, B_Q))]
        + [tok(A_OUT)] * 3 + [cls(4)] * 3 + [cls(16)] * 3 + [tok(B_Q), tok(B_KV), tok(B_KV)],
        out_specs=[tok(A_OUT), tok(B_Q)],
        scratch_shapes=[pltpu.VMEM((N_GROUPS_A - 1, A_OUT // LANES, seq, LANES), _F32)] * 3,
        compiler_params=_cparams(1),
        name="attn_prompt",
    )(jnp.repeat(sinks_p, HEAD_DIM)[None, :], q1, k1, v1, q2, k2, v2, q3, k3, v3, qb, kb, vb)


def _attn_sample_body(sink_ref, q_ref, kvt_ref, cache_refs, oa_ref, ob_ref, new_refs, n, t_new):
    c1_ref, c2_ref, c3_ref, cb_ref = cache_refs
    n1_ref, n2_ref, n3_ref, nb_ref = new_refs
    per_tile = LANES // t_new
    first_new = LANES - t_new
    shift = first_new - (n & (per_tile - 1)) * t_new
    new_t = pltpu.roll(kvt_ref[...], shift, 1)
    q = q_ref[...]

    def new_cache(old_ref, new_rows, out_ref):
        lb = old_ref.shape[1]
        rolled = pltpu.roll(old_ref[...], lb - t_new, 1)
        if lb > LANES:
            out_ref[:, :lb - LANES] = rolled[:, :lb - LANES]
        lane = lax.broadcasted_iota(jnp.int32, (old_ref.shape[0], LANES), 1)
        out_ref[:, lb - LANES:] = jnp.where(lane >= first_new, new_rows, rolled[:, lb - LANES:])

    def masks(rows, lb, window, dil, strict):
        t_old = lax.broadcasted_iota(jnp.int32, (rows, lb), 0) & (t_new - 1)
        dist = lb + t_old - lax.broadcasted_iota(jnp.int32, (rows, lb), 1)
        hi = (dist < window) if strict else (dist <= window)
        valid_old = (dist >= 0) & hi & ((dist & (dil - 1)) == 0)
        t_nw = lax.broadcasted_iota(jnp.int32, (rows, LANES), 0) & (t_new - 1)
        c_nw = lax.broadcasted_iota(jnp.int32, (rows, LANES), 1) - first_new
        dn = t_nw - c_nw
        hin = (dn < window) if strict else (dn <= window)
        valid_new = (c_nw >= 0) & (dn >= 0) & hin & ((dn & (dil - 1)) == 0)
        return valid_old, valid_new

    def attend(qs, k_old, v_old, k_new, v_new, valid_old, valid_new):
        s_old = jnp.where(valid_old, _dot(qs, k_old), NEG)
        s_new = jnp.where(valid_new, _dot(qs, k_new), NEG)
        m = jnp.maximum(jnp.max(s_old, axis=-1, keepdims=True),
                        jnp.max(s_new, axis=-1, keepdims=True))
        e_old = jnp.exp(s_old - m)
        e_new = jnp.exp(s_new - m)
        den = jnp.sum(e_old, axis=-1, keepdims=True) + jnp.sum(e_new, axis=-1, keepdims=True)
        o = _dot_nt(e_old.astype(_BF16), v_old) + _dot_nt(e_new.astype(_BF16), v_new)
        return o, den, m

    def classes_reduce(x, dil, op):
        sh = dil
        while sh < LANES:
            x = op(x, pltpu.roll(x, sh, 1))
            sh *= 2
        return x

    def fold_tiles(x, op):
        acc = x[:, :LANES]
        for jt in range(1, x.shape[1] // LANES):
            acc = op(acc, x[:, jt * LANES:(jt + 1) * LANES])
        return acc

    def attend_lane_classes(old_ref, new_g, q_t, window, dil):
        lb = old_ref.shape[1]
        n_tiles = lb // LANES
        assert t_new <= dil and lb % dil == 0 and LANES % dil == 0 and lb <= window
        lane_q = lax.broadcasted_iota(jnp.int32, (A_OUT, LANES), 1)
        q_new = jnp.where(lane_q >= first_new, q_t, 0.0)
        q_cls = classes_reduce(pltpu.roll(q_new, t_new, 1), dil, jnp.add)
        lane_1 = lax.broadcasted_iota(jnp.int32, (1, LANES), 1)
        cls_ok = jnp.concatenate([(lane_1 & (dil - 1)) < t_new] * n_tiles, axis=1)
        o_rows, lse_rows = [], []
        for hh in range(HEADS_A):
            rows = slice(hh * HEAD_DIM, (hh + 1) * HEAD_DIM)
            q_h = jnp.concatenate([q_cls[rows]] * n_tiles, axis=1)
            s_old = jnp.sum(old_ref[rows, :] * q_h, axis=0, keepdims=True)
            s_old = jnp.where(cls_ok, s_old, NEG)
            s_new = jnp.sum(new_g[rows] * q_new[rows], axis=0, keepdims=True)
            s_new = jnp.where(lane_1 < t_new, pltpu.roll(s_new, t_new, 1), NEG)
            m = classes_reduce(jnp.maximum(fold_tiles(s_old, jnp.maximum), s_new), dil, jnp.maximum)
            e_old = jnp.where(cls_ok, jnp.exp(s_old - jnp.concatenate([m] * n_tiles, axis=1)), 0.0)
            e_new = jnp.where(lane_1 < t_new, jnp.exp(s_new - m), 0.0)
            den = classes_reduce(fold_tiles(e_old, jnp.add) + e_new, dil, jnp.add)
            den = jnp.where((lane_1 & (dil - 1)) < t_new, den, 1.0)
            v_rows = slice(A_OUT + hh * HEAD_DIM, A_OUT + (hh + 1) * HEAD_DIM)
            acc = fold_tiles(old_ref[v_rows, :] * e_old, jnp.add)
            acc = acc + pltpu.roll(new_g[v_rows], t_new, 1) * e_new
            acc = classes_reduce(acc, dil, jnp.add)
            o_rows.append(acc / den)
            lse_rows.append(jnp.broadcast_to(m + jnp.log(den), (HEAD_DIM, LANES)))
        out_t = jnp.concatenate(o_rows, axis=0)
        lse_t = jnp.concatenate(lse_rows, axis=0)
        return out_t.T[:t_new], lse_t.T[:t_new]

    outs, lses = [], []
    for g, (old_ref, out_ref) in enumerate(((c1_ref, n1_ref), (c2_ref, n2_ref), (c3_ref, n3_ref))):
        window, dil = DIL_PAIRS[g]
        lb = old_ref.shape[1]
        new_g = new_t[g * KV_ROWS_A:(g + 1) * KV_ROWS_A]
        if g == N_GROUPS_A - 1:
            out, lse = attend_lane_classes(old_ref, new_g, new_t[KVT_ROWS:KVT_ROWS + A_OUT],
                                           window, dil)
            outs.append(out)
            lses.append(lse)
            new_cache(old_ref, new_g, out_ref)
            continue
        qs = _head_stack_a(q[:, g * A_OUT:(g + 1) * A_OUT]).astype(_BF16)
        valid_old, valid_new = masks(HEADS_A * t_new, lb, window, dil, False)
        o, den, m = attend(qs, old_ref[0:A_OUT, :].astype(_BF16), old_ref[A_OUT:, :].astype(_BF16),
                           new_g[:A_OUT].astype(_BF16), new_g[A_OUT:].astype(_BF16),
                           valid_old, valid_new)
        outs.append(_head_unstack_a(o, 1.0 / den, t_new))
        lses.append(_head_unstack_a(None, m + jnp.log(den), t_new))
        new_cache(old_ref, new_g, out_ref)
    m = jnp.maximum(jnp.maximum(lses[0], lses[1]), lses[2])
    ws = [jnp.exp(l - m) for l in lses]
    oa_ref[...] = (ws[0] * outs[0] + ws[1] * outs[1] + ws[2] * outs[2]) / (ws[0] + ws[1] + ws[2])

    lb = cb_ref.shape[1]
    new_b = new_t[N_GROUPS_A * KV_ROWS_A:KVT_ROWS]
    qs = _head_stack_b(q[:, N_GROUPS_A * A_OUT:]).astype(_BF16)
    valid_old, valid_new = masks(N_HEADS_B * t_new, lb, WINDOW_B, 1, True)
    o, den, m = attend(qs, cb_ref[0:B_KV, :].astype(_BF16), cb_ref[B_KV:, :].astype(_BF16),
                       new_b[:B_KV].astype(_BF16), new_b[B_KV:].astype(_BF16),
                       valid_old, valid_new)
    scale = _sigmoid(m + jnp.log(den) - _sink_rows(sink_ref, t_new)) / den
    ob_ref[...] = _head_unstack_b(o, scale, t_new)
    new_cache(cb_ref, new_b, nb_ref)


def _mix_out_kernel(x_ref, g_ref, oa_ref, ob_ref, wga_ref, wgb_ref, wba_ref, wbb_ref, wout_ref,
                    o_ref):
    x = x_ref[...]
    h = _rms(x, g_ref[...]).astype(_BF16)
    gate_a = _dot(h, wga_ref[...])
    gate_b = _dot(h, wgb_ref[...])
    ya = _dot(oa_ref[...].astype(_BF16), wba_ref[...])
    yb = _dot(ob_ref[...].astype(_BF16), wbb_ref[...])
    merged = _sigmoid(gate_a) * ya + _sigmoid(gate_b) * yb
    o_ref[...] = x + _dot(merged.astype(_BF16), wout_ref[...])


def _mix_out(x, layer, g, oa, ob, w_in, w_ba, w_bb, w_out, tm):
    n_tok, d = x.shape
    assert QKV_COLS % d == 0 and w_in.shape[2] == QKV_COLS + 2 * d
    gate_specs = [_layer_block(w_in, layer, (d, d), (0, QKV_COLS // d + k)) for k in range(2)]
    tok = lambda w: pl.BlockSpec((tm, w), lambda i: (i, 0))
    return pl.pallas_call(
        _mix_out_kernel,
        out_shape=jax.ShapeDtypeStruct((n_tok, d), _F32),
        grid=(n_tok // tm,),
        in_specs=[tok(d), _layer_block(g, layer), tok(A_OUT), tok(B_Q)] + gate_specs
        + [_layer_block(a, layer) for a in (w_ba, w_bb, w_out)],
        out_specs=tok(d),
        compiler_params=_cparams(1),
        name="mix_out",
    )(x, g, oa, ob, w_in, w_in, w_ba, w_bb, w_out)


def _cache_view(c):
    dep, nb, rows = c.shape[:3]
    return jnp.transpose(c, (0, 1, 3, 4, 5, 2)).reshape(dep, nb, -1, rows)


def _cache_unview(c, heads):
    dep, nb, _, rows = c.shape
    return jnp.transpose(c.reshape(dep, nb, 2, heads, HEAD_DIM, rows), (0, 1, 5, 2, 3, 4))


def kernel(x_prompt, x_sample, cache_a1, cache_a2, cache_a3, cache_b, norm_ffn1, ffn1_gate, ffn1_up, ffn1_down, norm_mix, w_in, sinks, w_branch_a, w_branch_b, w_out, norm_ffn2, ffn2_gate, ffn2_up, ffn2_down, norm_final):
    n_batch, seq, d = x_prompt.shape
    dec_batch, t_new, _ = x_sample.shape
    depth = w_in.shape[0]
    n_sample = dec_batch * t_new
    tm_s = min(TM, n_sample)
    hosts_per_layer = 2
    n_hosted = dec_batch // hosts_per_layer
    tm_p = n_batch * seq // n_hosted
    assert seq == DIL_PAIRS[2][0] and LANES % t_new == 0 and n_sample % tm_s == 0
    assert tm_s % t_new == 0 and dec_batch % (LANES // t_new) == 0
    assert dec_batch % hosts_per_layer == 0 and (n_batch * seq) % n_hosted == 0 and tm_p % 8 == 0

    perm = jnp.array(B_HEAD_PERM)
    col_b = 3 * N_GROUPS_A * A_OUT

    cos_p, sin_p = _rope_tables(jnp.arange(seq))
    cos_s, sin_s = _rope_tables(PAST_LEN + jnp.arange(t_new))
    cos_s, sin_s = jnp.tile(cos_s, (tm_s // t_new, 1)), jnp.tile(sin_s, (tm_s // t_new, 1))
    nat_p, nat_s = _nat_tables(cos_p, sin_p), _nat_tables(cos_s, sin_s)
    t_p, t_s = (cos_p.T, sin_p.T), (cos_s.T, sin_s.T)

    caches = [_cache_view(c) for c in (cache_a1, cache_a2, cache_a3, cache_b)]

    bf = lambda w: w.astype(_BF16)
    w_in_bf = bf(w_in)
    cols = lambda lo, hi: w_in_bf[:, :, lo:hi]
    bq = cols(col_b, col_b + B_Q).reshape(depth, d, N_HEADS_B, HEAD_DIM)[:, :, perm]
    bq = bq.reshape(depth, d, B_Q)
    q_cols = [cols(3 * g * A_OUT, (3 * g + 1) * A_OUT) for g in range(N_GROUPS_A)]
    kv_cols = [cols((3 * g + 1) * A_OUT, (3 * g + 3) * A_OUT) for g in range(N_GROUPS_A)]
    kv_cols.append(cols(col_b + B_Q, QKV_COLS))
    w_qkv = jnp.concatenate([cols(0, col_b), bq, cols(col_b + B_Q, QKV_COLS)], axis=2)
    w_kvt = jnp.swapaxes(jnp.concatenate(kv_cols + [q_cols[-1]], axis=2), 1, 2)
    w_q = jnp.concatenate(q_cols + [bq], axis=2)
    w_bb = bf(w_branch_b).reshape(depth, N_HEADS_B, HEAD_DIM, d)[:, perm].reshape(depth, B_Q, d)
    w_ba, w_o = bf(w_branch_a), bf(w_out)
    sinks_perm = sinks[:, perm]
    row = lambda v: v.reshape(-1, 1, d)
    f1 = (row(norm_ffn1), bf(ffn1_gate), bf(ffn1_up), bf(ffn1_down))
    f2 = (row(norm_ffn2), bf(ffn2_gate), bf(ffn2_up), bf(ffn2_down))
    g_mix = row(norm_mix)
    gf = norm_final.reshape(1, d)

    xp = x_prompt.reshape(n_batch * seq, d)
    xs = x_sample.reshape(dec_batch * t_new, d)
    prompt_bufs = []
    sample_bufs = None
    for l in range(depth):
        last = l == depth - 1
        sinks_p = sinks_perm[l]

        xs = _ffn(xs, l, *f1, gf, final_norm=False, tm=tm_s)[0]
        q_nat, kvt_new = _mix_in_sample(xs, l, g_mix, w_q, w_kvt, nat_s, t_s, tm_s)
        host = lambda n_base: (n_base, t_new, sinks_p, q_nat, kvt_new, caches, sample_bufs)

        xp, oa_0, ob_0, *sample_bufs = _ffn(xp, l, *f1, gf, final_norm=False, tm=tm_p,
                                            host=host(0))
        outs = _mix_in_prompt(xp, l, g_mix, w_qkv, w_kvt, nat_p, t_p, n_batch, seq, prompt_bufs)
        prompt_bufs = list(outs[12:])
        oa_p, ob_p = _attn_prompt(sinks_p, outs[:12], n_batch, seq)
        xp = _mix_out(xp, l, g_mix, oa_p, ob_p, w_in_bf, w_ba, w_bb, w_o, TM)
        xp, oa_1, ob_1, *sample_bufs = _ffn(xp, l, *f2, gf, final_norm=last, tm=tm_p,
                                            host=host(n_hosted))

        oa_s = jnp.concatenate([oa_0, oa_1], axis=0)
        ob_s = jnp.concatenate([ob_0, ob_1], axis=0)
        xs = _mix_out(xs, l, g_mix, oa_s, ob_s, w_in_bf, w_ba, w_bb, w_o, tm_s)
        xs = _ffn(xs, l, *f2, gf, final_norm=last, tm=tm_s)[0]

    heads = (HEADS_A, HEADS_A, HEADS_A, N_KV_B)
    new_p = [_cache_unview(prompt_bufs[i], heads[i]) for i in range(4)]
    new_s = [_cache_unview(sample_bufs[i], heads[i]) for i in range(4)]
    return (xp.reshape(n_batch, seq, d), xs.reshape(dec_batch, t_new, d), *new_p, *new_s)
```

```python
import functools

import jax
import jax.numpy as jnp
from jax import lax
from jax.experimental import pallas as pl
from jax.experimental.pallas import tpu as pltpu

PAST_LEN = 16384
HEAD_DIM = 64
ROT_DIM = HEAD_DIM // 4
ROT_HALF = ROT_DIM // 2
ROPE_THETA = 500000.0
DIL_PAIRS = ((128, 1), (512, 4), (2048, 16))
N_GROUPS_A = 3
HEADS_A = 4
A_OUT = HEADS_A * HEAD_DIM
N_HEADS_B = 8
N_KV_B = 2
WINDOW_B = 128
B_Q = N_HEADS_B * HEAD_DIM
B_KV = N_KV_B * HEAD_DIM
QKV_COLS = N_GROUPS_A * 3 * A_OUT + B_Q + 2 * B_KV
KV_ROWS_A = 2 * A_OUT
KV_ROWS_B = 2 * B_KV
KVT_ROWS = N_GROUPS_A * KV_ROWS_A + KV_ROWS_B
VPU_GROUPS = (2,)
KVT_ROWS_Q = KVT_ROWS + len(VPU_GROUPS) * A_OUT
NORM_EPS = 1e-6
NEG = -1e30
SCALE = HEAD_DIM ** -0.5
LOG2_E = 1.4426950408889634
BLK = 128
LANES = 128
MXU_TILE = 256
TM = 512
FFN_CHUNKS = 1
B_HEAD_PERM = (0, 4, 1, 5, 2, 6, 3, 7)
V7X_SCOPED_VMEM_BYTES = 60000 * 1024

_BF16 = jnp.bfloat16
_F32 = jnp.float32


def _cparams(n_axes):
    return pltpu.CompilerParams(
        dimension_semantics=("arbitrary",) * n_axes,
        vmem_limit_bytes=V7X_SCOPED_VMEM_BYTES,
    )


def _resident(shape):
    nd = len(shape)
    return pl.BlockSpec(shape, lambda *_: (0,) * nd, pipeline_mode=pl.Buffered(1))


def _layer_block(arr, layer, block=None, index=None):
    block = tuple(arr.shape[1:]) if block is None else tuple(block)
    index = (0,) * len(block) if index is None else tuple(index)
    return pl.BlockSpec((None,) + block, lambda *_: (layer,) + index,
                        pipeline_mode=pl.Buffered(1))


def _rms(x, g):
    return x * lax.rsqrt(jnp.mean(x * x, axis=-1, keepdims=True) + NORM_EPS) * g


def _sigmoid(x):
    return 1.0 / (1.0 + jnp.exp(-x))


def _dot(a, b):
    return jnp.dot(a, b, preferred_element_type=_F32)


def _dot_nt(a, b):
    return lax.dot_general(a, b, (((1,), (1,)), ((), ())), preferred_element_type=_F32)


def _ffn_kernel(x_ref, g_ref, wg_ref, wu_ref, wd_ref, gf_ref, *rest, bounds, final_norm, host):
    o_ref = rest[-1] if host is None else rest[-7]
    if host is not None:
        n_base, t_new = host
        _attn_sample_body(rest[0], rest[1], rest[2], rest[3:7], rest[-6], rest[-5], rest[-4:],
                          n_base + pl.program_id(0), t_new)
    x = x_ref[...]
    h = _rms(x, g_ref[...]).astype(_BF16)
    acc = None
    for lo, hi in zip(bounds[:-1], bounds[1:]):
        gate = _dot(h, wg_ref[:, lo:hi])
        up = _dot(h, wu_ref[:, lo:hi])
        act = (gate * _sigmoid(gate) * up).astype(_BF16)
        part = _dot(act, wd_ref[lo:hi, :])
        acc = part if acc is None else acc + part
    y = x + 0.5 * acc
    if final_norm:
        y = _rms(y, gf_ref[...])
    o_ref[...] = y


def _ffn(x, layer, g, wg, wu, wd, gf, *, final_norm, tm, host=None):
    n_tok, d = x.shape
    f = wg.shape[-1]
    steps = n_tok // tm
    n_tiles = f // MXU_TILE
    cuts = [0, (n_tiles + 1) // 2 * MXU_TILE, f] if FFN_CHUNKS == 2 and n_tiles > 1 else [0, f]
    tok_spec = pl.BlockSpec((tm, d), lambda i: (i, 0))
    in_specs = ([tok_spec] + [_layer_block(a, layer) for a in (g, wg, wu, wd)]
                + [_resident((1, d))])
    out_shape = [jax.ShapeDtypeStruct((n_tok, d), _F32)]
    out_specs = [tok_spec]
    operands = [x, g, wg, wu, wd, gf]
    aliases = {}
    host_static = None
    if host is not None:
        n_base, t_new, sinks_p, q_nat, kvt_new, caches, prev_out = host
        per_tile = LANES // t_new
        host_static = (n_base, t_new)
        cache_spec = lambda c: pl.BlockSpec((None, None) + c.shape[2:],
                                            lambda i: (layer, n_base + i, 0, 0))
        prev_out = [] if prev_out is None else list(prev_out)
        in_specs += ([pl.BlockSpec(memory_space=pltpu.SMEM),
                      pl.BlockSpec((t_new, q_nat.shape[1]), lambda i: (n_base + i, 0)),
                      pl.BlockSpec((KVT_ROWS_Q, LANES), lambda i: (0, (n_base + i) // per_tile))]
                     + [cache_spec(c) for c in caches]
                     + [pl.BlockSpec(memory_space=pl.ANY)] * len(prev_out))
        aliases = {len(operands) + 3 + len(caches) + k: 3 + k for k in range(len(prev_out))}
        operands += [sinks_p, q_nat, kvt_new, *caches, *prev_out]
        out_shape += ([jax.ShapeDtypeStruct((steps * t_new, A_OUT), _F32),
                       jax.ShapeDtypeStruct((steps * t_new, B_Q), _F32)]
                      + [jax.ShapeDtypeStruct(c.shape, c.dtype) for c in caches])
        out_specs += ([pl.BlockSpec((t_new, A_OUT), lambda i: (i, 0)),
                       pl.BlockSpec((t_new, B_Q), lambda i: (i, 0))]
                      + [cache_spec(c) for c in caches])
    kern = functools.partial(_ffn_kernel, bounds=tuple(cuts), final_norm=final_norm,
                             host=host_static)
    name = ("ffn_final" if final_norm else "ffn") + ("" if host is None else "_attn_sample")
    return pl.pallas_call(
        kern,
        out_shape=out_shape,
        grid=(steps,),
        in_specs=in_specs,
        out_specs=out_specs,
        input_output_aliases=aliases,
        compiler_params=_cparams(1),
        name=name,
    )(*operands)


def _rope_nat(x, c, s_lo, s_hi):
    return x * c + pltpu.roll(x, ROT_HALF, 1) * s_hi + pltpu.roll(x, LANES - ROT_HALF, 1) * s_lo


def _rope_t(k_t, cos_t, sin_t, n_heads):
    parts = []
    for hh in range(n_heads):
        b = hh * HEAD_DIM
        x1 = k_t[b:b + ROT_HALF]
        x2 = k_t[b + ROT_HALF:b + ROT_DIM]
        parts += [x1 * cos_t - x2 * sin_t, x2 * cos_t + x1 * sin_t, k_t[b + ROT_DIM:b + HEAD_DIM]]
    return jnp.concatenate(parts, axis=0)


def _rope_tables(pos):
    inv = ROPE_THETA ** (-(jnp.arange(ROT_HALF, dtype=_F32) * 2.0 / ROT_DIM))
    ang = pos.astype(_F32)[:, None] * inv[None, :]
    return jnp.cos(ang), jnp.sin(ang)


def _nat_tables(cos, sin):
    p = cos.shape[0]
    one = jnp.ones((p, HEAD_DIM - ROT_DIM), _F32)
    zero = jnp.zeros((p, HEAD_DIM - ROT_DIM), _F32)
    zh = jnp.zeros((p, ROT_HALF), _F32)
    c = jnp.concatenate([cos, cos, one], axis=1)
    s_hi = jnp.concatenate([zh, sin, zero], axis=1)
    s_lo = jnp.concatenate([-sin, zh, zero], axis=1)
    rep = LANES // HEAD_DIM
    return tuple(jnp.tile(t, (1, rep)) for t in (c, s_lo, s_hi))


def _mix_in_prompt_kernel(x_ref, g_ref, w_ref, wt_ref, c_ref, slo_ref, shi_ref, ct_ref, st_ref,
                          *rest, n_prev):
    prev = rest[:4] if n_prev else None
    (q1_ref, k1_ref, v1_ref, q2_ref, k2_ref, v2_ref, q3_ref, k3_ref, v3_ref,
     qb_ref, kb_ref, vb_ref, c1_ref, c2_ref, c3_ref, cb_ref, zs_ref) = rest[-17:]
    j = pl.program_id(1)
    tm = x_ref.shape[0]
    h = _rms(x_ref[...], g_ref[...]).astype(_BF16)
    c, s_lo, s_hi = c_ref[...], slo_ref[...], shi_ref[...]

    def proj(col0, width, kind):
        z = _dot(h, w_ref[:, col0:col0 + width])
        if kind == "v":
            return z
        slabs = []
        for s in range(width // LANES):
            zz = _rope_nat(z[:, s * LANES:(s + 1) * LANES], c, s_lo, s_hi)
            slabs.append(zz * (SCALE * LOG2_E) if kind == "q" else zz)
        return jnp.concatenate(slabs, axis=1)

    for i, (ref, kind) in enumerate(((q1_ref, "q"), (k1_ref, "k"), (v1_ref, "v"))):
        ref[...] = proj(i * A_OUT, A_OUT, kind).astype(_BF16)
    for g, refs in ((1, (q2_ref, k2_ref, v2_ref)), (2, (q3_ref, k3_ref, v3_ref))):
        dil = DIL_PAIRS[g][1]
        rows = tm // dil
        for i, (ref, kind) in enumerate(zip(refs, ("q", "k", "v"))):
            z = proj((3 * g + i) * A_OUT, A_OUT, kind)
            for s in range(A_OUT // LANES):
                zs_ref[s] = z[:, s * LANES:(s + 1) * LANES]
            for r in range(dil):
                for s in range(A_OUT // LANES):
                    ref[r, :, s * LANES:(s + 1) * LANES] = (
                        zs_ref[s, pl.ds(r, rows, stride=dil), :].astype(_BF16))
    col_b = 3 * N_GROUPS_A * A_OUT
    qb_ref[...] = proj(col_b, B_Q, "q").astype(_BF16)
    z = _dot(h, w_ref[:, col_b + B_Q:col_b + B_Q + 2 * B_KV])
    kb_ref[...] = _rope_nat(z[:, :B_KV], c, s_lo, s_hi).astype(_BF16)
    vb_ref[...] = z[:, B_KV:].astype(_BF16)

    def kv_t(row0, n_rows, h_tok, cos_t, sin_t, n_heads):
        z = _dot_nt(wt_ref[row0:row0 + n_rows, :], h_tok)
        half = n_rows // 2
        return jnp.concatenate([_rope_t(z[:half], cos_t, sin_t, n_heads), z[half:]], axis=0)

    def put(i, out_ref, new):
        if n_prev:
            out_ref[:n_prev] = prev[i][...]
        out_ref[n_prev] = new

    ct, st = ct_ref[...], st_ref[...]
    put(2, c3_ref, kv_t(2 * KV_ROWS_A, KV_ROWS_A, h, ct, st, HEADS_A))

    @pl.when(j == pl.num_programs(1) - 1)
    def _():
        put(1, c2_ref, kv_t(KV_ROWS_A, KV_ROWS_A, h, ct, st, HEADS_A))
        tail = tm - BLK
        put(0, c1_ref, kv_t(0, KV_ROWS_A, h[tail:], ct[:, tail:], st[:, tail:], HEADS_A))
        put(3, cb_ref, kv_t(3 * KV_ROWS_A, KV_ROWS_B, h[tail:], ct[:, tail:], st[:, tail:], N_KV_B))


def _mix_in_prompt(x, layer, g, w_qkv, w_kvt, nat_tabs, t_tabs, n_batch, seq, prev_bufs):
    d = x.shape[1]
    n_j = seq // TM
    n_prev = prev_bufs[0].shape[0] if prev_bufs else 0
    row = lambda n, j: (n * n_j + j, 0)
    tok_spec = lambda w: pl.BlockSpec((TM, w), row)
    cls_spec = lambda dil: pl.BlockSpec((None, dil, TM // dil, A_OUT), lambda n, j: (n, 0, j, 0))
    nat = lambda w: jax.ShapeDtypeStruct((n_batch * seq, w), _BF16)
    cls = lambda dil: jax.ShapeDtypeStruct((n_batch, dil, seq // dil, A_OUT), _BF16)
    keep = [min(w, seq) for w, _ in DIL_PAIRS]
    assert keep == [BLK, TM, seq] and seq % TM == 0
    bufs = ((KV_ROWS_A, keep[0], BLK, lambda j: 0), (KV_ROWS_A, keep[1], TM, lambda j: 0),
            (KV_ROWS_A, keep[2], TM, lambda j: j), (KV_ROWS_B, min(WINDOW_B, seq), BLK, lambda j: 0))

    def buf_spec(layers, rows, width, jmap):
        return pl.BlockSpec((layers, None, rows, width), lambda n, j: (0, n, 0, jmap(j)))

    def buf_shape(layers, rows, kept):
        return jax.ShapeDtypeStruct((layers, n_batch, rows, kept), _F32)

    n_out = n_prev + 1
    out_shape = ([nat(A_OUT)] * 3 + [cls(4)] * 3 + [cls(16)] * 3 + [nat(B_Q), nat(B_KV), nat(B_KV)]
                 + [buf_shape(n_out, r, k) for r, k, _, _ in bufs])
    out_specs = ([tok_spec(A_OUT)] * 3 + [cls_spec(4)] * 3 + [cls_spec(16)] * 3
                 + [tok_spec(B_Q), tok_spec(B_KV), tok_spec(B_KV)]
                 + [buf_spec(n_out, r, w, jm) for r, _, w, jm in bufs])
    prev_specs = [buf_spec(n_prev, r, w, jm) for r, _, w, jm in bufs] if n_prev else []
    tab_spec = pl.BlockSpec((TM, LANES), lambda n, j: (j, 0))
    ttab_spec = pl.BlockSpec((ROT_HALF, TM), lambda n, j: (0, j))
    return pl.pallas_call(
        functools.partial(_mix_in_prompt_kernel, n_prev=n_prev),
        out_shape=out_shape,
        grid=(n_batch, n_j),
        in_specs=[tok_spec(d)] + [_layer_block(a, layer) for a in (g, w_qkv, w_kvt)]
        + [tab_spec, tab_spec, tab_spec, ttab_spec, ttab_spec] + prev_specs,
        out_specs=out_specs,
        scratch_shapes=[pltpu.VMEM((A_OUT // LANES, TM, LANES), _F32)],
        compiler_params=_cparams(2),
        name="mix_in_prompt",
    )(x, g, w_qkv, w_kvt, *nat_tabs, *t_tabs, *prev_bufs)


def _mix_in_sample_kernel(x_ref, g_ref, w_ref, wt_ref, c_ref, slo_ref, shi_ref, ct_ref, st_ref,
                          q_ref, kvt_ref):
    h = _rms(x_ref[...], g_ref[...]).astype(_BF16)
    c, s_lo, s_hi = c_ref[...], slo_ref[...], shi_ref[...]
    z = _dot(h, w_ref[...])
    for s in range(z.shape[1] // LANES):
        sl = slice(s * LANES, (s + 1) * LANES)
        q_ref[:, sl] = _rope_nat(z[:, sl], c, s_lo, s_hi) * SCALE
    ct, st = ct_ref[...], st_ref[...]
    for g in range(N_GROUPS_A):
        r0 = g * KV_ROWS_A
        zt = _dot_nt(wt_ref[r0:r0 + KV_ROWS_A, :], h)
        kvt_ref[r0:r0 + A_OUT, :] = _rope_t(zt[:A_OUT], ct, st, HEADS_A)
        kvt_ref[r0 + A_OUT:r0 + KV_ROWS_A, :] = zt[A_OUT:]
    r0 = N_GROUPS_A * KV_ROWS_A
    zt = _dot_nt(wt_ref[r0:r0 + KV_ROWS_B, :], h)
    kvt_ref[r0:r0 + B_KV, :] = _rope_t(zt[:B_KV], ct, st, N_KV_B)
    kvt_ref[r0 + B_KV:r0 + KV_ROWS_B, :] = zt[B_KV:]
    for i in range(len(VPU_GROUPS)):
        r0 = KVT_ROWS + i * A_OUT
        zt = _dot_nt(wt_ref[r0:r0 + A_OUT, :], h)
        kvt_ref[r0:r0 + A_OUT, :] = _rope_t(zt, ct, st, HEADS_A) * SCALE


def _mix_in_sample(x, layer, g, w_q, w_kvt, nat_tabs, t_tabs, tm):
    n_tok, d = x.shape
    qw = w_q.shape[-1]
    tab_spec = pl.BlockSpec((tm, LANES), lambda i: (0, 0))
    ttab_spec = pl.BlockSpec((ROT_HALF, tm), lambda i: (0, 0))
    assert w_kvt.shape[1] == KVT_ROWS_Q
    return pl.pallas_call(
        _mix_in_sample_kernel,
        out_shape=[jax.ShapeDtypeStruct((n_tok, qw), _F32),
                   jax.ShapeDtypeStruct((KVT_ROWS_Q, n_tok), _F32)],
        grid=(n_tok // tm,),
        in_specs=[pl.BlockSpec((tm, d), lambda i: (i, 0))]
        + [_layer_block(a, layer) for a in (g, w_q, w_kvt)]
        + [tab_spec, tab_spec, tab_spec, ttab_spec, ttab_spec],
        out_specs=[pl.BlockSpec((tm, qw), lambda i: (i, 0)),
                   pl.BlockSpec((KVT_ROWS_Q, tm), lambda i: (0, i))],
        compiler_params=_cparams(1),
        name="mix_in_sample",
    )(x, g, w_q, w_kvt, *nat_tabs, *t_tabs)


def _softmax_rows(s, valid):
    s = jnp.where(valid, s, NEG)
    m = jnp.max(s, axis=-1, keepdims=True)
    e = jnp.exp(s - m)
    return e, jnp.sum(e, axis=-1, keepdims=True), m


def _lane_head(t, width):
    return lax.shift_right_logical(lax.broadcasted_iota(jnp.int32, (t, width), 1),
                                   HEAD_DIM.bit_length() - 1)


def _head_stack_a(q):
    lane_head = _lane_head(q.shape[0], A_OUT)
    return jnp.concatenate([jnp.where(lane_head == hh, q, 0.0) for hh in range(HEADS_A)], axis=0)


def _head_unstack_a(o_st, col_st, t):
    lane_head = _lane_head(t, A_OUT)
    out = jnp.zeros((t, A_OUT), _F32)
    for hh in range(HEADS_A):
        col = col_st[hh * t:(hh + 1) * t]
        blk = jnp.broadcast_to(col, (t, A_OUT)) if o_st is None else o_st[hh * t:(hh + 1) * t] * col
        out = jnp.where(lane_head == hh, blk, out)
    return out


def _head_stack_b(q):
    t = q.shape[0]
    lo = lax.broadcasted_iota(jnp.int32, (t, LANES), 1) < HEAD_DIM
    parts = []
    for c in range(B_Q // LANES):
        qc = q[:, c * LANES:(c + 1) * LANES]
        parts += [jnp.where(lo, qc, 0.0), jnp.where(lo, 0.0, qc)]
    return jnp.concatenate(parts, axis=0)


def _head_unstack_b(o_st, col_st, t):
    lo = lax.broadcasted_iota(jnp.int32, (t, LANES), 1) < HEAD_DIM
    cols = []
    for c in range(B_Q // LANES):
        r0 = 2 * c * t
        a = o_st[r0:r0 + t] * col_st[r0:r0 + t]
        b = o_st[r0 + t:r0 + 2 * t] * col_st[r0 + t:r0 + 2 * t]
        cols.append(jnp.where(lo, a, b))
    return jnp.concatenate(cols, axis=1)


def _sink_rows(sink_ref, t):
    return jnp.concatenate(
        [jnp.full((t, 1), sink_ref[p], _F32) for p in range(N_HEADS_B)], axis=0)


def _band_valid(n_keys, n_back, has_prev):
    r = lax.broadcasted_iota(jnp.int32, (BLK, n_keys), 0)
    c = lax.broadcasted_iota(jnp.int32, (BLK, n_keys), 1)
    dist = (BLK + r - c) if has_prev else (r - c)
    return (dist >= 0) & (dist <= n_back)


def _attn_prompt_kernel(sinkl_ref, q1_ref, k1_ref, v1_ref, q2_ref, k2_ref, v2_ref,
                        q3_ref, k3_ref, v3_ref, qb_ref, kb_ref, vb_ref,
                        oa_ref, ob_ref, acc_s, den_s, m_s, bias_s):
    seq = q1_ref.shape[0]
    n_blk = seq // BLK
    dil2, dil3 = DIL_PAIRS[1][1], DIL_PAIRS[2][1]
    per_class2 = seq // dil2 // BLK
    assert dil2 * per_class2 == dil3 == n_blk and seq // dil3 == BLK

    lane_head_a = _lane_head(BLK, A_OUT)
    head_a_bf = lane_head_a.astype(_F32).astype(_BF16)
    lane_head_b = _lane_head(BLK, LANES)
    head_b_bf = lane_head_b.astype(_F32).astype(_BF16)
    r_i = lax.broadcasted_iota(jnp.int32, (BLK, 2 * BLK), 0)
    c_i = lax.broadcasted_iota(jnp.int32, (BLK, 2 * BLK), 1)
    dist = BLK + r_i - c_i
    for i, n_back in enumerate((BLK, WINDOW_B - 1)):
        band = (dist >= 0) & (dist <= n_back)
        bias_s[2 * i] = jnp.where(band & (c_i >= BLK), 0.0, NEG)
        bias_s[2 * i + 1] = jnp.where(band, 0.0, NEG)
    causal = jnp.where(lax.broadcasted_iota(jnp.int32, (BLK, BLK), 0)
                       >= lax.broadcasted_iota(jnp.int32, (BLK, BLK), 1), 0.0, NEG)

    def with_prev(i, has_prev):
        return bias_s[2 * i + has_prev.astype(jnp.int32)]

    def window(ref, lead, r0):
        p0 = pl.multiple_of(jnp.maximum(r0 - BLK, 0), BLK)
        return jnp.concatenate([ref[lead + (pl.ds(p0, BLK),)], ref[lead + (pl.ds(r0, BLK),)]],
                               axis=0)

    def softmax_block(qs, kw, vw, bias, n_heads):
        n_keys = kw.shape[0]
        s = _dot_nt(qs, kw).reshape(n_heads, BLK, n_keys) + bias[None]
        m = jnp.max(s, axis=-1, keepdims=True)
        e = jnp.exp2(s - m)
        den = jnp.sum(e, axis=-1, keepdims=True)
        o = _dot(e.reshape(n_heads * BLK, n_keys).astype(_BF16), vw)
        return o, den.reshape(n_heads * BLK, 1), m.reshape(n_heads * BLK, 1)

    def block_a(q, kw, vw, bias):
        zero = jnp.zeros_like(q)
        qs = jnp.concatenate([jnp.where(head_a_bf == hh, q, zero) for hh in range(HEADS_A)], axis=0)
        o, den, m = softmax_block(qs, kw, vw, bias, HEADS_A)
        acc = o[:BLK]
        den_u = jnp.broadcast_to(den[:BLK], (BLK, A_OUT))
        m_u = jnp.broadcast_to(m[:BLK], (BLK, A_OUT))
        for hh in range(1, HEADS_A):
            sel = lane_head_a == hh
            rows = slice(hh * BLK, (hh + 1) * BLK)
            acc = jnp.where(sel, o[rows], acc)
            den_u = jnp.where(sel, jnp.broadcast_to(den[rows], (BLK, A_OUT)), den_u)
            m_u = jnp.where(sel, jnp.broadcast_to(m[rows], (BLK, A_OUT)), m_u)
        return acc, den_u, m_u

    def store(g, start, stride, acc, den_u, m_u):
        idx = pl.ds(start, BLK, stride=stride)
        for s in range(A_OUT // LANES):
            sl = slice(s * LANES, (s + 1) * LANES)
            acc_s[g, s, idx, :] = acc[:, sl]
            den_s[g, s, idx, :] = den_u[:, sl]
            m_s[g, s, idx, :] = m_u[:, sl]

    def classes_body(i, carry):
        r = lax.div(i, per_class2)
        sb = i - r * per_class2
        s0 = pl.multiple_of(sb * BLK, BLK)
        res = block_a(q2_ref[r, pl.ds(s0, BLK)], window(k2_ref, (r,), s0), window(v2_ref, (r,), s0),
                      with_prev(0, sb > 0))
        store(0, r + dil2 * s0, dil2, *res)
        res = block_a(q3_ref[i], k3_ref[i], v3_ref[i], causal)
        store(1, i, dil3, *res)
        return carry

    lax.fori_loop(0, n_blk, classes_body, 0, unroll=4)

    lo = lane_head_b == 0
    sink_lanes = sinkl_ref[...]

    def tokens_body(b, carry):
        r0 = pl.multiple_of(b * BLK, BLK)
        acc1, den1, m1 = block_a(q1_ref[pl.ds(r0, BLK)], window(k1_ref, (), r0),
                                 window(v1_ref, (), r0), with_prev(0, b > 0))
        cols = []
        for s in range(A_OUT // LANES):
            sl = slice(s * LANES, (s + 1) * LANES)
            rows = pl.ds(r0, BLK)
            ms = [m1[:, sl], m_s[0, s, rows, :], m_s[1, s, rows, :]]
            dens = [den1[:, sl], den_s[0, s, rows, :], den_s[1, s, rows, :]]
            accs = [acc1[:, sl], acc_s[0, s, rows, :], acc_s[1, s, rows, :]]
            top = jnp.maximum(jnp.maximum(ms[0], ms[1]), ms[2])
            ws = [jnp.exp2(mm - top) for mm in ms]
            num = ws[0] * accs[0] + ws[1] * accs[1] + ws[2] * accs[2]
            dn = ws[0] * dens[0] + ws[1] * dens[1] + ws[2] * dens[2]
            cols.append(num / dn)
        oa_ref[pl.ds(r0, BLK), :] = jnp.concatenate(cols, axis=1).astype(oa_ref.dtype)

        q = qb_ref[pl.ds(r0, BLK)]
        zero = jnp.zeros((BLK, LANES), q.dtype)
        parts = []
        for c in range(B_Q // LANES):
            qc = q[:, c * LANES:(c + 1) * LANES]
            parts += [jnp.where(head_b_bf == 0, qc, zero), jnp.where(head_b_bf == 1, qc, zero)]
        o, den, m = softmax_block(jnp.concatenate(parts, axis=0), window(kb_ref, (), r0),
                                  window(vb_ref, (), r0), with_prev(1, b > 0), N_HEADS_B)
        cols = []
        for c in range(B_Q // LANES):
            ra = slice(2 * c * BLK, (2 * c + 1) * BLK)
            rb = slice((2 * c + 1) * BLK, (2 * c + 2) * BLK)
            o_c = jnp.where(lo, o[ra], o[rb])
            den_c = jnp.where(lo, jnp.broadcast_to(den[ra], (BLK, LANES)),
                              jnp.broadcast_to(den[rb], (BLK, LANES)))
            m_c = jnp.where(lo, jnp.broadcast_to(m[ra], (BLK, LANES)),
                            jnp.broadcast_to(m[rb], (BLK, LANES)))
            sink_c = sink_lanes[:, c * LANES:(c + 1) * LANES]
            cols.append(o_c / (den_c + jnp.exp2(sink_c - m_c)))
        ob_ref[pl.ds(r0, BLK), :] = jnp.concatenate(cols, axis=1).astype(ob_ref.dtype)
        return carry

    lax.fori_loop(0, n_blk, tokens_body, 0, unroll=4)


def _attn_prompt(sinks_p, qkv, n_batch, seq):
    (q1, k1, v1, q2, k2, v2, q3, k3, v3, qb, kb, vb) = qkv
    tok = lambda w: pl.BlockSpec((seq, w), lambda n: (n, 0))
    cls = lambda dil: pl.BlockSpec((None, dil, seq // dil, A_OUT), lambda n: (n, 0, 0, 0))
    return pl.pallas_call(
        _attn_prompt_kernel,
        out_shape=[jax.ShapeDtypeStruct((n_batch * seq, A_OUT), _BF16),
                   jax.ShapeDtypeStruct((n_batch * seq, B_Q), _BF16)],
        grid=(n_batch,),
        in_specs=[_resident((1, B_Q))]
        + [tok(A_OUT)] * 3 + [cls(4)] * 3 + [cls(16)] * 3 + [tok(B_Q), tok(B_KV), tok(B_KV)],
        out_specs=[tok(A_OUT), tok(B_Q)],
        scratch_shapes=[pltpu.VMEM((N_GROUPS_A - 1, A_OUT // LANES, seq, LANES), _F32)] * 3
        + [pltpu.VMEM((4, BLK, 2 * BLK), _F32)],
        compiler_params=_cparams(1),
        name="attn_prompt",
    )(jnp.repeat(sinks_p * LOG2_E, HEAD_DIM)[None, :], q1, k1, v1, q2, k2, v2, q3, k3, v3,
      qb, kb, vb)


def _attn_sample_body(sink_ref, q_ref, kvt_ref, cache_refs, oa_ref, ob_ref, new_refs, n, t_new):
    c1_ref, c2_ref, c3_ref, cb_ref = cache_refs
    n1_ref, n2_ref, n3_ref, nb_ref = new_refs
    per_tile = LANES // t_new
    first_new = LANES - t_new
    shift = first_new - (n & (per_tile - 1)) * t_new
    new_t = pltpu.roll(kvt_ref[...], shift, 1)
    q = q_ref[...]

    def new_cache(old_ref, new_rows, out_ref):
        lb = old_ref.shape[1]
        rolled = pltpu.roll(old_ref[...], lb - t_new, 1)
        if lb > LANES:
            out_ref[:, :lb - LANES] = rolled[:, :lb - LANES]
        lane = lax.broadcasted_iota(jnp.int32, (old_ref.shape[0], LANES), 1)
        out_ref[:, lb - LANES:] = jnp.where(lane >= first_new, new_rows, rolled[:, lb - LANES:])

    def masks(rows, lb, window, dil, strict):
        t_old = lax.broadcasted_iota(jnp.int32, (rows, lb), 0) & (t_new - 1)
        dist = lb + t_old - lax.broadcasted_iota(jnp.int32, (rows, lb), 1)
        hi = (dist < window) if strict else (dist <= window)
        valid_old = (dist >= 0) & hi & ((dist & (dil - 1)) == 0)
        t_nw = lax.broadcasted_iota(jnp.int32, (rows, LANES), 0) & (t_new - 1)
        c_nw = lax.broadcasted_iota(jnp.int32, (rows, LANES), 1) - first_new
        dn = t_nw - c_nw
        hin = (dn < window) if strict else (dn <= window)
        valid_new = (c_nw >= 0) & (dn >= 0) & hin & ((dn & (dil - 1)) == 0)
        return valid_old, valid_new

    def attend(qs, k_old, v_old, k_new, v_new, valid_old, valid_new):
        s_old = jnp.where(valid_old, _dot(qs, k_old), NEG)
        s_new = jnp.where(valid_new, _dot(qs, k_new), NEG)
        m = jnp.maximum(jnp.max(s_old, axis=-1, keepdims=True),
                        jnp.max(s_new, axis=-1, keepdims=True))
        e_old = jnp.exp(s_old - m)
        e_new = jnp.exp(s_new - m)
        den = jnp.sum(e_old, axis=-1, keepdims=True) + jnp.sum(e_new, axis=-1, keepdims=True)
        o = _dot_nt(e_old.astype(_BF16), v_old) + _dot_nt(e_new.astype(_BF16), v_new)
        return o, den, m

    def classes_reduce(x, dil, op):
        sh = dil
        while sh < LANES:
            x = op(x, pltpu.roll(x, sh, 1))
            sh *= 2
        return x

    def fold_tiles(x, op):
        acc = x[:, :LANES]
        for jt in range(1, x.shape[1] // LANES):
            acc = op(acc, x[:, jt * LANES:(jt + 1) * LANES])
        return acc

    def attend_lane_classes(old_ref, new_g, q_t, window, dil):
        lb = old_ref.shape[1]
        n_tiles = lb // LANES
        assert t_new <= dil and lb % dil == 0 and LANES % dil == 0 and lb <= window
        lane_q = lax.broadcasted_iota(jnp.int32, (A_OUT, LANES), 1)
        q_new = jnp.where(lane_q >= first_new, q_t, 0.0)
        q_cls = classes_reduce(pltpu.roll(q_new, t_new, 1), dil, jnp.add)
        lane_1 = lax.broadcasted_iota(jnp.int32, (1, LANES), 1)
        cls_ok = jnp.concatenate([(lane_1 & (dil - 1)) < t_new] * n_tiles, axis=1)
        o_rows, lse_rows = [], []
        for hh in range(HEADS_A):
            rows = slice(hh * HEAD_DIM, (hh + 1) * HEAD_DIM)
            q_h = jnp.concatenate([q_cls[rows]] * n_tiles, axis=1)
            s_old = jnp.sum(old_ref[rows, :] * q_h, axis=0, keepdims=True)
            s_old = jnp.where(cls_ok, s_old, NEG)
            s_new = jnp.sum(new_g[rows] * q_new[rows], axis=0, keepdims=True)
            s_new = jnp.where(lane_1 < t_new, pltpu.roll(s_new, t_new, 1), NEG)
            m = classes_reduce(jnp.maximum(fold_tiles(s_old, jnp.maximum), s_new), dil, jnp.maximum)
            e_old = jnp.where(cls_ok, jnp.exp(s_old - jnp.concatenate([m] * n_tiles, axis=1)), 0.0)
            e_new = jnp.where(lane_1 < t_new, jnp.exp(s_new - m), 0.0)
            den = classes_reduce(fold_tiles(e_old, jnp.add) + e_new, dil, jnp.add)
            den = jnp.where((lane_1 & (dil - 1)) < t_new, den, 1.0)
            v_rows = slice(A_OUT + hh * HEAD_DIM, A_OUT + (hh + 1) * HEAD_DIM)
            acc = fold_tiles(old_ref[v_rows, :] * e_old, jnp.add)
            acc = acc + pltpu.roll(new_g[v_rows], t_new, 1) * e_new
            acc = classes_reduce(acc, dil, jnp.add)
            o_rows.append(acc / den)
            lse_rows.append(jnp.broadcast_to(m + jnp.log(den), (HEAD_DIM, LANES)))
        out_t = jnp.concatenate(o_rows, axis=0)
        lse_t = jnp.concatenate(lse_rows, axis=0)
        return out_t.T[:t_new], lse_t.T[:t_new]

    outs, lses = [], []
    for g, (old_ref, out_ref) in enumerate(((c1_ref, n1_ref), (c2_ref, n2_ref), (c3_ref, n3_ref))):
        window, dil = DIL_PAIRS[g]
        lb = old_ref.shape[1]
        new_g = new_t[g * KV_ROWS_A:(g + 1) * KV_ROWS_A]
        if g in VPU_GROUPS:
            q_row0 = KVT_ROWS + VPU_GROUPS.index(g) * A_OUT
            out, lse = attend_lane_classes(old_ref, new_g, new_t[q_row0:q_row0 + A_OUT],
                                           window, dil)
            outs.append(out)
            lses.append(lse)
            new_cache(old_ref, new_g, out_ref)
            continue
        qs = _head_stack_a(q[:, g * A_OUT:(g + 1) * A_OUT]).astype(_BF16)
        valid_old, valid_new = masks(HEADS_A * t_new, lb, window, dil, False)
        o, den, m = attend(qs, old_ref[0:A_OUT, :].astype(_BF16), old_ref[A_OUT:, :].astype(_BF16),
                           new_g[:A_OUT].astype(_BF16), new_g[A_OUT:].astype(_BF16),
                           valid_old, valid_new)
        outs.append(_head_unstack_a(o, 1.0 / den, t_new))
        lses.append(_head_unstack_a(None, m + jnp.log(den), t_new))
        new_cache(old_ref, new_g, out_ref)
    m = jnp.maximum(jnp.maximum(lses[0], lses[1]), lses[2])
    ws = [jnp.exp(l - m) for l in lses]
    oa_ref[...] = (ws[0] * outs[0] + ws[1] * outs[1] + ws[2] * outs[2]) / (ws[0] + ws[1] + ws[2])

    lb = cb_ref.shape[1]
    new_b = new_t[N_GROUPS_A * KV_ROWS_A:KVT_ROWS]
    qs = _head_stack_b(q[:, N_GROUPS_A * A_OUT:]).astype(_BF16)
    valid_old, valid_new = masks(N_HEADS_B * t_new, lb, WINDOW_B, 1, True)
    o, den, m = attend(qs, cb_ref[0:B_KV, :].astype(_BF16), cb_ref[B_KV:, :].astype(_BF16),
                       new_b[:B_KV].astype(_BF16), new_b[B_KV:].astype(_BF16),
                       valid_old, valid_new)
    scale = _sigmoid(m + jnp.log(den) - _sink_rows(sink_ref, t_new)) / den
    ob_ref[...] = _head_unstack_b(o, scale, t_new)
    new_cache(cb_ref, new_b, nb_ref)


def _mix_out_kernel(x_ref, g_ref, oa_ref, ob_ref, wga_ref, wgb_ref, wba_ref, wbb_ref, wout_ref,
                    o_ref):
    x = x_ref[...]
    h = _rms(x, g_ref[...]).astype(_BF16)
    gate_a = _dot(h, wga_ref[...])
    gate_b = _dot(h, wgb_ref[...])
    ya = _dot(oa_ref[...].astype(_BF16), wba_ref[...])
    yb = _dot(ob_ref[...].astype(_BF16), wbb_ref[...])
    merged = _sigmoid(gate_a) * ya + _sigmoid(gate_b) * yb
    o_ref[...] = x + _dot(merged.astype(_BF16), wout_ref[...])


def _mix_out(x, layer, g, oa, ob, w_in, w_ba, w_bb, w_out, tm):
    n_tok, d = x.shape
    assert QKV_COLS % d == 0 and w_in.shape[2] == QKV_COLS + 2 * d
    gate_specs = [_layer_block(w_in, layer, (d, d), (0, QKV_COLS // d + k)) for k in range(2)]
    tok = lambda w: pl.BlockSpec((tm, w), lambda i: (i, 0))
    return pl.pallas_call(
        _mix_out_kernel,
        out_shape=jax.ShapeDtypeStruct((n_tok, d), _F32),
        grid=(n_tok // tm,),
        in_specs=[tok(d), _layer_block(g, layer), tok(A_OUT), tok(B_Q)] + gate_specs
        + [_layer_block(a, layer) for a in (w_ba, w_bb, w_out)],
        out_specs=tok(d),
        compiler_params=_cparams(1),
        name="mix_out",
    )(x, g, oa, ob, w_in, w_in, w_ba, w_bb, w_out)


def _cache_view(c):
    dep, nb, rows = c.shape[:3]
    return jnp.transpose(c, (0, 1, 3, 4, 5, 2)).reshape(dep, nb, -1, rows)


def _cache_unview(c, heads):
    dep, nb, _, rows = c.shape
    return jnp.transpose(c.reshape(dep, nb, 2, heads, HEAD_DIM, rows), (0, 1, 5, 2, 3, 4))


def kernel(x_prompt, x_sample, cache_a1, cache_a2, cache_a3, cache_b, norm_ffn1, ffn1_gate, ffn1_up, ffn1_down, norm_mix, w_in, sinks, w_branch_a, w_branch_b, w_out, norm_ffn2, ffn2_gate, ffn2_up, ffn2_down, norm_final):
    n_batch, seq, d = x_prompt.shape
    dec_batch, t_new, _ = x_sample.shape
    depth = w_in.shape[0]
    n_sample = dec_batch * t_new
    tm_s = min(TM, n_sample)
    hosts_per_layer = 2
    n_hosted = dec_batch // hosts_per_layer
    tm_p = n_batch * seq // n_hosted
    assert seq == DIL_PAIRS[2][0] and LANES % t_new == 0 and n_sample % tm_s == 0
    assert tm_s % t_new == 0 and dec_batch % (LANES // t_new) == 0
    assert dec_batch % hosts_per_layer == 0 and (n_batch * seq) % n_hosted == 0 and tm_p % 8 == 0

    perm = jnp.array(B_HEAD_PERM)
    col_b = 3 * N_GROUPS_A * A_OUT

    cos_p, sin_p = _rope_tables(jnp.arange(seq))
    cos_s, sin_s = _rope_tables(PAST_LEN + jnp.arange(t_new))
    cos_s, sin_s = jnp.tile(cos_s, (tm_s // t_new, 1)), jnp.tile(sin_s, (tm_s // t_new, 1))
    nat_p, nat_s = _nat_tables(cos_p, sin_p), _nat_tables(cos_s, sin_s)
    t_p, t_s = (cos_p.T, sin_p.T), (cos_s.T, sin_s.T)

    caches = [_cache_view(c) for c in (cache_a1, cache_a2, cache_a3, cache_b)]

    bf = lambda w: w.astype(_BF16)
    w_in_bf = bf(w_in)
    cols = lambda lo, hi: w_in_bf[:, :, lo:hi]
    bq = cols(col_b, col_b + B_Q).reshape(depth, d, N_HEADS_B, HEAD_DIM)[:, :, perm]
    bq = bq.reshape(depth, d, B_Q)
    q_cols = [cols(3 * g * A_OUT, (3 * g + 1) * A_OUT) for g in range(N_GROUPS_A)]
    kv_cols = [cols((3 * g + 1) * A_OUT, (3 * g + 3) * A_OUT) for g in range(N_GROUPS_A)]
    kv_cols.append(cols(col_b + B_Q, QKV_COLS))
    w_qkv = jnp.concatenate([cols(0, col_b), bq, cols(col_b + B_Q, QKV_COLS)], axis=2)
    w_kvt = jnp.swapaxes(jnp.concatenate(kv_cols + [q_cols[g] for g in VPU_GROUPS], axis=2), 1, 2)
    w_q = jnp.concatenate(q_cols + [bq], axis=2)
    w_bb = bf(w_branch_b).reshape(depth, N_HEADS_B, HEAD_DIM, d)[:, perm].reshape(depth, B_Q, d)
    w_ba, w_o = bf(w_branch_a), bf(w_out)
    sinks_perm = sinks[:, perm]
    row = lambda v: v.reshape(-1, 1, d)
    f1 = (row(norm_ffn1), bf(ffn1_gate), bf(ffn1_up), bf(ffn1_down))
    f2 = (row(norm_ffn2), bf(ffn2_gate), bf(ffn2_up), bf(ffn2_down))
    g_mix = row(norm_mix)
    gf = norm_final.reshape(1, d)

    xp = x_prompt.reshape(n_batch * seq, d)
    xs = x_sample.reshape(dec_batch * t_new, d)
    prompt_bufs = []
    sample_bufs = None
    for l in range(depth):
        last = l == depth - 1
        sinks_p = sinks_perm[l]

        xs = _ffn(xs, l, *f1, gf, final_norm=False, tm=tm_s)[0]
        q_nat, kvt_new = _mix_in_sample(xs, l, g_mix, w_q, w_kvt, nat_s, t_s, tm_s)
        host = lambda n_base: (n_base, t_new, sinks_p, q_nat, kvt_new, caches, sample_bufs)

        xp, oa_0, ob_0, *sample_bufs = _ffn(xp, l, *f1, gf, final_norm=False, tm=tm_p,
                                            host=host(0))
        outs = _mix_in_prompt(xp, l, g_mix, w_qkv, w_kvt, nat_p, t_p, n_batch, seq, prompt_bufs)
        prompt_bufs = list(outs[12:])
        oa_p, ob_p = _attn_prompt(sinks_p, outs[:12], n_batch, seq)
        xp = _mix_out(xp, l, g_mix, oa_p, ob_p, w_in_bf, w_ba, w_bb, w_o, TM)
        xp, oa_1, ob_1, *sample_bufs = _ffn(xp, l, *f2, gf, final_norm=last, tm=tm_p,
                                            host=host(n_hosted))

        oa_s = jnp.concatenate([oa_0, oa_1], axis=0)
        ob_s = jnp.concatenate([ob_0, ob_1], axis=0)
        xs = _mix_out(xs, l, g_mix, oa_s, ob_s, w_in_bf, w_ba, w_bb, w_o, tm_s)
        xs = _ffn(xs, l, *f2, gf, final_norm=last, tm=tm_s)[0]

    heads = (HEADS_A, HEADS_A, HEADS_A, N_KV_B)
    new_p = [_cache_unview(prompt_bufs[i], heads[i]) for i in range(4)]
    new_s = [_cache_unview(sample_bufs[i], heads[i]) for i in range(4)]
    return (xp.reshape(n_batch, seq, d), xs.reshape(dec_batch, t_new, d), *new_p, *new_s)
```

```python
import functools

import jax
import jax.numpy as jnp
from jax import lax
from jax.experimental import pallas as pl
from jax.experimental.pallas import tpu as pltpu

PAST_LEN = 16384
HEAD_DIM = 64
ROT_DIM = HEAD_DIM // 4
ROT_HALF = ROT_DIM // 2
ROPE_THETA = 500000.0
DIL_PAIRS = ((128, 1), (512, 4), (2048, 16))
N_GROUPS_A = 3
HEADS_A = 4
A_OUT = HEADS_A * HEAD_DIM
N_HEADS_B = 8
N_KV_B = 2
WINDOW_B = 128
B_Q = N_HEADS_B * HEAD_DIM
B_KV = N_KV_B * HEAD_DIM
QKV_COLS = N_GROUPS_A * 3 * A_OUT + B_Q + 2 * B_KV
KV_ROWS_A = 2 * A_OUT
KV_ROWS_B = 2 * B_KV
KVT_ROWS = N_GROUPS_A * KV_ROWS_A + KV_ROWS_B
VPU_GROUPS = (2,)
MXU_GROUPS = tuple(g for g in range(N_GROUPS_A) if g not in VPU_GROUPS)
KVT_ROWS_Q = KVT_ROWS + len(VPU_GROUPS) * A_OUT
NORM_EPS = 1e-6
NEG = -1e30
SCALE = HEAD_DIM ** -0.5
LOG2_E = 1.4426950408889634
BLK = 128
LANES = 128
MXU_TILE = 256
TM = 512
FFN_CHUNKS = 1
B_HEAD_PERM = (0, 4, 1, 5, 2, 6, 3, 7)
V7X_SCOPED_VMEM_BYTES = 60000 * 1024

_BF16 = jnp.bfloat16
_F32 = jnp.float32


def _cparams(n_axes):
    return pltpu.CompilerParams(
        dimension_semantics=("arbitrary",) * n_axes,
        vmem_limit_bytes=V7X_SCOPED_VMEM_BYTES,
    )


def _resident(shape):
    nd = len(shape)
    return pl.BlockSpec(shape, lambda *_: (0,) * nd, pipeline_mode=pl.Buffered(1))


def _layer_block(arr, layer, block=None, index=None):
    block = tuple(arr.shape[1:]) if block is None else tuple(block)
    index = (0,) * len(block) if index is None else tuple(index)
    return pl.BlockSpec((None,) + block, lambda *_: (layer,) + index,
                        pipeline_mode=pl.Buffered(1))


def _rms(x, g):
    return x * lax.rsqrt(jnp.mean(x * x, axis=-1, keepdims=True) + NORM_EPS) * g


def _sigmoid(x):
    return 1.0 / (1.0 + jnp.exp(-x))


def _dot(a, b):
    return jnp.dot(a, b, preferred_element_type=_F32)


def _dot_nt(a, b):
    return lax.dot_general(a, b, (((1,), (1,)), ((), ())), preferred_element_type=_F32)


def _ffn_kernel(x_ref, g_ref, wg_ref, wu_ref, wd_ref, gf_ref, *rest, bounds, final_norm, host):
    o_ref = rest[-1] if host is None else rest[-7]
    if host is not None:
        n_base, t_new = host
        _attn_sample_body(rest[0], rest[1], rest[2], rest[3:7], rest[-6], rest[-5], rest[-4:],
                          n_base + pl.program_id(0), t_new)
    x = x_ref[...]
    h = _rms(x, g_ref[...]).astype(_BF16)
    acc = None
    for lo, hi in zip(bounds[:-1], bounds[1:]):
        gate = _dot(h, wg_ref[:, lo:hi])
        up = _dot(h, wu_ref[:, lo:hi])
        act = (gate * _sigmoid(gate) * up).astype(_BF16)
        part = _dot(act, wd_ref[lo:hi, :])
        acc = part if acc is None else acc + part
    y = x + 0.5 * acc
    if final_norm:
        y = _rms(y, gf_ref[...])
    o_ref[...] = y


def _ffn(x, layer, g, wg, wu, wd, gf, *, final_norm, tm, host=None):
    n_tok, d = x.shape
    f = wg.shape[-1]
    steps = n_tok // tm
    n_tiles = f // MXU_TILE
    cuts = [0, (n_tiles + 1) // 2 * MXU_TILE, f] if FFN_CHUNKS == 2 and n_tiles > 1 else [0, f]
    tok_spec = pl.BlockSpec((tm, d), lambda i: (i, 0))
    in_specs = ([tok_spec] + [_layer_block(a, layer) for a in (g, wg, wu, wd)]
                + [_resident((1, d))])
    out_shape = [jax.ShapeDtypeStruct((n_tok, d), _F32)]
    out_specs = [tok_spec]
    operands = [x, g, wg, wu, wd, gf]
    aliases = {}
    host_static = None
    if host is not None:
        n_base, t_new, sinks_p, q_nat, kvt_new, caches, prev_out = host
        per_tile = LANES // t_new
        host_static = (n_base, t_new)
        cache_spec = lambda c: pl.BlockSpec((None, None) + c.shape[2:],
                                            lambda i: (layer, n_base + i, 0, 0))
        prev_out = [] if prev_out is None else list(prev_out)
        in_specs += ([pl.BlockSpec(memory_space=pltpu.SMEM),
                      pl.BlockSpec((t_new, q_nat.shape[1]), lambda i: (n_base + i, 0)),
                      pl.BlockSpec((KVT_ROWS_Q, LANES), lambda i: (0, (n_base + i) // per_tile))]
                     + [cache_spec(c) for c in caches]
                     + [pl.BlockSpec(memory_space=pl.ANY)] * len(prev_out))
        aliases = {len(operands) + 3 + len(caches) + k: 3 + k for k in range(len(prev_out))}
        operands += [sinks_p, q_nat, kvt_new, *caches, *prev_out]
        out_shape += ([jax.ShapeDtypeStruct((steps * t_new, A_OUT), _F32),
                       jax.ShapeDtypeStruct((steps * t_new, B_Q), _F32)]
                      + [jax.ShapeDtypeStruct(c.shape, c.dtype) for c in caches])
        out_specs += ([pl.BlockSpec((t_new, A_OUT), lambda i: (i, 0)),
                       pl.BlockSpec((t_new, B_Q), lambda i: (i, 0))]
                      + [cache_spec(c) for c in caches])
    kern = functools.partial(_ffn_kernel, bounds=tuple(cuts), final_norm=final_norm,
                             host=host_static)
    name = ("ffn_final" if final_norm else "ffn") + ("" if host is None else "_attn_sample")
    return pl.pallas_call(
        kern,
        out_shape=out_shape,
        grid=(steps,),
        in_specs=in_specs,
        out_specs=out_specs,
        input_output_aliases=aliases,
        compiler_params=_cparams(1),
        name=name,
    )(*operands)


def _rope_nat(x, c, s_lo, s_hi):
    return x * c + pltpu.roll(x, ROT_HALF, 1) * s_hi + pltpu.roll(x, LANES - ROT_HALF, 1) * s_lo


def _rope_t(k_t, cos_t, sin_t, n_heads):
    parts = []
    for hh in range(n_heads):
        b = hh * HEAD_DIM
        x1 = k_t[b:b + ROT_HALF]
        x2 = k_t[b + ROT_HALF:b + ROT_DIM]
        parts += [x1 * cos_t - x2 * sin_t, x2 * cos_t + x1 * sin_t, k_t[b + ROT_DIM:b + HEAD_DIM]]
    return jnp.concatenate(parts, axis=0)


def _rope_tables(pos):
    inv = ROPE_THETA ** (-(jnp.arange(ROT_HALF, dtype=_F32) * 2.0 / ROT_DIM))
    ang = pos.astype(_F32)[:, None] * inv[None, :]
    return jnp.cos(ang), jnp.sin(ang)


def _nat_tables(cos, sin):
    p = cos.shape[0]
    one = jnp.ones((p, HEAD_DIM - ROT_DIM), _F32)
    zero = jnp.zeros((p, HEAD_DIM - ROT_DIM), _F32)
    zh = jnp.zeros((p, ROT_HALF), _F32)
    c = jnp.concatenate([cos, cos, one], axis=1)
    s_hi = jnp.concatenate([zh, sin, zero], axis=1)
    s_lo = jnp.concatenate([-sin, zh, zero], axis=1)
    rep = LANES // HEAD_DIM
    return tuple(jnp.tile(t, (1, rep)) for t in (c, s_lo, s_hi))


def _mix_in_prompt_kernel(x_ref, g_ref, w_ref, wt_ref, c_ref, slo_ref, shi_ref, ct_ref, st_ref,
                          *rest, n_prev):
    prev = rest[:4] if n_prev else None
    (q1_ref, k1_ref, v1_ref, q2_ref, k2_ref, v2_ref, q3_ref, k3_ref, v3_ref,
     qb_ref, kb_ref, vb_ref, c1_ref, c2_ref, c3_ref, cb_ref, zs_ref) = rest[-17:]
    j = pl.program_id(1)
    tm = x_ref.shape[0]
    h = _rms(x_ref[...], g_ref[...]).astype(_BF16)
    c, s_lo, s_hi = c_ref[...], slo_ref[...], shi_ref[...]

    def proj(col0, width, kind):
        z = _dot(h, w_ref[:, col0:col0 + width])
        if kind == "v":
            return z
        slabs = []
        for s in range(width // LANES):
            zz = _rope_nat(z[:, s * LANES:(s + 1) * LANES], c, s_lo, s_hi)
            slabs.append(zz * (SCALE * LOG2_E) if kind == "q" else zz)
        return jnp.concatenate(slabs, axis=1)

    for i, (ref, kind) in enumerate(((q1_ref, "q"), (k1_ref, "k"), (v1_ref, "v"))):
        ref[...] = proj(i * A_OUT, A_OUT, kind).astype(_BF16)
    for g, refs in ((1, (q2_ref, k2_ref, v2_ref)), (2, (q3_ref, k3_ref, v3_ref))):
        dil = DIL_PAIRS[g][1]
        rows = tm // dil
        for i, (ref, kind) in enumerate(zip(refs, ("q", "k", "v"))):
            z = proj((3 * g + i) * A_OUT, A_OUT, kind)
            for s in range(A_OUT // LANES):
                zs_ref[s] = z[:, s * LANES:(s + 1) * LANES]
            for r in range(dil):
                for s in range(A_OUT // LANES):
                    ref[r, :, s * LANES:(s + 1) * LANES] = (
                        zs_ref[s, pl.ds(r, rows, stride=dil), :].astype(_BF16))
    col_b = 3 * N_GROUPS_A * A_OUT
    qb_ref[...] = proj(col_b, B_Q, "q").astype(_BF16)
    z = _dot(h, w_ref[:, col_b + B_Q:col_b + B_Q + 2 * B_KV])
    kb_ref[...] = _rope_nat(z[:, :B_KV], c, s_lo, s_hi).astype(_BF16)
    vb_ref[...] = z[:, B_KV:].astype(_BF16)

    def kv_t(row0, n_rows, h_tok, cos_t, sin_t, n_heads):
        z = _dot_nt(wt_ref[row0:row0 + n_rows, :], h_tok)
        half = n_rows // 2
        return jnp.concatenate([_rope_t(z[:half], cos_t, sin_t, n_heads), z[half:]], axis=0)

    def put(i, out_ref, new):
        if n_prev:
            out_ref[:n_prev] = prev[i][...]
        out_ref[n_prev] = new

    ct, st = ct_ref[...], st_ref[...]
    put(2, c3_ref, kv_t(2 * KV_ROWS_A, KV_ROWS_A, h, ct, st, HEADS_A))

    @pl.when(j == pl.num_programs(1) - 1)
    def _():
        put(1, c2_ref, kv_t(KV_ROWS_A, KV_ROWS_A, h, ct, st, HEADS_A))
        tail = tm - BLK
        put(0, c1_ref, kv_t(0, KV_ROWS_A, h[tail:], ct[:, tail:], st[:, tail:], HEADS_A))
        put(3, cb_ref, kv_t(3 * KV_ROWS_A, KV_ROWS_B, h[tail:], ct[:, tail:], st[:, tail:], N_KV_B))


def _mix_in_prompt(x, layer, g, w_qkv, w_kvt, nat_tabs, t_tabs, n_batch, seq, prev_bufs):
    d = x.shape[1]
    n_j = seq // TM
    n_prev = prev_bufs[0].shape[0] if prev_bufs else 0
    row = lambda n, j: (n * n_j + j, 0)
    tok_spec = lambda w: pl.BlockSpec((TM, w), row)
    cls_spec = lambda dil: pl.BlockSpec((None, dil, TM // dil, A_OUT), lambda n, j: (n, 0, j, 0))
    nat = lambda w: jax.ShapeDtypeStruct((n_batch * seq, w), _BF16)
    cls = lambda dil: jax.ShapeDtypeStruct((n_batch, dil, seq // dil, A_OUT), _BF16)
    keep = [min(w, seq) for w, _ in DIL_PAIRS]
    assert keep == [BLK, TM, seq] and seq % TM == 0
    bufs = ((KV_ROWS_A, keep[0], BLK, lambda j: 0), (KV_ROWS_A, keep[1], TM, lambda j: 0),
            (KV_ROWS_A, keep[2], TM, lambda j: j), (KV_ROWS_B, min(WINDOW_B, seq), BLK, lambda j: 0))

    def buf_spec(layers, rows, width, jmap):
        return pl.BlockSpec((layers, None, rows, width), lambda n, j: (0, n, 0, jmap(j)))

    def buf_shape(layers, rows, kept):
        return jax.ShapeDtypeStruct((layers, n_batch, rows, kept), _F32)

    n_out = n_prev + 1
    out_shape = ([nat(A_OUT)] * 3 + [cls(4)] * 3 + [cls(16)] * 3 + [nat(B_Q), nat(B_KV), nat(B_KV)]
                 + [buf_shape(n_out, r, k) for r, k, _, _ in bufs])
    out_specs = ([tok_spec(A_OUT)] * 3 + [cls_spec(4)] * 3 + [cls_spec(16)] * 3
                 + [tok_spec(B_Q), tok_spec(B_KV), tok_spec(B_KV)]
                 + [buf_spec(n_out, r, w, jm) for r, _, w, jm in bufs])
    prev_specs = [buf_spec(n_prev, r, w, jm) for r, _, w, jm in bufs] if n_prev else []
    tab_spec = pl.BlockSpec((TM, LANES), lambda n, j: (j, 0))
    ttab_spec = pl.BlockSpec((ROT_HALF, TM), lambda n, j: (0, j))
    return pl.pallas_call(
        functools.partial(_mix_in_prompt_kernel, n_prev=n_prev),
        out_shape=out_shape,
        grid=(n_batch, n_j),
        in_specs=[tok_spec(d)] + [_layer_block(a, layer) for a in (g, w_qkv, w_kvt)]
        + [tab_spec, tab_spec, tab_spec, ttab_spec, ttab_spec] + prev_specs,
        out_specs=out_specs,
        scratch_shapes=[pltpu.VMEM((A_OUT // LANES, TM, LANES), _F32)],
        compiler_params=_cparams(2),
        name="mix_in_prompt",
    )(x, g, w_qkv, w_kvt, *nat_tabs, *t_tabs, *prev_bufs)


def _mix_in_sample_kernel(x_ref, g_ref, w_ref, wt_ref, c_ref, slo_ref, shi_ref, ct_ref, st_ref,
                          q_ref, kvt_ref):
    h = _rms(x_ref[...], g_ref[...]).astype(_BF16)
    c, s_lo, s_hi = c_ref[...], slo_ref[...], shi_ref[...]
    z = _dot(h, w_ref[...])
    for s in range(z.shape[1] // LANES):
        sl = slice(s * LANES, (s + 1) * LANES)
        q_ref[:, sl] = _rope_nat(z[:, sl], c, s_lo, s_hi) * SCALE
    ct, st = ct_ref[...], st_ref[...]
    for g in range(N_GROUPS_A):
        r0 = g * KV_ROWS_A
        zt = _dot_nt(wt_ref[r0:r0 + KV_ROWS_A, :], h)
        kvt_ref[r0:r0 + A_OUT, :] = _rope_t(zt[:A_OUT], ct, st, HEADS_A)
        kvt_ref[r0 + A_OUT:r0 + KV_ROWS_A, :] = zt[A_OUT:]
    r0 = N_GROUPS_A * KV_ROWS_A
    zt = _dot_nt(wt_ref[r0:r0 + KV_ROWS_B, :], h)
    kvt_ref[r0:r0 + B_KV, :] = _rope_t(zt[:B_KV], ct, st, N_KV_B)
    kvt_ref[r0 + B_KV:r0 + KV_ROWS_B, :] = zt[B_KV:]
    for i in range(len(VPU_GROUPS)):
        r0 = KVT_ROWS + i * A_OUT
        zt = _dot_nt(wt_ref[r0:r0 + A_OUT, :], h)
        kvt_ref[r0:r0 + A_OUT, :] = _rope_t(zt, ct, st, HEADS_A) * SCALE


def _mix_in_sample(x, layer, g, w_q, w_kvt, nat_tabs, t_tabs, tm):
    n_tok, d = x.shape
    qw = w_q.shape[-1]
    tab_spec = pl.BlockSpec((tm, LANES), lambda i: (0, 0))
    ttab_spec = pl.BlockSpec((ROT_HALF, tm), lambda i: (0, 0))
    assert w_kvt.shape[1] == KVT_ROWS_Q
    return pl.pallas_call(
        _mix_in_sample_kernel,
        out_shape=[jax.ShapeDtypeStruct((n_tok, qw), _F32),
                   jax.ShapeDtypeStruct((KVT_ROWS_Q, n_tok), _F32)],
        grid=(n_tok // tm,),
        in_specs=[pl.BlockSpec((tm, d), lambda i: (i, 0))]
        + [_layer_block(a, layer) for a in (g, w_q, w_kvt)]
        + [tab_spec, tab_spec, tab_spec, ttab_spec, ttab_spec],
        out_specs=[pl.BlockSpec((tm, qw), lambda i: (i, 0)),
                   pl.BlockSpec((KVT_ROWS_Q, tm), lambda i: (0, i))],
        compiler_params=_cparams(1),
        name="mix_in_sample",
    )(x, g, w_q, w_kvt, *nat_tabs, *t_tabs)


def _softmax_rows(s, valid):
    s = jnp.where(valid, s, NEG)
    m = jnp.max(s, axis=-1, keepdims=True)
    e = jnp.exp(s - m)
    return e, jnp.sum(e, axis=-1, keepdims=True), m


def _lane_head(t, width):
    return lax.shift_right_logical(lax.broadcasted_iota(jnp.int32, (t, width), 1),
                                   HEAD_DIM.bit_length() - 1)


def _head_stack_a(q):
    lane_head = _lane_head(q.shape[0], A_OUT)
    return jnp.concatenate([jnp.where(lane_head == hh, q, 0.0) for hh in range(HEADS_A)], axis=0)


def _head_unstack_a(o_st, col_st, t):
    lane_head = _lane_head(t, A_OUT)
    out = jnp.zeros((t, A_OUT), _F32)
    for hh in range(HEADS_A):
        col = col_st[hh * t:(hh + 1) * t]
        blk = jnp.broadcast_to(col, (t, A_OUT)) if o_st is None else o_st[hh * t:(hh + 1) * t] * col
        out = jnp.where(lane_head == hh, blk, out)
    return out


def _head_stack_b(q):
    t = q.shape[0]
    lo = lax.broadcasted_iota(jnp.int32, (t, LANES), 1) < HEAD_DIM
    parts = []
    for c in range(B_Q // LANES):
        qc = q[:, c * LANES:(c + 1) * LANES]
        parts += [jnp.where(lo, qc, 0.0), jnp.where(lo, 0.0, qc)]
    return jnp.concatenate(parts, axis=0)


def _head_unstack_b(o_st, col_st, t):
    lo = lax.broadcasted_iota(jnp.int32, (t, LANES), 1) < HEAD_DIM
    cols = []
    for c in range(B_Q // LANES):
        r0 = 2 * c * t
        a = o_st[r0:r0 + t] * col_st[r0:r0 + t]
        b = o_st[r0 + t:r0 + 2 * t] * col_st[r0 + t:r0 + 2 * t]
        cols.append(jnp.where(lo, a, b))
    return jnp.concatenate(cols, axis=1)


def _sink_rows(sink_ref, t):
    return jnp.concatenate(
        [jnp.full((t, 1), sink_ref[p], _F32) for p in range(N_HEADS_B)], axis=0)


def _band_valid(n_keys, n_back, has_prev):
    r = lax.broadcasted_iota(jnp.int32, (BLK, n_keys), 0)
    c = lax.broadcasted_iota(jnp.int32, (BLK, n_keys), 1)
    dist = (BLK + r - c) if has_prev else (r - c)
    return (dist >= 0) & (dist <= n_back)


def _attn_prompt_kernel(sinkl_ref, q1_ref, k1_ref, v1_ref, q2_ref, k2_ref, v2_ref,
                        q3_ref, k3_ref, v3_ref, qb_ref, kb_ref, vb_ref,
                        oa_ref, ob_ref, acc_s, den_s, m_s, bias_s):
    seq = q1_ref.shape[0]
    n_blk = seq // BLK
    dil2, dil3 = DIL_PAIRS[1][1], DIL_PAIRS[2][1]
    per_class2 = seq // dil2 // BLK
    assert dil2 * per_class2 == dil3 == n_blk and seq // dil3 == BLK

    lane_head_a = _lane_head(BLK, A_OUT)
    head_a_bf = lane_head_a.astype(_F32).astype(_BF16)
    lane_head_b = _lane_head(BLK, LANES)
    head_b_bf = lane_head_b.astype(_F32).astype(_BF16)
    r_i = lax.broadcasted_iota(jnp.int32, (BLK, 2 * BLK), 0)
    c_i = lax.broadcasted_iota(jnp.int32, (BLK, 2 * BLK), 1)
    dist = BLK + r_i - c_i
    for i, n_back in enumerate((BLK, WINDOW_B - 1)):
        band = (dist >= 0) & (dist <= n_back)
        bias_s[2 * i] = jnp.where(band & (c_i >= BLK), 0.0, NEG)
        bias_s[2 * i + 1] = jnp.where(band, 0.0, NEG)
    causal = jnp.where(lax.broadcasted_iota(jnp.int32, (BLK, BLK), 0)
                       >= lax.broadcasted_iota(jnp.int32, (BLK, BLK), 1), 0.0, NEG)

    def with_prev(i, has_prev):
        return bias_s[2 * i + jnp.where(has_prev, 1, 0)]

    def window(ref, lead, r0):
        p0 = pl.multiple_of(jnp.maximum(r0 - BLK, 0), BLK)
        return jnp.concatenate([ref[lead + (pl.ds(p0, BLK),)], ref[lead + (pl.ds(r0, BLK),)]],
                               axis=0)

    def softmax_block(qs, kw, vw, bias, n_heads):
        n_keys = kw.shape[0]
        s = _dot_nt(qs, kw).reshape(n_heads, BLK, n_keys) + bias[None]
        m = jnp.max(s, axis=-1, keepdims=True)
        e = jnp.exp2(s - m)
        den = jnp.sum(e, axis=-1, keepdims=True)
        o = _dot(e.reshape(n_heads * BLK, n_keys).astype(_BF16), vw)
        return o, den.reshape(n_heads * BLK, 1), m.reshape(n_heads * BLK, 1)

    def block_a(q, kw, vw, bias):
        zero = jnp.zeros_like(q)
        qs = jnp.concatenate([jnp.where(head_a_bf == hh, q, zero) for hh in range(HEADS_A)], axis=0)
        o, den, m = softmax_block(qs, kw, vw, bias, HEADS_A)
        acc = o[:BLK]
        den_u = jnp.broadcast_to(den[:BLK], (BLK, A_OUT))
        m_u = jnp.broadcast_to(m[:BLK], (BLK, A_OUT))
        for hh in range(1, HEADS_A):
            sel = lane_head_a == hh
            rows = slice(hh * BLK, (hh + 1) * BLK)
            acc = jnp.where(sel, o[rows], acc)
            den_u = jnp.where(sel, jnp.broadcast_to(den[rows], (BLK, A_OUT)), den_u)
            m_u = jnp.where(sel, jnp.broadcast_to(m[rows], (BLK, A_OUT)), m_u)
        return acc, den_u, m_u

    def store(g, start, stride, acc, den_u, m_u):
        idx = pl.ds(start, BLK, stride=stride)
        for s in range(A_OUT // LANES):
            sl = slice(s * LANES, (s + 1) * LANES)
            acc_s[g, s, idx, :] = acc[:, sl]
            den_s[g, s, idx, :] = den_u[:, sl]
            m_s[g, s, idx, :] = m_u[:, sl]

    def classes_body(i, carry):
        r = lax.div(i, per_class2)
        sb = i - r * per_class2
        s0 = pl.multiple_of(sb * BLK, BLK)
        res = block_a(q2_ref[r, pl.ds(s0, BLK)], window(k2_ref, (r,), s0), window(v2_ref, (r,), s0),
                      with_prev(0, sb > 0))
        store(0, r + dil2 * s0, dil2, *res)
        res = block_a(q3_ref[i], k3_ref[i], v3_ref[i], causal)
        store(1, i, dil3, *res)
        return carry

    lax.fori_loop(0, n_blk, classes_body, 0, unroll=True)

    lo = lane_head_b == 0
    sink_lanes = sinkl_ref[...]

    def tokens_body(b, carry):
        r0 = pl.multiple_of(b * BLK, BLK)
        acc1, den1, m1 = block_a(q1_ref[pl.ds(r0, BLK)], window(k1_ref, (), r0),
                                 window(v1_ref, (), r0), with_prev(0, b > 0))
        cols = []
        for s in range(A_OUT // LANES):
            sl = slice(s * LANES, (s + 1) * LANES)
            rows = pl.ds(r0, BLK)
            ms = [m1[:, sl], m_s[0, s, rows, :], m_s[1, s, rows, :]]
            dens = [den1[:, sl], den_s[0, s, rows, :], den_s[1, s, rows, :]]
            accs = [acc1[:, sl], acc_s[0, s, rows, :], acc_s[1, s, rows, :]]
            top = jnp.maximum(jnp.maximum(ms[0], ms[1]), ms[2])
            ws = [jnp.exp2(mm - top) for mm in ms]
            num = ws[0] * accs[0] + ws[1] * accs[1] + ws[2] * accs[2]
            dn = ws[0] * dens[0] + ws[1] * dens[1] + ws[2] * dens[2]
            cols.append(num / dn)
        oa_ref[pl.ds(r0, BLK), :] = jnp.concatenate(cols, axis=1).astype(oa_ref.dtype)

        q = qb_ref[pl.ds(r0, BLK)]
        zero = jnp.zeros((BLK, LANES), q.dtype)
        parts = []
        for c in range(B_Q // LANES):
            qc = q[:, c * LANES:(c + 1) * LANES]
            parts += [jnp.where(head_b_bf == 0, qc, zero), jnp.where(head_b_bf == 1, qc, zero)]
        o, den, m = softmax_block(jnp.concatenate(parts, axis=0), window(kb_ref, (), r0),
                                  window(vb_ref, (), r0), with_prev(1, b > 0), N_HEADS_B)
        cols = []
        for c in range(B_Q // LANES):
            ra = slice(2 * c * BLK, (2 * c + 1) * BLK)
            rb = slice((2 * c + 1) * BLK, (2 * c + 2) * BLK)
            o_c = jnp.where(lo, o[ra], o[rb])
            den_c = jnp.where(lo, jnp.broadcast_to(den[ra], (BLK, LANES)),
                              jnp.broadcast_to(den[rb], (BLK, LANES)))
            m_c = jnp.where(lo, jnp.broadcast_to(m[ra], (BLK, LANES)),
                            jnp.broadcast_to(m[rb], (BLK, LANES)))
            sink_c = sink_lanes[:, c * LANES:(c + 1) * LANES]
            cols.append(o_c / (den_c + jnp.exp2(sink_c - m_c)))
        ob_ref[pl.ds(r0, BLK), :] = jnp.concatenate(cols, axis=1).astype(ob_ref.dtype)
        return carry

    lax.fori_loop(0, n_blk, tokens_body, 0, unroll=True)


def _attn_prompt(sinks_p, qkv, n_batch, seq):
    (q1, k1, v1, q2, k2, v2, q3, k3, v3, qb, kb, vb) = qkv
    tok = lambda w: pl.BlockSpec((seq, w), lambda n: (n, 0))
    cls = lambda dil: pl.BlockSpec((None, dil, seq // dil, A_OUT), lambda n: (n, 0, 0, 0))
    return pl.pallas_call(
        _attn_prompt_kernel,
        out_shape=[jax.ShapeDtypeStruct((n_batch * seq, A_OUT), _BF16),
                   jax.ShapeDtypeStruct((n_batch * seq, B_Q), _BF16)],
        grid=(n_batch,),
        in_specs=[_resident((1, B_Q))]
        + [tok(A_OUT)] * 3 + [cls(4)] * 3 + [cls(16)] * 3 + [tok(B_Q), tok(B_KV), tok(B_KV)],
        out_specs=[tok(A_OUT), tok(B_Q)],
        scratch_shapes=[pltpu.VMEM((N_GROUPS_A - 1, A_OUT // LANES, seq, LANES), _F32)] * 3
        + [pltpu.VMEM((4, BLK, 2 * BLK), _F32)],
        compiler_params=_cparams(1),
        name="attn_prompt",
    )(jnp.repeat(sinks_p * LOG2_E, HEAD_DIM)[None, :], q1, k1, v1, q2, k2, v2, q3, k3, v3,
      qb, kb, vb)


def _attn_sample_body(sink_ref, q_ref, kvt_ref, cache_refs, oa_ref, ob_ref, new_refs, n, t_new):
    c1_ref, c2_ref, c3_ref, cb_ref = cache_refs
    n1_ref, n2_ref, n3_ref, nb_ref = new_refs
    per_tile = LANES // t_new
    first_new = LANES - t_new
    shift = first_new - (n & (per_tile - 1)) * t_new
    new_t = pltpu.roll(kvt_ref[...], shift, 1)
    q = q_ref[...]

    def new_cache(old_ref, new_rows, out_ref):
        lb = old_ref.shape[1]
        rolled = pltpu.roll(old_ref[...], lb - t_new, 1)
        if lb > LANES:
            out_ref[:, :lb - LANES] = rolled[:, :lb - LANES]
        lane = lax.broadcasted_iota(jnp.int32, (old_ref.shape[0], LANES), 1)
        out_ref[:, lb - LANES:] = jnp.where(lane >= first_new, new_rows, rolled[:, lb - LANES:])

    def masks(rows, lb, window, dil, strict):
        t_old = lax.broadcasted_iota(jnp.int32, (rows, lb), 0) & (t_new - 1)
        dist = lb + t_old - lax.broadcasted_iota(jnp.int32, (rows, lb), 1)
        hi = (dist < window) if strict else (dist <= window)
        valid_old = (dist >= 0) & hi & ((dist & (dil - 1)) == 0)
        t_nw = lax.broadcasted_iota(jnp.int32, (rows, LANES), 0) & (t_new - 1)
        c_nw = lax.broadcasted_iota(jnp.int32, (rows, LANES), 1) - first_new
        dn = t_nw - c_nw
        hin = (dn < window) if strict else (dn <= window)
        valid_new = (c_nw >= 0) & (dn >= 0) & hin & ((dn & (dil - 1)) == 0)
        return valid_old, valid_new

    def attend(qs, k_old, v_old, k_new, v_new, valid_old, valid_new):
        s_old = jnp.where(valid_old, _dot(qs, k_old), NEG)
        s_new = jnp.where(valid_new, _dot(qs, k_new), NEG)
        m = jnp.maximum(jnp.max(s_old, axis=-1, keepdims=True),
                        jnp.max(s_new, axis=-1, keepdims=True))
        e_old = jnp.exp(s_old - m)
        e_new = jnp.exp(s_new - m)
        den = jnp.sum(e_old, axis=-1, keepdims=True) + jnp.sum(e_new, axis=-1, keepdims=True)
        o = _dot_nt(e_old.astype(_BF16), v_old) + _dot_nt(e_new.astype(_BF16), v_new)
        return o, den, m

    def classes_reduce(x, dil, op):
        sh = dil
        while sh < LANES:
            x = op(x, pltpu.roll(x, sh, 1))
            sh *= 2
        return x

    def fold_tiles(x, op):
        acc = x[:, :LANES]
        for jt in range(1, x.shape[1] // LANES):
            acc = op(acc, x[:, jt * LANES:(jt + 1) * LANES])
        return acc

    def attend_lane_classes(old_ref, new_g, q_t, window, dil):
        lb = old_ref.shape[1]
        n_tiles = lb // LANES
        assert t_new <= dil and lb % dil == 0 and LANES % dil == 0 and lb <= window
        lane_q = lax.broadcasted_iota(jnp.int32, (A_OUT, LANES), 1)
        q_new = jnp.where(lane_q >= first_new, q_t, 0.0)
        q_cls = classes_reduce(pltpu.roll(q_new, t_new, 1), dil, jnp.add)
        lane_1 = lax.broadcasted_iota(jnp.int32, (1, LANES), 1)
        cls_ok = jnp.concatenate([(lane_1 & (dil - 1)) < t_new] * n_tiles, axis=1)
        o_rows, lse_rows = [], []
        for hh in range(HEADS_A):
            rows = slice(hh * HEAD_DIM, (hh + 1) * HEAD_DIM)
            q_h = jnp.concatenate([q_cls[rows]] * n_tiles, axis=1)
            s_old = jnp.sum(old_ref[rows, :] * q_h, axis=0, keepdims=True)
            s_old = jnp.where(cls_ok, s_old, NEG)
            s_new = jnp.sum(new_g[rows] * q_new[rows], axis=0, keepdims=True)
            s_new = jnp.where(lane_1 < t_new, pltpu.roll(s_new, t_new, 1), NEG)
            m = classes_reduce(jnp.maximum(fold_tiles(s_old, jnp.maximum), s_new), dil, jnp.maximum)
            e_old = jnp.where(cls_ok, jnp.exp(s_old - jnp.concatenate([m] * n_tiles, axis=1)), 0.0)
            e_new = jnp.where(lane_1 < t_new, jnp.exp(s_new - m), 0.0)
            den = classes_reduce(fold_tiles(e_old, jnp.add) + e_new, dil, jnp.add)
            den = jnp.where((lane_1 & (dil - 1)) < t_new, den, 1.0)
            v_rows = slice(A_OUT + hh * HEAD_DIM, A_OUT + (hh + 1) * HEAD_DIM)
            acc = fold_tiles(old_ref[v_rows, :] * e_old, jnp.add)
            acc = acc + pltpu.roll(new_g[v_rows], t_new, 1) * e_new
            acc = classes_reduce(acc, dil, jnp.add)
            o_rows.append(acc / den)
            lse_rows.append(jnp.broadcast_to(m + jnp.log(den), (HEAD_DIM, LANES)))
        out_t = jnp.concatenate(o_rows, axis=0)
        lse_t = jnp.concatenate(lse_rows, axis=0)
        return out_t.T[:t_new], lse_t.T[:t_new]

    outs, lses = [], []
    for g, (old_ref, out_ref) in enumerate(((c1_ref, n1_ref), (c2_ref, n2_ref), (c3_ref, n3_ref))):
        window, dil = DIL_PAIRS[g]
        lb = old_ref.shape[1]
        new_g = new_t[g * KV_ROWS_A:(g + 1) * KV_ROWS_A]
        if g in VPU_GROUPS:
            q_row0 = KVT_ROWS + VPU_GROUPS.index(g) * A_OUT
            out, lse = attend_lane_classes(old_ref, new_g, new_t[q_row0:q_row0 + A_OUT],
                                           window, dil)
            outs.append(out)
            lses.append(lse)
            new_cache(old_ref, new_g, out_ref)
            continue
        q0 = MXU_GROUPS.index(g) * A_OUT
        qs = _head_stack_a(q[:, q0:q0 + A_OUT]).astype(_BF16)
        valid_old, valid_new = masks(HEADS_A * t_new, lb, window, dil, False)
        o, den, m = attend(qs, old_ref[0:A_OUT, :].astype(_BF16), old_ref[A_OUT:, :].astype(_BF16),
                           new_g[:A_OUT].astype(_BF16), new_g[A_OUT:].astype(_BF16),
                           valid_old, valid_new)
        outs.append(_head_unstack_a(o, 1.0 / den, t_new))
        lses.append(_head_unstack_a(None, m + jnp.log(den), t_new))
        new_cache(old_ref, new_g, out_ref)
    m = jnp.maximum(jnp.maximum(lses[0], lses[1]), lses[2])
    ws = [jnp.exp(l - m) for l in lses]
    oa_ref[...] = (ws[0] * outs[0] + ws[1] * outs[1] + ws[2] * outs[2]) / (ws[0] + ws[1] + ws[2])

    lb = cb_ref.shape[1]
    new_b = new_t[N_GROUPS_A * KV_ROWS_A:KVT_ROWS]
    qs = _head_stack_b(q[:, len(MXU_GROUPS) * A_OUT:]).astype(_BF16)
    valid_old, valid_new = masks(N_HEADS_B * t_new, lb, WINDOW_B, 1, True)
    o, den, m = attend(qs, cb_ref[0:B_KV, :].astype(_BF16), cb_ref[B_KV:, :].astype(_BF16),
                       new_b[:B_KV].astype(_BF16), new_b[B_KV:].astype(_BF16),
                       valid_old, valid_new)
    scale = _sigmoid(m + jnp.log(den) - _sink_rows(sink_ref, t_new)) / den
    ob_ref[...] = _head_unstack_b(o, scale, t_new)
    new_cache(cb_ref, new_b, nb_ref)


def _mix_out_kernel(x_ref, g_ref, oa_ref, ob_ref, wga_ref, wgb_ref, wba_ref, wbb_ref, wout_ref,
                    o_ref):
    x = x_ref[...]
    h = _rms(x, g_ref[...]).astype(_BF16)
    gate_a = _dot(h, wga_ref[...])
    gate_b = _dot(h, wgb_ref[...])
    ya = _dot(oa_ref[...].astype(_BF16), wba_ref[...])
    yb = _dot(ob_ref[...].astype(_BF16), wbb_ref[...])
    merged = _sigmoid(gate_a) * ya + _sigmoid(gate_b) * yb
    o_ref[...] = x + _dot(merged.astype(_BF16), wout_ref[...])


def _mix_out(x, layer, g, oa, ob, w_in, w_ba, w_bb, w_out, tm):
    n_tok, d = x.shape
    assert QKV_COLS % d == 0 and w_in.shape[2] == QKV_COLS + 2 * d
    gate_specs = [_layer_block(w_in, layer, (d, d), (0, QKV_COLS // d + k)) for k in range(2)]
    tok = lambda w: pl.BlockSpec((tm, w), lambda i: (i, 0))
    return pl.pallas_call(
        _mix_out_kernel,
        out_shape=jax.ShapeDtypeStruct((n_tok, d), _F32),
        grid=(n_tok // tm,),
        in_specs=[tok(d), _layer_block(g, layer), tok(A_OUT), tok(B_Q)] + gate_specs
        + [_layer_block(a, layer) for a in (w_ba, w_bb, w_out)],
        out_specs=tok(d),
        compiler_params=_cparams(1),
        name="mix_out",
    )(x, g, oa, ob, w_in, w_in, w_ba, w_bb, w_out)


def _cache_view(c):
    dep, nb, rows = c.shape[:3]
    return jnp.transpose(c, (0, 1, 3, 4, 5, 2)).reshape(dep, nb, -1, rows)


def _cache_unview(c, heads):
    dep, nb, _, rows = c.shape
    return jnp.transpose(c.reshape(dep, nb, 2, heads, HEAD_DIM, rows), (0, 1, 5, 2, 3, 4))


def kernel(x_prompt, x_sample, cache_a1, cache_a2, cache_a3, cache_b, norm_ffn1, ffn1_gate, ffn1_up, ffn1_down, norm_mix, w_in, sinks, w_branch_a, w_branch_b, w_out, norm_ffn2, ffn2_gate, ffn2_up, ffn2_down, norm_final):
    n_batch, seq, d = x_prompt.shape
    dec_batch, t_new, _ = x_sample.shape
    depth = w_in.shape[0]
    n_sample = dec_batch * t_new
    tm_s = min(TM, n_sample)
    hosts_per_layer = 2
    n_hosted = dec_batch // hosts_per_layer
    tm_p = n_batch * seq // n_hosted
    assert seq == DIL_PAIRS[2][0] and LANES % t_new == 0 and n_sample % tm_s == 0
    assert tm_s % t_new == 0 and dec_batch % (LANES // t_new) == 0
    assert dec_batch % hosts_per_layer == 0 and (n_batch * seq) % n_hosted == 0 and tm_p % 8 == 0

    perm = jnp.array(B_HEAD_PERM)
    col_b = 3 * N_GROUPS_A * A_OUT

    cos_p, sin_p = _rope_tables(jnp.arange(seq))
    cos_s, sin_s = _rope_tables(PAST_LEN + jnp.arange(t_new))
    cos_s, sin_s = jnp.tile(cos_s, (tm_s // t_new, 1)), jnp.tile(sin_s, (tm_s // t_new, 1))
    nat_p, nat_s = _nat_tables(cos_p, sin_p), _nat_tables(cos_s, sin_s)
    t_p, t_s = (cos_p.T, sin_p.T), (cos_s.T, sin_s.T)

    caches = [_cache_view(c) for c in (cache_a1, cache_a2, cache_a3, cache_b)]

    bf = lambda w: w.astype(_BF16)
    w_in_bf = bf(w_in)
    cols = lambda lo, hi: w_in_bf[:, :, lo:hi]
    bq = cols(col_b, col_b + B_Q).reshape(depth, d, N_HEADS_B, HEAD_DIM)[:, :, perm]
    bq = bq.reshape(depth, d, B_Q)
    q_cols = [cols(3 * g * A_OUT, (3 * g + 1) * A_OUT) for g in range(N_GROUPS_A)]
    kv_cols = [cols((3 * g + 1) * A_OUT, (3 * g + 3) * A_OUT) for g in range(N_GROUPS_A)]
    kv_cols.append(cols(col_b + B_Q, QKV_COLS))
    w_qkv = jnp.concatenate([cols(0, col_b), bq, cols(col_b + B_Q, QKV_COLS)], axis=2)
    w_kvt = jnp.swapaxes(jnp.concatenate(kv_cols + [q_cols[g] for g in VPU_GROUPS], axis=2), 1, 2)
    w_q = jnp.concatenate([q_cols[g] for g in MXU_GROUPS] + [bq], axis=2)
    w_bb = bf(w_branch_b).reshape(depth, N_HEADS_B, HEAD_DIM, d)[:, perm].reshape(depth, B_Q, d)
    w_ba, w_o = bf(w_branch_a), bf(w_out)
    sinks_perm = sinks[:, perm]
    row = lambda v: v.reshape(-1, 1, d)
    f1 = (row(norm_ffn1), bf(ffn1_gate), bf(ffn1_up), bf(ffn1_down))
    f2 = (row(norm_ffn2), bf(ffn2_gate), bf(ffn2_up), bf(ffn2_down))
    g_mix = row(norm_mix)
    gf = norm_final.reshape(1, d)

    xp = x_prompt.reshape(n_batch * seq, d)
    xs = x_sample.reshape(dec_batch * t_new, d)
    prompt_bufs = []
    sample_bufs = None
    for l in range(depth):
        last = l == depth - 1
        sinks_p = sinks_perm[l]

        xs = _ffn(xs, l, *f1, gf, final_norm=False, tm=tm_s)[0]
        q_nat, kvt_new = _mix_in_sample(xs, l, g_mix, w_q, w_kvt, nat_s, t_s, tm_s)
        host = lambda n_base: (n_base, t_new, sinks_p, q_nat, kvt_new, caches, sample_bufs)

        xp, oa_0, ob_0, *sample_bufs = _ffn(xp, l, *f1, gf, final_norm=False, tm=tm_p,
                                            host=host(0))
        outs = _mix_in_prompt(xp, l, g_mix, w_qkv, w_kvt, nat_p, t_p, n_batch, seq, prompt_bufs)
        prompt_bufs = list(outs[12:])
        oa_p, ob_p = _attn_prompt(sinks_p, outs[:12], n_batch, seq)
        xp = _mix_out(xp, l, g_mix, oa_p, ob_p, w_in_bf, w_ba, w_bb, w_o, TM)
        xp, oa_1, ob_1, *sample_bufs = _ffn(xp, l, *f2, gf, final_norm=last, tm=tm_p,
                                            host=host(n_hosted))

        oa_s = jnp.concatenate([oa_0, oa_1], axis=0)
        ob_s = jnp.concatenate([ob_0, ob_1], axis=0)
        xs = _mix_out(xs, l, g_mix, oa_s, ob_s, w_in_bf, w_ba, w_bb, w_o, tm_s)
        xs = _ffn(xs, l, *f2, gf, final_norm=last, tm=tm_s)[0]

    heads = (HEADS_A, HEADS_A, HEADS_A, N_KV_B)
    new_p = [_cache_unview(prompt_bufs[i], heads[i]) for i in range(4)]
    new_s = [_cache_unview(sample_bufs[i], heads[i]) for i in range(4)]
    return (xp.reshape(n_batch, seq, d), xs.reshape(dec_batch, t_new, d), *new_p, *new_s)
```

```python
import functools

import jax
import jax.numpy as jnp
from jax import lax
from jax.experimental import pallas as pl
from jax.experimental.pallas import tpu as pltpu

PAST_LEN = 16384
HEAD_DIM = 64
ROT_DIM = HEAD_DIM // 4
ROT_HALF = ROT_DIM // 2
ROPE_THETA = 500000.0
DIL_PAIRS = ((128, 1), (512, 4), (2048, 16))
N_GROUPS_A = 3
HEADS_A = 4
A_OUT = HEADS_A * HEAD_DIM
N_HEADS_B = 8
N_KV_B = 2
WINDOW_B = 128
B_Q = N_HEADS_B * HEAD_DIM
B_KV = N_KV_B * HEAD_DIM
QKV_COLS = N_GROUPS_A * 3 * A_OUT + B_Q + 2 * B_KV
KV_ROWS_A = 2 * A_OUT
KV_ROWS_B = 2 * B_KV
KVT_ROWS = N_GROUPS_A * KV_ROWS_A + KV_ROWS_B
VPU_GROUPS = (2,)
MXU_GROUPS = tuple(g for g in range(N_GROUPS_A) if g not in VPU_GROUPS)
KVT_ROWS_Q = KVT_ROWS + len(VPU_GROUPS) * A_OUT
NORM_EPS = 1e-6
NEG = -1e30
SCALE = HEAD_DIM ** -0.5
LOG2_E = 1.4426950408889634
BLK = 128
LANES = 128
MXU_TILE = 256
TM = 512
FFN_CHUNKS = 1
B_HEAD_PERM = (0, 4, 1, 5, 2, 6, 3, 7)
V7X_SCOPED_VMEM_BYTES = 60000 * 1024

_BF16 = jnp.bfloat16
_F32 = jnp.float32


def _cparams(n_axes):
    return pltpu.CompilerParams(
        dimension_semantics=("arbitrary",) * n_axes,
        vmem_limit_bytes=V7X_SCOPED_VMEM_BYTES,
    )


def _resident(shape):
    nd = len(shape)
    return pl.BlockSpec(shape, lambda *_: (0,) * nd, pipeline_mode=pl.Buffered(1))


def _layer_block(arr, layer, block=None, index=None):
    block = tuple(arr.shape[1:]) if block is None else tuple(block)
    index = (0,) * len(block) if index is None else tuple(index)
    return pl.BlockSpec((None,) + block, lambda *_: (layer,) + index,
                        pipeline_mode=pl.Buffered(1))


def _rms(x, g):
    return x * lax.rsqrt(jnp.mean(x * x, axis=-1, keepdims=True) + NORM_EPS) * g


def _sigmoid(x):
    return 1.0 / (1.0 + jnp.exp(-x))


def _dot(a, b):
    return jnp.dot(a, b, preferred_element_type=_F32)


def _dot_nt(a, b):
    return lax.dot_general(a, b, (((1,), (1,)), ((), ())), preferred_element_type=_F32)


def _ffn_kernel(x_ref, g_ref, wg_ref, wu_ref, wd_ref, gf_ref, *rest, bounds, final_norm, host):
    o_ref = rest[-1] if host is None else rest[-7]
    if host is not None:
        n_base, t_new = host
        _attn_sample_body(rest[0], rest[1], rest[2], rest[3:7], rest[-6], rest[-5], rest[-4:],
                          n_base + pl.program_id(0), t_new)
    x = x_ref[...]
    h = _rms(x, g_ref[...]).astype(_BF16)
    acc = None
    for lo, hi in zip(bounds[:-1], bounds[1:]):
        gate = _dot(h, wg_ref[:, lo:hi])
        up = _dot(h, wu_ref[:, lo:hi])
        act = (gate * _sigmoid(gate) * up).astype(_BF16)
        part = _dot(act, wd_ref[lo:hi, :])
        acc = part if acc is None else acc + part
    y = x + 0.5 * acc
    if final_norm:
        y = _rms(y, gf_ref[...])
    o_ref[...] = y


def _ffn(x, layer, g, wg, wu, wd, gf, *, final_norm, tm, host=None):
    n_tok, d = x.shape
    f = wg.shape[-1]
    steps = n_tok // tm
    n_tiles = f // MXU_TILE
    cuts = [0, (n_tiles + 1) // 2 * MXU_TILE, f] if FFN_CHUNKS == 2 and n_tiles > 1 else [0, f]
    tok_spec = pl.BlockSpec((tm, d), lambda i: (i, 0))
    in_specs = ([tok_spec] + [_layer_block(a, layer) for a in (g, wg, wu, wd)]
                + [_resident((1, d))])
    out_shape = [jax.ShapeDtypeStruct((n_tok, d), _F32)]
    out_specs = [tok_spec]
    operands = [x, g, wg, wu, wd, gf]
    aliases = {}
    host_static = None
    if host is not None:
        n_base, t_new, sinks_p, q_nat, kvt_new, caches, prev_out = host
        per_tile = LANES // t_new
        host_static = (n_base, t_new)
        cache_spec = lambda c: pl.BlockSpec((None, None) + c.shape[2:],
                                            lambda i: (layer, n_base + i, 0, 0))
        prev_out = [] if prev_out is None else list(prev_out)
        in_specs += ([pl.BlockSpec(memory_space=pltpu.SMEM),
                      pl.BlockSpec((t_new, q_nat.shape[1]), lambda i: (n_base + i, 0)),
                      pl.BlockSpec((KVT_ROWS_Q, LANES), lambda i: (0, (n_base + i) // per_tile))]
                     + [cache_spec(c) for c in caches]
                     + [pl.BlockSpec(memory_space=pl.ANY)] * len(prev_out))
        aliases = {len(operands) + 3 + len(caches) + k: 3 + k for k in range(len(prev_out))}
        operands += [sinks_p, q_nat, kvt_new, *caches, *prev_out]
        out_shape += ([jax.ShapeDtypeStruct((steps * t_new, A_OUT), _F32),
                       jax.ShapeDtypeStruct((steps * t_new, B_Q), _F32)]
                      + [jax.ShapeDtypeStruct(c.shape, c.dtype) for c in caches])
        out_specs += ([pl.BlockSpec((t_new, A_OUT), lambda i: (i, 0)),
                       pl.BlockSpec((t_new, B_Q), lambda i: (i, 0))]
                      + [cache_spec(c) for c in caches])
    kern = functools.partial(_ffn_kernel, bounds=tuple(cuts), final_norm=final_norm,
                             host=host_static)
    name = ("ffn_final" if final_norm else "ffn") + ("" if host is None else "_attn_sample")
    return pl.pallas_call(
        kern,
        out_shape=out_shape,
        grid=(steps,),
        in_specs=in_specs,
        out_specs=out_specs,
        input_output_aliases=aliases,
        compiler_params=_cparams(1),
        name=name,
    )(*operands)


def _rope_nat(x, c, s_lo, s_hi):
    return x * c + pltpu.roll(x, ROT_HALF, 1) * s_hi + pltpu.roll(x, LANES - ROT_HALF, 1) * s_lo


def _rope_t(k_t, cos_t, sin_t, n_heads):
    parts = []
    for hh in range(n_heads):
        b = hh * HEAD_DIM
        x1 = k_t[b:b + ROT_HALF]
        x2 = k_t[b + ROT_HALF:b + ROT_DIM]
        parts += [x1 * cos_t - x2 * sin_t, x2 * cos_t + x1 * sin_t, k_t[b + ROT_DIM:b + HEAD_DIM]]
    return jnp.concatenate(parts, axis=0)


def _rope_tables(pos):
    inv = ROPE_THETA ** (-(jnp.arange(ROT_HALF, dtype=_F32) * 2.0 / ROT_DIM))
    ang = pos.astype(_F32)[:, None] * inv[None, :]
    return jnp.cos(ang), jnp.sin(ang)


def _nat_tables(cos, sin):
    p = cos.shape[0]
    one = jnp.ones((p, HEAD_DIM - ROT_DIM), _F32)
    zero = jnp.zeros((p, HEAD_DIM - ROT_DIM), _F32)
    zh = jnp.zeros((p, ROT_HALF), _F32)
    c = jnp.concatenate([cos, cos, one], axis=1)
    s_hi = jnp.concatenate([zh, sin, zero], axis=1)
    s_lo = jnp.concatenate([-sin, zh, zero], axis=1)
    rep = LANES // HEAD_DIM
    return tuple(jnp.tile(t, (1, rep)) for t in (c, s_lo, s_hi))


def _mix_in_prompt_kernel(x_ref, g_ref, w_ref, wt_ref, c_ref, slo_ref, shi_ref, ct_ref, st_ref,
                          *rest, n_prev):
    prev = rest[:4] if n_prev else None
    (q1_ref, k1_ref, v1_ref, q2_ref, k2_ref, v2_ref, q3_ref, k3_ref, v3_ref,
     qb_ref, kb_ref, vb_ref, c1_ref, c2_ref, c3_ref, cb_ref, zs_ref) = rest[-17:]
    j = pl.program_id(1)
    tm = x_ref.shape[0]
    h = _rms(x_ref[...], g_ref[...]).astype(_BF16)
    c, s_lo, s_hi = c_ref[...], slo_ref[...], shi_ref[...]

    def proj(col0, width, kind):
        z = _dot(h, w_ref[:, col0:col0 + width])
        if kind == "v":
            return z
        slabs = []
        for s in range(width // LANES):
            zz = _rope_nat(z[:, s * LANES:(s + 1) * LANES], c, s_lo, s_hi)
            slabs.append(zz * (SCALE * LOG2_E) if kind == "q" else zz)
        return jnp.concatenate(slabs, axis=1)

    for i, (ref, kind) in enumerate(((q1_ref, "q"), (k1_ref, "k"), (v1_ref, "v"))):
        ref[...] = proj(i * A_OUT, A_OUT, kind).astype(_BF16)
    for g, refs in ((1, (q2_ref, k2_ref, v2_ref)), (2, (q3_ref, k3_ref, v3_ref))):
        dil = DIL_PAIRS[g][1]
        rows = tm // dil
        for i, (ref, kind) in enumerate(zip(refs, ("q", "k", "v"))):
            z = proj((3 * g + i) * A_OUT, A_OUT, kind)
            for s in range(A_OUT // LANES):
                zs_ref[s] = z[:, s * LANES:(s + 1) * LANES]
            for r in range(dil):
                for s in range(A_OUT // LANES):
                    ref[r, :, s * LANES:(s + 1) * LANES] = (
                        zs_ref[s, pl.ds(r, rows, stride=dil), :].astype(_BF16))
    col_b = 3 * N_GROUPS_A * A_OUT
    qb_ref[...] = proj(col_b, B_Q, "q").astype(_BF16)
    z = _dot(h, w_ref[:, col_b + B_Q:col_b + B_Q + 2 * B_KV])
    kb_ref[...] = _rope_nat(z[:, :B_KV], c, s_lo, s_hi).astype(_BF16)
    vb_ref[...] = z[:, B_KV:].astype(_BF16)

    def kv_t(row0, n_rows, h_tok, cos_t, sin_t, n_heads):
        z = _dot_nt(wt_ref[row0:row0 + n_rows, :], h_tok)
        half = n_rows // 2
        return jnp.concatenate([_rope_t(z[:half], cos_t, sin_t, n_heads), z[half:]], axis=0)

    def put(i, out_ref, new):
        if n_prev:
            out_ref[:n_prev] = prev[i][...]
        out_ref[n_prev] = new

    ct, st = ct_ref[...], st_ref[...]
    put(2, c3_ref, kv_t(2 * KV_ROWS_A, KV_ROWS_A, h, ct, st, HEADS_A))

    @pl.when(j == pl.num_programs(1) - 1)
    def _():
        put(1, c2_ref, kv_t(KV_ROWS_A, KV_ROWS_A, h, ct, st, HEADS_A))
        tail = tm - BLK
        put(0, c1_ref, kv_t(0, KV_ROWS_A, h[tail:], ct[:, tail:], st[:, tail:], HEADS_A))
        put(3, cb_ref, kv_t(3 * KV_ROWS_A, KV_ROWS_B, h[tail:], ct[:, tail:], st[:, tail:], N_KV_B))


def _mix_in_prompt(x, layer, g, w_qkv, w_kvt, nat_tabs, t_tabs, n_batch, seq, prev_bufs):
    d = x.shape[1]
    n_j = seq // TM
    n_prev = prev_bufs[0].shape[0] if prev_bufs else 0
    row = lambda n, j: (n * n_j + j, 0)
    tok_spec = lambda w: pl.BlockSpec((TM, w), row)
    cls_spec = lambda dil: pl.BlockSpec((None, dil, TM // dil, A_OUT), lambda n, j: (n, 0, j, 0))
    nat = lambda w: jax.ShapeDtypeStruct((n_batch * seq, w), _BF16)
    cls = lambda dil: jax.ShapeDtypeStruct((n_batch, dil, seq // dil, A_OUT), _BF16)
    keep = [min(w, seq) for w, _ in DIL_PAIRS]
    assert keep == [BLK, TM, seq] and seq % TM == 0
    bufs = ((KV_ROWS_A, keep[0], BLK, lambda j: 0), (KV_ROWS_A, keep[1], TM, lambda j: 0),
            (KV_ROWS_A, keep[2], TM, lambda j: j), (KV_ROWS_B, min(WINDOW_B, seq), BLK, lambda j: 0))

    def buf_spec(layers, rows, width, jmap):
        return pl.BlockSpec((layers, None, rows, width), lambda n, j: (0, n, 0, jmap(j)))

    def buf_shape(layers, rows, kept):
        return jax.ShapeDtypeStruct((layers, n_batch, rows, kept), _F32)

    n_out = n_prev + 1
    out_shape = ([nat(A_OUT)] * 3 + [cls(4)] * 3 + [cls(16)] * 3 + [nat(B_Q), nat(B_KV), nat(B_KV)]
                 + [buf_shape(n_out, r, k) for r, k, _, _ in bufs])
    out_specs = ([tok_spec(A_OUT)] * 3 + [cls_spec(4)] * 3 + [cls_spec(16)] * 3
                 + [tok_spec(B_Q), tok_spec(B_KV), tok_spec(B_KV)]
                 + [buf_spec(n_out, r, w, jm) for r, _, w, jm in bufs])
    prev_specs = [buf_spec(n_prev, r, w, jm) for r, _, w, jm in bufs] if n_prev else []
    tab_spec = pl.BlockSpec((TM, LANES), lambda n, j: (j, 0))
    ttab_spec = pl.BlockSpec((ROT_HALF, TM), lambda n, j: (0, j))
    return pl.pallas_call(
        functools.partial(_mix_in_prompt_kernel, n_prev=n_prev),
        out_shape=out_shape,
        grid=(n_batch, n_j),
        in_specs=[tok_spec(d)] + [_layer_block(a, layer) for a in (g, w_qkv, w_kvt)]
        + [tab_spec, tab_spec, tab_spec, ttab_spec, ttab_spec] + prev_specs,
        out_specs=out_specs,
        scratch_shapes=[pltpu.VMEM((A_OUT // LANES, TM, LANES), _F32)],
        compiler_params=_cparams(2),
        name="mix_in_prompt",
    )(x, g, w_qkv, w_kvt, *nat_tabs, *t_tabs, *prev_bufs)


def _mix_in_sample_kernel(x_ref, g_ref, w_ref, wt_ref, c_ref, slo_ref, shi_ref, ct_ref, st_ref,
                          q_ref, kvt_ref):
    h = _rms(x_ref[...], g_ref[...]).astype(_BF16)
    c, s_lo, s_hi = c_ref[...], slo_ref[...], shi_ref[...]
    z = _dot(h, w_ref[...])
    for s in range(z.shape[1] // LANES):
        sl = slice(s * LANES, (s + 1) * LANES)
        q_ref[:, sl] = _rope_nat(z[:, sl], c, s_lo, s_hi) * SCALE
    ct, st = ct_ref[...], st_ref[...]
    for g in range(N_GROUPS_A):
        r0 = g * KV_ROWS_A
        zt = _dot_nt(wt_ref[r0:r0 + KV_ROWS_A, :], h)
        kvt_ref[r0:r0 + A_OUT, :] = _rope_t(zt[:A_OUT], ct, st, HEADS_A)
        kvt_ref[r0 + A_OUT:r0 + KV_ROWS_A, :] = zt[A_OUT:]
    r0 = N_GROUPS_A * KV_ROWS_A
    zt = _dot_nt(wt_ref[r0:r0 + KV_ROWS_B, :], h)
    kvt_ref[r0:r0 + B_KV, :] = _rope_t(zt[:B_KV], ct, st, N_KV_B)
    kvt_ref[r0 + B_KV:r0 + KV_ROWS_B, :] = zt[B_KV:]
    for i in range(len(VPU_GROUPS)):
        r0 = KVT_ROWS + i * A_OUT
        zt = _dot_nt(wt_ref[r0:r0 + A_OUT, :], h)
        kvt_ref[r0:r0 + A_OUT, :] = _rope_t(zt, ct, st, HEADS_A) * SCALE


def _mix_in_sample(x, layer, g, w_q, w_kvt, nat_tabs, t_tabs, tm):
    n_tok, d = x.shape
    qw = w_q.shape[-1]
    tab_spec = pl.BlockSpec((tm, LANES), lambda i: (0, 0))
    ttab_spec = pl.BlockSpec((ROT_HALF, tm), lambda i: (0, 0))
    assert w_kvt.shape[1] == KVT_ROWS_Q
    return pl.pallas_call(
        _mix_in_sample_kernel,
        out_shape=[jax.ShapeDtypeStruct((n_tok, qw), _F32),
                   jax.ShapeDtypeStruct((KVT_ROWS_Q, n_tok), _F32)],
        grid=(n_tok // tm,),
        in_specs=[pl.BlockSpec((tm, d), lambda i: (i, 0))]
        + [_layer_block(a, layer) for a in (g, w_q, w_kvt)]
        + [tab_spec, tab_spec, tab_spec, ttab_spec, ttab_spec],
        out_specs=[pl.BlockSpec((tm, qw), lambda i: (i, 0)),
                   pl.BlockSpec((KVT_ROWS_Q, tm), lambda i: (0, i))],
        compiler_params=_cparams(1),
        name="mix_in_sample",
    )(x, g, w_q, w_kvt, *nat_tabs, *t_tabs)


def _lane_head(t, width):
    return lax.shift_right_logical(lax.broadcasted_iota(jnp.int32, (t, width), 1),
                                   HEAD_DIM.bit_length() - 1)


def _head_stack_a(q):
    lane_head = _lane_head(q.shape[0], A_OUT)
    return jnp.concatenate([jnp.where(lane_head == hh, q, 0.0) for hh in range(HEADS_A)], axis=0)


def _head_unstack_a(o_st, col_st, t):
    lane_head = _lane_head(t, A_OUT)
    out = jnp.zeros((t, A_OUT), _F32)
    for hh in range(HEADS_A):
        col = col_st[hh * t:(hh + 1) * t]
        blk = jnp.broadcast_to(col, (t, A_OUT)) if o_st is None else o_st[hh * t:(hh + 1) * t] * col
        out = jnp.where(lane_head == hh, blk, out)
    return out


def _head_stack_b(q):
    t = q.shape[0]
    lo = lax.broadcasted_iota(jnp.int32, (t, LANES), 1) < HEAD_DIM
    parts = []
    for c in range(B_Q // LANES):
        qc = q[:, c * LANES:(c + 1) * LANES]
        parts += [jnp.where(lo, qc, 0.0), jnp.where(lo, 0.0, qc)]
    return jnp.concatenate(parts, axis=0)


def _head_unstack_b(o_st, col_st, t):
    lo = lax.broadcasted_iota(jnp.int32, (t, LANES), 1) < HEAD_DIM
    cols = []
    for c in range(B_Q // LANES):
        r0 = 2 * c * t
        a = o_st[r0:r0 + t] * col_st[r0:r0 + t]
        b = o_st[r0 + t:r0 + 2 * t] * col_st[r0 + t:r0 + 2 * t]
        cols.append(jnp.where(lo, a, b))
    return jnp.concatenate(cols, axis=1)


def _sink_rows(sink_ref, t):
    return jnp.concatenate(
        [jnp.full((t, 1), sink_ref[p], _F32) for p in range(N_HEADS_B)], axis=0)


def _attn_prompt_kernel(sinkl_ref, q1_ref, k1_ref, v1_ref, q2_ref, k2_ref, v2_ref,
                        q3_ref, k3_ref, v3_ref, qb_ref, kb_ref, vb_ref,
                        oa_ref, ob_ref, acc_s, den_s, m_s, bias_s):
    seq = q1_ref.shape[0]
    n_blk = seq // BLK
    dil2, dil3 = DIL_PAIRS[1][1], DIL_PAIRS[2][1]
    per_class2 = seq // dil2 // BLK
    assert dil2 * per_class2 == dil3 == n_blk and seq // dil3 == BLK

    lane_head_a = _lane_head(BLK, A_OUT)
    head_a_bf = lane_head_a.astype(_F32).astype(_BF16)
    lane_head_b = _lane_head(BLK, LANES)
    head_b_bf = lane_head_b.astype(_F32).astype(_BF16)
    r_i = lax.broadcasted_iota(jnp.int32, (BLK, 2 * BLK), 0)
    c_i = lax.broadcasted_iota(jnp.int32, (BLK, 2 * BLK), 1)
    dist = BLK + r_i - c_i
    for i, n_back in enumerate((BLK, WINDOW_B - 1)):
        band = (dist >= 0) & (dist <= n_back)
        bias_s[2 * i] = jnp.where(band & (c_i >= BLK), 0.0, NEG)
        bias_s[2 * i + 1] = jnp.where(band, 0.0, NEG)
    causal = jnp.where(lax.broadcasted_iota(jnp.int32, (BLK, BLK), 0)
                       >= lax.broadcasted_iota(jnp.int32, (BLK, BLK), 1), 0.0, NEG)

    def with_prev(i, has_prev):
        return bias_s[2 * i + jnp.where(has_prev, 1, 0)]

    def window(ref, lead, r0):
        p0 = pl.multiple_of(jnp.maximum(r0 - BLK, 0), BLK)
        return jnp.concatenate([ref[lead + (pl.ds(p0, BLK),)], ref[lead + (pl.ds(r0, BLK),)]],
                               axis=0)

    def softmax_block(qs, kw, vw, bias, n_heads):
        n_keys = kw.shape[0]
        s = _dot_nt(qs, kw).reshape(n_heads, BLK, n_keys) + bias[None]
        m = jnp.max(s, axis=-1, keepdims=True)
        e = jnp.exp2(s - m)
        den = jnp.sum(e, axis=-1, keepdims=True)
        o = _dot(e.reshape(n_heads * BLK, n_keys).astype(_BF16), vw)
        return o, den.reshape(n_heads * BLK, 1), m.reshape(n_heads * BLK, 1)

    def block_a(q, kw, vw, bias):
        zero = jnp.zeros_like(q)
        qs = jnp.concatenate([jnp.where(head_a_bf == hh, q, zero) for hh in range(HEADS_A)], axis=0)
        o, den, m = softmax_block(qs, kw, vw, bias, HEADS_A)
        acc = o[:BLK]
        den_u = jnp.broadcast_to(den[:BLK], (BLK, A_OUT))
        m_u = jnp.broadcast_to(m[:BLK], (BLK, A_OUT))
        for hh in range(1, HEADS_A):
            sel = lane_head_a == hh
            rows = slice(hh * BLK, (hh + 1) * BLK)
            acc = jnp.where(sel, o[rows], acc)
            den_u = jnp.where(sel, jnp.broadcast_to(den[rows], (BLK, A_OUT)), den_u)
            m_u = jnp.where(sel, jnp.broadcast_to(m[rows], (BLK, A_OUT)), m_u)
        return acc, den_u, m_u

    def store(g, start, stride, acc, den_u, m_u):
        idx = pl.ds(start, BLK, stride=stride)
        for s in range(A_OUT // LANES):
            sl = slice(s * LANES, (s + 1) * LANES)
            acc_s[g, s, idx, :] = acc[:, sl]
            den_s[g, s, idx, :] = den_u[:, sl]
            m_s[g, s, idx, :] = m_u[:, sl]

    def classes_body(i, carry):
        r = lax.div(i, per_class2)
        sb = i - r * per_class2
        s0 = pl.multiple_of(sb * BLK, BLK)
        res = block_a(q2_ref[r, pl.ds(s0, BLK)], window(k2_ref, (r,), s0), window(v2_ref, (r,), s0),
                      with_prev(0, sb > 0))
        store(0, r + dil2 * s0, dil2, *res)
        res = block_a(q3_ref[i], k3_ref[i], v3_ref[i], causal)
        store(1, i, dil3, *res)
        return carry

    lax.fori_loop(0, n_blk, classes_body, 0, unroll=True)

    lo = lane_head_b == 0
    sink_lanes = sinkl_ref[...]

    def tokens_body(b, carry):
        r0 = pl.multiple_of(b * BLK, BLK)
        acc1, den1, m1 = block_a(q1_ref[pl.ds(r0, BLK)], window(k1_ref, (), r0),
                                 window(v1_ref, (), r0), with_prev(0, b > 0))
        cols = []
        for s in range(A_OUT // LANES):
            sl = slice(s * LANES, (s + 1) * LANES)
            rows = pl.ds(r0, BLK)
            ms = [m1[:, sl], m_s[0, s, rows, :], m_s[1, s, rows, :]]
            dens = [den1[:, sl], den_s[0, s, rows, :], den_s[1, s, rows, :]]
            accs = [acc1[:, sl], acc_s[0, s, rows, :], acc_s[1, s, rows, :]]
            top = jnp.maximum(jnp.maximum(ms[0], ms[1]), ms[2])
            ws = [jnp.exp2(mm - top) for mm in ms]
            num = ws[0] * accs[0] + ws[1] * accs[1] + ws[2] * accs[2]
            dn = ws[0] * dens[0] + ws[1] * dens[1] + ws[2] * dens[2]
            cols.append(num / dn)
        oa_ref[pl.ds(r0, BLK), :] = jnp.concatenate(cols, axis=1).astype(oa_ref.dtype)

        q = qb_ref[pl.ds(r0, BLK)]
        zero = jnp.zeros((BLK, LANES), q.dtype)
        parts = []
        for c in range(B_Q // LANES):
            qc = q[:, c * LANES:(c + 1) * LANES]
            parts += [jnp.where(head_b_bf == 0, qc, zero), jnp.where(head_b_bf == 1, qc, zero)]
        o, den, m = softmax_block(jnp.concatenate(parts, axis=0), window(kb_ref, (), r0),
                                  window(vb_ref, (), r0), with_prev(1, b > 0), N_HEADS_B)
        cols = []
        for c in range(B_Q // LANES):
            ra = slice(2 * c * BLK, (2 * c + 1) * BLK)
            rb = slice((2 * c + 1) * BLK, (2 * c + 2) * BLK)
            o_c = jnp.where(lo, o[ra], o[rb])
            den_c = jnp.where(lo, jnp.broadcast_to(den[ra], (BLK, LANES)),
                              jnp.broadcast_to(den[rb], (BLK, LANES)))
            m_c = jnp.where(lo, jnp.broadcast_to(m[ra], (BLK, LANES)),
                            jnp.broadcast_to(m[rb], (BLK, LANES)))
            sink_c = sink_lanes[:, c * LANES:(c + 1) * LANES]
            cols.append(o_c / (den_c + jnp.exp2(sink_c - m_c)))
        ob_ref[pl.ds(r0, BLK), :] = jnp.concatenate(cols, axis=1).astype(ob_ref.dtype)
        return carry

    lax.fori_loop(0, n_blk, tokens_body, 0, unroll=True)


def _attn_prompt(sinks_p, qkv, n_batch, seq):
    (q1, k1, v1, q2, k2, v2, q3, k3, v3, qb, kb, vb) = qkv
    tok = lambda w: pl.BlockSpec((seq, w), lambda n: (n, 0))
    cls = lambda dil: pl.BlockSpec((None, dil, seq // dil, A_OUT), lambda n: (n, 0, 0, 0))
    return pl.pallas_call(
        _attn_prompt_kernel,
        out_shape=[jax.ShapeDtypeStruct((n_batch * seq, A_OUT), _BF16),
                   jax.ShapeDtypeStruct((n_batch * seq, B_Q), _BF16)],
        grid=(n_batch,),
        in_specs=[_resident((1, B_Q))]
        + [tok(A_OUT)] * 3 + [cls(4)] * 3 + [cls(16)] * 3 + [tok(B_Q), tok(B_KV), tok(B_KV)],
        out_specs=[tok(A_OUT), tok(B_Q)],
        scratch_shapes=[pltpu.VMEM((N_GROUPS_A - 1, A_OUT // LANES, seq, LANES), _F32)] * 3
        + [pltpu.VMEM((4, BLK, 2 * BLK), _F32)],
        compiler_params=_cparams(1),
        name="attn_prompt",
    )(jnp.repeat(sinks_p * LOG2_E, HEAD_DIM)[None, :], q1, k1, v1, q2, k2, v2, q3, k3, v3,
      qb, kb, vb)


def _attn_sample_body(sink_ref, q_ref, kvt_ref, cache_refs, oa_ref, ob_ref, new_refs, n, t_new):
    c1_ref, c2_ref, c3_ref, cb_ref = cache_refs
    n1_ref, n2_ref, n3_ref, nb_ref = new_refs
    per_tile = LANES // t_new
    first_new = LANES - t_new
    shift = first_new - (n & (per_tile - 1)) * t_new
    new_t = pltpu.roll(kvt_ref[...], shift, 1)
    q = q_ref[...]

    def new_cache(old_ref, new_rows, out_ref):
        lb = old_ref.shape[1]
        rolled = pltpu.roll(old_ref[...], lb - t_new, 1)
        if lb > LANES:
            out_ref[:, :lb - LANES] = rolled[:, :lb - LANES]
        lane = lax.broadcasted_iota(jnp.int32, (old_ref.shape[0], LANES), 1)
        out_ref[:, lb - LANES:] = jnp.where(lane >= first_new, new_rows, rolled[:, lb - LANES:])

    def masks(rows, lb, window, dil, strict):
        t_old = lax.broadcasted_iota(jnp.int32, (rows, lb), 0) & (t_new - 1)
        dist = lb + t_old - lax.broadcasted_iota(jnp.int32, (rows, lb), 1)
        hi = (dist < window) if strict else (dist <= window)
        valid_old = (dist >= 0) & hi & ((dist & (dil - 1)) == 0)
        t_nw = lax.broadcasted_iota(jnp.int32, (rows, LANES), 0) & (t_new - 1)
        c_nw = lax.broadcasted_iota(jnp.int32, (rows, LANES), 1) - first_new
        dn = t_nw - c_nw
        hin = (dn < window) if strict else (dn <= window)
        valid_new = (c_nw >= 0) & (dn >= 0) & hin & ((dn & (dil - 1)) == 0)
        return valid_old, valid_new

    def attend(qs, k_old, v_old, k_new, v_new, valid_old, valid_new):
        s_old = jnp.where(valid_old, _dot(qs, k_old), NEG)
        s_new = jnp.where(valid_new, _dot(qs, k_new), NEG)
        m = jnp.maximum(jnp.max(s_old, axis=-1, keepdims=True),
                        jnp.max(s_new, axis=-1, keepdims=True))
        e_old = jnp.exp(s_old - m)
        e_new = jnp.exp(s_new - m)
        den = jnp.sum(e_old, axis=-1, keepdims=True) + jnp.sum(e_new, axis=-1, keepdims=True)
        o = _dot_nt(e_old.astype(_BF16), v_old) + _dot_nt(e_new.astype(_BF16), v_new)
        return o, den, m

    def classes_reduce(x, dil, op):
        sh = dil
        while sh < LANES:
            x = op(x, pltpu.roll(x, sh, 1))
            sh *= 2
        return x

    def fold_tiles(x, op):
        acc = x[:, :LANES]
        for jt in range(1, x.shape[1] // LANES):
            acc = op(acc, x[:, jt * LANES:(jt + 1) * LANES])
        return acc

    def attend_lane_classes(old_ref, new_g, q_t, window, dil):
        lb = old_ref.shape[1]
        n_tiles = lb // LANES
        assert t_new <= dil and lb % dil == 0 and LANES % dil == 0 and lb <= window
        lane_q = lax.broadcasted_iota(jnp.int32, (A_OUT, LANES), 1)
        q_new = jnp.where(lane_q >= first_new, q_t, 0.0)
        q_cls = classes_reduce(pltpu.roll(q_new, t_new, 1), dil, jnp.add)
        lane_1 = lax.broadcasted_iota(jnp.int32, (1, LANES), 1)
        cls_ok = jnp.concatenate([(lane_1 & (dil - 1)) < t_new] * n_tiles, axis=1)
        o_rows, lse_rows = [], []
        for hh in range(HEADS_A):
            rows = slice(hh * HEAD_DIM, (hh + 1) * HEAD_DIM)
            q_h = jnp.concatenate([q_cls[rows]] * n_tiles, axis=1)
            s_old = jnp.sum(old_ref[rows, :] * q_h, axis=0, keepdims=True)
            s_old = jnp.where(cls_ok, s_old, NEG)
            s_new = jnp.sum(new_g[rows] * q_new[rows], axis=0, keepdims=True)
            s_new = jnp.where(lane_1 < t_new, pltpu.roll(s_new, t_new, 1), NEG)
            m = classes_reduce(jnp.maximum(fold_tiles(s_old, jnp.maximum), s_new), dil, jnp.maximum)
            e_old = jnp.where(cls_ok, jnp.exp(s_old - jnp.concatenate([m] * n_tiles, axis=1)), 0.0)
            e_new = jnp.where(lane_1 < t_new, jnp.exp(s_new - m), 0.0)
            den = classes_reduce(fold_tiles(e_old, jnp.add) + e_new, dil, jnp.add)
            den = jnp.where((lane_1 & (dil - 1)) < t_new, den, 1.0)
            v_rows = slice(A_OUT + hh * HEAD_DIM, A_OUT + (hh + 1) * HEAD_DIM)
            acc = fold_tiles(old_ref[v_rows, :] * e_old, jnp.add)
            acc = acc + pltpu.roll(new_g[v_rows], t_new, 1) * e_new
            acc = classes_reduce(acc, dil, jnp.add)
            o_rows.append(acc / den)
            lse_rows.append(jnp.broadcast_to(m + jnp.log(den), (HEAD_DIM, LANES)))
        out_t = jnp.concatenate(o_rows, axis=0)
        lse_t = jnp.concatenate(lse_rows, axis=0)
        return out_t.T[:t_new], lse_t.T[:t_new]

    outs, lses = [], []
    for g, (old_ref, out_ref) in enumerate(((c1_ref, n1_ref), (c2_ref, n2_ref), (c3_ref, n3_ref))):
        window, dil = DIL_PAIRS[g]
        lb = old_ref.shape[1]
        new_g = new_t[g * KV_ROWS_A:(g + 1) * KV_ROWS_A]
        if g in VPU_GROUPS:
            q_row0 = KVT_ROWS + VPU_GROUPS.index(g) * A_OUT
            out, lse = attend_lane_classes(old_ref, new_g, new_t[q_row0:q_row0 + A_OUT],
                                           window, dil)
            outs.append(out)
            lses.append(lse)
            new_cache(old_ref, new_g, out_ref)
            continue
        q0 = MXU_GROUPS.index(g) * A_OUT
        qs = _head_stack_a(q[:, q0:q0 + A_OUT]).astype(_BF16)
        valid_old, valid_new = masks(HEADS_A * t_new, lb, window, dil, False)
        o, den, m = attend(qs, old_ref[0:A_OUT, :].astype(_BF16), old_ref[A_OUT:, :].astype(_BF16),
                           new_g[:A_OUT].astype(_BF16), new_g[A_OUT:].astype(_BF16),
                           valid_old, valid_new)
        outs.append(_head_unstack_a(o, 1.0 / den, t_new))
        lses.append(_head_unstack_a(None, m + jnp.log(den), t_new))
        new_cache(old_ref, new_g, out_ref)
    m = jnp.maximum(jnp.maximum(lses[0], lses[1]), lses[2])
    ws = [jnp.exp(l - m) for l in lses]
    oa_ref[...] = (ws[0] * outs[0] + ws[1] * outs[1] + ws[2] * outs[2]) / (ws[0] + ws[1] + ws[2])

    lb = cb_ref.shape[1]
    new_b = new_t[N_GROUPS_A * KV_ROWS_A:KVT_ROWS]
    qs = _head_stack_b(q[:, len(MXU_GROUPS) * A_OUT:]).astype(_BF16)
    valid_old, valid_new = masks(N_HEADS_B * t_new, lb, WINDOW_B, 1, True)
    o, den, m = attend(qs, cb_ref[0:B_KV, :].astype(_BF16), cb_ref[B_KV:, :].astype(_BF16),
                       new_b[:B_KV].astype(_BF16), new_b[B_KV:].astype(_BF16),
                       valid_old, valid_new)
    scale = _sigmoid(m + jnp.log(den) - _sink_rows(sink_ref, t_new)) / den
    ob_ref[...] = _head_unstack_b(o, scale, t_new)
    new_cache(cb_ref, new_b, nb_ref)


def _mix_out_kernel(x_ref, g_ref, oa_ref, ob_ref, wga_ref, wgb_ref, wba_ref, wbb_ref, wout_ref,
                    o_ref):
    x = x_ref[...]
    h = _rms(x, g_ref[...]).astype(_BF16)
    gate_a = _dot(h, wga_ref[...])
    gate_b = _dot(h, wgb_ref[...])
    ya = _dot(oa_ref[...].astype(_BF16), wba_ref[...])
    yb = _dot(ob_ref[...].astype(_BF16), wbb_ref[...])
    merged = _sigmoid(gate_a) * ya + _sigmoid(gate_b) * yb
    o_ref[...] = x + _dot(merged.astype(_BF16), wout_ref[...])


def _mix_out(x, layer, g, oa, ob, w_in, w_ba, w_bb, w_out, tm):
    n_tok, d = x.shape
    assert QKV_COLS % d == 0 and w_in.shape[2] == QKV_COLS + 2 * d
    gate_specs = [_layer_block(w_in, layer, (d, d), (0, QKV_COLS // d + k)) for k in range(2)]
    tok = lambda w: pl.BlockSpec((tm, w), lambda i: (i, 0))
    return pl.pallas_call(
        _mix_out_kernel,
        out_shape=jax.ShapeDtypeStruct((n_tok, d), _F32),
        grid=(n_tok // tm,),
        in_specs=[tok(d), _layer_block(g, layer), tok(A_OUT), tok(B_Q)] + gate_specs
        + [_layer_block(a, layer) for a in (w_ba, w_bb, w_out)],
        out_specs=tok(d),
        compiler_params=_cparams(1),
        name="mix_out",
    )(x, g, oa, ob, w_in, w_in, w_ba, w_bb, w_out)


def _cache_view(c):
    dep, nb, rows = c.shape[:3]
    return jnp.transpose(c, (0, 1, 3, 4, 5, 2)).reshape(dep, nb, -1, rows)


def _cache_unview(c, heads):
    dep, nb, _, rows = c.shape
    return jnp.transpose(c.reshape(dep, nb, 2, heads, HEAD_DIM, rows), (0, 1, 5, 2, 3, 4))


def kernel(x_prompt, x_sample, cache_a1, cache_a2, cache_a3, cache_b, norm_ffn1, ffn1_gate, ffn1_up, ffn1_down, norm_mix, w_in, sinks, w_branch_a, w_branch_b, w_out, norm_ffn2, ffn2_gate, ffn2_up, ffn2_down, norm_final):
    n_batch, seq, d = x_prompt.shape
    dec_batch, t_new, _ = x_sample.shape
    depth = w_in.shape[0]
    n_sample = dec_batch * t_new
    tm_s = min(TM, n_sample)
    hosts_per_layer = 2
    n_hosted = dec_batch // hosts_per_layer
    tm_p = n_batch * seq // n_hosted
    assert seq == DIL_PAIRS[2][0] and LANES % t_new == 0 and n_sample % tm_s == 0
    assert tm_s % t_new == 0 and dec_batch % (LANES // t_new) == 0
    assert dec_batch % hosts_per_layer == 0 and (n_batch * seq) % n_hosted == 0 and tm_p % 8 == 0

    perm = jnp.array(B_HEAD_PERM)
    col_b = 3 * N_GROUPS_A * A_OUT

    cos_p, sin_p = _rope_tables(jnp.arange(seq))
    cos_s, sin_s = _rope_tables(PAST_LEN + jnp.arange(t_new))
    cos_s, sin_s = jnp.tile(cos_s, (tm_s // t_new, 1)), jnp.tile(sin_s, (tm_s // t_new, 1))
    nat_p, nat_s = _nat_tables(cos_p, sin_p), _nat_tables(cos_s, sin_s)
    t_p, t_s = (cos_p.T, sin_p.T), (cos_s.T, sin_s.T)

    caches = [_cache_view(c) for c in (cache_a1, cache_a2, cache_a3, cache_b)]

    bf = lambda w: w.astype(_BF16)
    w_in_bf = bf(w_in)
    cols = lambda lo, hi: w_in_bf[:, :, lo:hi]
    bq = cols(col_b, col_b + B_Q).reshape(depth, d, N_HEADS_B, HEAD_DIM)[:, :, perm]
    bq = bq.reshape(depth, d, B_Q)
    q_cols = [cols(3 * g * A_OUT, (3 * g + 1) * A_OUT) for g in range(N_GROUPS_A)]
    kv_cols = [cols((3 * g + 1) * A_OUT, (3 * g + 3) * A_OUT) for g in range(N_GROUPS_A)]
    kv_cols.append(cols(col_b + B_Q, QKV_COLS))
    w_qkv = jnp.concatenate([cols(0, col_b), bq, cols(col_b + B_Q, QKV_COLS)], axis=2)
    w_kvt = jnp.swapaxes(jnp.concatenate(kv_cols + [q_cols[g] for g in VPU_GROUPS], axis=2), 1, 2)
    w_q = jnp.concatenate([q_cols[g] for g in MXU_GROUPS] + [bq], axis=2)
    w_bb = bf(w_branch_b).reshape(depth, N_HEADS_B, HEAD_DIM, d)[:, perm].reshape(depth, B_Q, d)
    w_ba, w_o = bf(w_branch_a), bf(w_out)
    sinks_perm = sinks[:, perm]
    row = lambda v: v.reshape(-1, 1, d)
    f1 = (row(norm_ffn1), bf(ffn1_gate), bf(ffn1_up), bf(ffn1_down))
    f2 = (row(norm_ffn2), bf(ffn2_gate), bf(ffn2_up), bf(ffn2_down))
    g_mix = row(norm_mix)
    gf = norm_final.reshape(1, d)

    xp = x_prompt.reshape(n_batch * seq, d)
    xs = x_sample.reshape(dec_batch * t_new, d)
    prompt_bufs = []
    sample_bufs = None
    for l in range(depth):
        last = l == depth - 1
        sinks_p = sinks_perm[l]

        xs = _ffn(xs, l, *f1, gf, final_norm=False, tm=tm_s)[0]
        q_nat, kvt_new = _mix_in_sample(xs, l, g_mix, w_q, w_kvt, nat_s, t_s, tm_s)
        host = lambda n_base: (n_base, t_new, sinks_p, q_nat, kvt_new, caches, sample_bufs)

        xp, oa_0, ob_0, *sample_bufs = _ffn(xp, l, *f1, gf, final_norm=False, tm=tm_p,
                                            host=host(0))
        outs = _mix_in_prompt(xp, l, g_mix, w_qkv, w_kvt, nat_p, t_p, n_batch, seq, prompt_bufs)
        prompt_bufs = list(outs[12:])
        oa_p, ob_p = _attn_prompt(sinks_p, outs[:12], n_batch, seq)
        xp = _mix_out(xp, l, g_mix, oa_p, ob_p, w_in_bf, w_ba, w_bb, w_o, TM)
        xp, oa_1, ob_1, *sample_bufs = _ffn(xp, l, *f2, gf, final_norm=last, tm=tm_p,
                                            host=host(n_hosted))

        oa_s = jnp.concatenate([oa_0, oa_1], axis=0)
        ob_s = jnp.concatenate([ob_0, ob_1], axis=0)
        xs = _mix_out(xs, l, g_mix, oa_s, ob_s, w_in_bf, w_ba, w_bb, w_o, tm_s)
        xs = _ffn(xs, l, *f2, gf, final_norm=last, tm=tm_s)[0]

    heads = (HEADS_A, HEADS_A, HEADS_A, N_KV_B)
    new_p = [_cache_unview(prompt_bufs[i], heads[i]) for i in range(4)]
    new_s = [_cache_unview(sample_bufs[i], heads[i]) for i in range(4)]
    return (xp.reshape(n_batch, seq, d), xs.reshape(dec_batch, t_new, d), *new_p, *new_s)
```

```python
import functools

import jax
import jax.numpy as jnp
from jax import lax
from jax.experimental import pallas as pl
from jax.experimental.pallas import tpu as pltpu

PAST_LEN = 16384
HEAD_DIM = 64
ROT_DIM = HEAD_DIM // 4
ROT_HALF = ROT_DIM // 2
ROPE_THETA = 500000.0
DIL_PAIRS = ((128, 1), (512, 4), (2048, 16))
N_GROUPS_A = 3
HEADS_A = 4
A_OUT = HEADS_A * HEAD_DIM
N_HEADS_B = 8
N_KV_B = 2
WINDOW_B = 128
B_Q = N_HEADS_B * HEAD_DIM
B_KV = N_KV_B * HEAD_DIM
QKV_COLS = N_GROUPS_A * 3 * A_OUT + B_Q + 2 * B_KV
KV_ROWS_A = 2 * A_OUT
KV_ROWS_B = 2 * B_KV
KVT_ROWS = N_GROUPS_A * KV_ROWS_A + KV_ROWS_B
VPU_GROUPS = (2,)
MXU_GROUPS = tuple(g for g in range(N_GROUPS_A) if g not in VPU_GROUPS)
KVT_ROWS_Q = KVT_ROWS + len(VPU_GROUPS) * A_OUT
NORM_EPS = 1e-6
NEG = -1e30
SCALE = HEAD_DIM ** -0.5
LOG2_E = 1.4426950408889634
BLK = 128
LANES = 128
MXU_TILE = 256
TM = 512
FFN_CHUNKS = 1
B_HEAD_PERM = (0, 4, 1, 5, 2, 6, 3, 7)
V7X_SCOPED_VMEM_BYTES = 60000 * 1024

_BF16 = jnp.bfloat16
_F32 = jnp.float32


def _cparams(n_axes):
    return pltpu.CompilerParams(
        dimension_semantics=("arbitrary",) * n_axes,
        vmem_limit_bytes=V7X_SCOPED_VMEM_BYTES,
    )


def _resident(shape):
    nd = len(shape)
    return pl.BlockSpec(shape, lambda *_: (0,) * nd, pipeline_mode=pl.Buffered(1))


def _layer_block(arr, layer, block=None, index=None):
    block = tuple(arr.shape[1:]) if block is None else tuple(block)
    index = (0,) * len(block) if index is None else tuple(index)
    return pl.BlockSpec((None,) + block, lambda *_: (layer,) + index,
                        pipeline_mode=pl.Buffered(1))


def _rms(x, g):
    return x * lax.rsqrt(jnp.mean(x * x, axis=-1, keepdims=True) + NORM_EPS) * g


def _sigmoid(x):
    return 1.0 / (1.0 + jnp.exp(-x))


def _dot(a, b):
    return jnp.dot(a, b, preferred_element_type=_F32)


def _dot_nt(a, b):
    return lax.dot_general(a, b, (((1,), (1,)), ((), ())), preferred_element_type=_F32)


def _ffn_kernel(x_ref, g_ref, wg_ref, wu_ref, wd_ref, gf_ref, *rest, bounds, final_norm, host):
    o_ref = rest[-1] if host is None else rest[-7]
    if host is not None:
        n_base, t_new = host
        _attn_sample_body(rest[0], rest[1], rest[2], rest[3:7], rest[-6], rest[-5], rest[-4:],
                          n_base + pl.program_id(0), t_new)
    x = x_ref[...]
    h = _rms(x, g_ref[...]).astype(_BF16)
    acc = None
    for lo, hi in zip(bounds[:-1], bounds[1:]):
        gate = _dot(h, wg_ref[:, lo:hi])
        up = _dot(h, wu_ref[:, lo:hi])
        act = (gate * _sigmoid(gate) * up).astype(_BF16)
        part = _dot(act, wd_ref[lo:hi, :])
        acc = part if acc is None else acc + part
    y = x + 0.5 * acc
    if final_norm:
        y = _rms(y, gf_ref[...])
    o_ref[...] = y


def _ffn(x, layer, g, wg, wu, wd, gf, *, final_norm, tm, host=None):
    n_tok, d = x.shape
    f = wg.shape[-1]
    steps = n_tok // tm
    n_tiles = f // MXU_TILE
    cuts = [0, (n_tiles + 1) // 2 * MXU_TILE, f] if FFN_CHUNKS == 2 and n_tiles > 1 else [0, f]
    tok_spec = pl.BlockSpec((tm, d), lambda i: (i, 0))
    in_specs = ([tok_spec] + [_layer_block(a, layer) for a in (g, wg, wu, wd)]
                + [_resident((1, d))])
    out_shape = [jax.ShapeDtypeStruct((n_tok, d), _F32)]
    out_specs = [tok_spec]
    operands = [x, g, wg, wu, wd, gf]
    aliases = {}
    host_static = None
    if host is not None:
        n_base, t_new, sinks_p, q_nat, kvt_new, caches, prev_out = host
        per_tile = LANES // t_new
        host_static = (n_base, t_new)
        cache_spec = lambda c: pl.BlockSpec((None, None) + c.shape[2:],
                                            lambda i: (layer, n_base + i, 0, 0))
        prev_out = [] if prev_out is None else list(prev_out)
        in_specs += ([pl.BlockSpec(memory_space=pltpu.SMEM),
                      pl.BlockSpec((t_new, q_nat.shape[1]), lambda i: (n_base + i, 0)),
                      pl.BlockSpec((KVT_ROWS_Q, LANES), lambda i: (0, (n_base + i) // per_tile))]
                     + [cache_spec(c) for c in caches]
                     + [pl.BlockSpec(memory_space=pl.ANY)] * len(prev_out))
        aliases = {len(operands) + 3 + len(caches) + k: 3 + k for k in range(len(prev_out))}
        operands += [sinks_p, q_nat, kvt_new, *caches, *prev_out]
        out_shape += ([jax.ShapeDtypeStruct((steps * t_new, A_OUT), _F32),
                       jax.ShapeDtypeStruct((steps * t_new, B_Q), _F32)]
                      + [jax.ShapeDtypeStruct(c.shape, c.dtype) for c in caches])
        out_specs += ([pl.BlockSpec((t_new, A_OUT), lambda i: (i, 0)),
                       pl.BlockSpec((t_new, B_Q), lambda i: (i, 0))]
                      + [cache_spec(c) for c in caches])
    kern = functools.partial(_ffn_kernel, bounds=tuple(cuts), final_norm=final_norm,
                             host=host_static)
    name = ("ffn_final" if final_norm else "ffn") + ("" if host is None else "_attn_sample")
    return pl.pallas_call(
        kern,
        out_shape=out_shape,
        grid=(steps,),
        in_specs=in_specs,
        out_specs=out_specs,
        input_output_aliases=aliases,
        compiler_params=_cparams(1),
        name=name,
    )(*operands)


def _rope_nat(x, c, s_lo, s_hi):
    return x * c + pltpu.roll(x, ROT_HALF, 1) * s_hi + pltpu.roll(x, LANES - ROT_HALF, 1) * s_lo


def _rope_t(k_t, cos_t, sin_t, n_heads):
    parts = []
    for hh in range(n_heads):
        b = hh * HEAD_DIM
        x1 = k_t[b:b + ROT_HALF]
        x2 = k_t[b + ROT_HALF:b + ROT_DIM]
        parts += [x1 * cos_t - x2 * sin_t, x2 * cos_t + x1 * sin_t, k_t[b + ROT_DIM:b + HEAD_DIM]]
    return jnp.concatenate(parts, axis=0)


def _rope_tables(pos):
    inv = ROPE_THETA ** (-(jnp.arange(ROT_HALF, dtype=_F32) * 2.0 / ROT_DIM))
    ang = pos.astype(_F32)[:, None] * inv[None, :]
    return jnp.cos(ang), jnp.sin(ang)


def _nat_tables(cos, sin):
    p = cos.shape[0]
    one = jnp.ones((p, HEAD_DIM - ROT_DIM), _F32)
    zero = jnp.zeros((p, HEAD_DIM - ROT_DIM), _F32)
    zh = jnp.zeros((p, ROT_HALF), _F32)
    c = jnp.concatenate([cos, cos, one], axis=1)
    s_hi = jnp.concatenate([zh, sin, zero], axis=1)
    s_lo = jnp.concatenate([-sin, zh, zero], axis=1)
    rep = LANES // HEAD_DIM
    return tuple(jnp.tile(t, (1, rep)) for t in (c, s_lo, s_hi))


def _mix_in_prompt_kernel(x_ref, g_ref, w_ref, wt_ref, c_ref, slo_ref, shi_ref, ct_ref, st_ref,
                          *rest, n_prev):
    prev = rest[:4] if n_prev else None
    (q1_ref, k1_ref, v1_ref, q2_ref, k2_ref, v2_ref, q3_ref, k3_ref, v3_ref,
     qb_ref, kb_ref, vb_ref, c1_ref, c2_ref, c3_ref, cb_ref, zs_ref) = rest[-17:]
    j = pl.program_id(1)
    tm = x_ref.shape[0]
    h = _rms(x_ref[...], g_ref[...]).astype(_BF16)
    c, s_lo, s_hi = c_ref[...], slo_ref[...], shi_ref[...]

    def proj(col0, width, kind):
        z = _dot(h, w_ref[:, col0:col0 + width])
        if kind == "v":
            return z
        slabs = []
        for s in range(width // LANES):
            zz = _rope_nat(z[:, s * LANES:(s + 1) * LANES], c, s_lo, s_hi)
            slabs.append(zz * (SCALE * LOG2_E) if kind == "q" else zz)
        return jnp.concatenate(slabs, axis=1)

    for i, (ref, kind) in enumerate(((q1_ref, "q"), (k1_ref, "k"), (v1_ref, "v"))):
        ref[...] = proj(i * A_OUT, A_OUT, kind).astype(_BF16)
    for g, refs in ((1, (q2_ref, k2_ref, v2_ref)), (2, (q3_ref, k3_ref, v3_ref))):
        dil = DIL_PAIRS[g][1]
        rows = tm // dil
        for i, (ref, kind) in enumerate(zip(refs, ("q", "k", "v"))):
            z = proj((3 * g + i) * A_OUT, A_OUT, kind)
            for s in range(A_OUT // LANES):
                zs_ref[s] = z[:, s * LANES:(s + 1) * LANES]
            for r in range(dil):
                for s in range(A_OUT // LANES):
                    ref[r, :, s * LANES:(s + 1) * LANES] = (
                        zs_ref[s, pl.ds(r, rows, stride=dil), :].astype(_BF16))
    col_b = 3 * N_GROUPS_A * A_OUT
    qb_ref[...] = proj(col_b, B_Q, "q").astype(_BF16)
    z = _dot(h, w_ref[:, col_b + B_Q:col_b + B_Q + 2 * B_KV])
    kb_ref[...] = _rope_nat(z[:, :B_KV], c, s_lo, s_hi).astype(_BF16)
    vb_ref[...] = z[:, B_KV:].astype(_BF16)

    def kv_t(row0, n_rows, h_tok, cos_t, sin_t, n_heads):
        z = _dot_nt(wt_ref[row0:row0 + n_rows, :], h_tok)
        half = n_rows // 2
        return jnp.concatenate([_rope_t(z[:half], cos_t, sin_t, n_heads), z[half:]], axis=0)

    def put(i, out_ref, new):
        if n_prev:
            out_ref[:n_prev] = prev[i][...]
        out_ref[n_prev] = new

    ct, st = ct_ref[...], st_ref[...]
    put(2, c3_ref, kv_t(2 * KV_ROWS_A, KV_ROWS_A, h, ct, st, HEADS_A))

    @pl.when(j == pl.num_programs(1) - 1)
    def _():
        put(1, c2_ref, kv_t(KV_ROWS_A, KV_ROWS_A, h, ct, st, HEADS_A))
        tail = tm - BLK
        put(0, c1_ref, kv_t(0, KV_ROWS_A, h[tail:], ct[:, tail:], st[:, tail:], HEADS_A))
        put(3, cb_ref, kv_t(3 * KV_ROWS_A, KV_ROWS_B, h[tail:], ct[:, tail:], st[:, tail:], N_KV_B))


def _mix_in_prompt(x, layer, g, w_qkv, w_kvt, nat_tabs, t_tabs, n_batch, seq, prev_bufs):
    d = x.shape[1]
    n_j = seq // TM
    n_prev = prev_bufs[0].shape[0] if prev_bufs else 0
    row = lambda n, j: (n * n_j + j, 0)
    tok_spec = lambda w: pl.BlockSpec((TM, w), row)
    cls_spec = lambda dil: pl.BlockSpec((None, dil, TM // dil, A_OUT), lambda n, j: (n, 0, j, 0))
    nat = lambda w: jax.ShapeDtypeStruct((n_batch * seq, w), _BF16)
    cls = lambda dil: jax.ShapeDtypeStruct((n_batch, dil, seq // dil, A_OUT), _BF16)
    keep = [min(w, seq) for w, _ in DIL_PAIRS]
    assert keep == [BLK, TM, seq] and seq % TM == 0
    bufs = ((KV_ROWS_A, keep[0], BLK, lambda j: 0), (KV_ROWS_A, keep[1], TM, lambda j: 0),
            (KV_ROWS_A, keep[2], TM, lambda j: j), (KV_ROWS_B, min(WINDOW_B, seq), BLK, lambda j: 0))

    def buf_spec(layers, rows, width, jmap):
        return pl.BlockSpec((layers, None, rows, width), lambda n, j: (0, n, 0, jmap(j)))

    def buf_shape(layers, rows, kept):
        return jax.ShapeDtypeStruct((layers, n_batch, rows, kept), _F32)

    n_out = n_prev + 1
    out_shape = ([nat(A_OUT)] * 3 + [cls(4)] * 3 + [cls(16)] * 3 + [nat(B_Q), nat(B_KV), nat(B_KV)]
                 + [buf_shape(n_out, r, k) for r, k, _, _ in bufs])
    out_specs = ([tok_spec(A_OUT)] * 3 + [cls_spec(4)] * 3 + [cls_spec(16)] * 3
                 + [tok_spec(B_Q), tok_spec(B_KV), tok_spec(B_KV)]
                 + [buf_spec(n_out, r, w, jm) for r, _, w, jm in bufs])
    prev_specs = [buf_spec(n_prev, r, w, jm) for r, _, w, jm in bufs] if n_prev else []
    tab_spec = pl.BlockSpec((TM, LANES), lambda n, j: (j, 0))
    ttab_spec = pl.BlockSpec((ROT_HALF, TM), lambda n, j: (0, j))
    return pl.pallas_call(
        functools.partial(_mix_in_prompt_kernel, n_prev=n_prev),
        out_shape=out_shape,
        grid=(n_batch, n_j),
        in_specs=[tok_spec(d)] + [_layer_block(a, layer) for a in (g, w_qkv, w_kvt)]
        + [tab_spec, tab_spec, tab_spec, ttab_spec, ttab_spec] + prev_specs,
        out_specs=out_specs,
        scratch_shapes=[pltpu.VMEM((A_OUT // LANES, TM, LANES), _F32)],
        compiler_params=_cparams(2),
        name="mix_in_prompt",
    )(x, g, w_qkv, w_kvt, *nat_tabs, *t_tabs, *prev_bufs)


def _mix_in_sample_kernel(x_ref, g_ref, w_ref, wt_ref, c_ref, slo_ref, shi_ref, ct_ref, st_ref,
                          q_ref, kvt_ref):
    h = _rms(x_ref[...], g_ref[...]).astype(_BF16)
    c, s_lo, s_hi = c_ref[...], slo_ref[...], shi_ref[...]
    z = _dot(h, w_ref[...])
    for s in range(z.shape[1] // LANES):
        sl = slice(s * LANES, (s + 1) * LANES)
        q_ref[:, sl] = _rope_nat(z[:, sl], c, s_lo, s_hi) * SCALE
    ct, st = ct_ref[...], st_ref[...]
    for g in range(N_GROUPS_A):
        r0 = g * KV_ROWS_A
        zt = _dot_nt(wt_ref[r0:r0 + KV_ROWS_A, :], h)
        kvt_ref[r0:r0 + A_OUT, :] = _rope_t(zt[:A_OUT], ct, st, HEADS_A)
        kvt_ref[r0 + A_OUT:r0 + KV_ROWS_A, :] = zt[A_OUT:]
    r0 = N_GROUPS_A * KV_ROWS_A
    zt = _dot_nt(wt_ref[r0:r0 + KV_ROWS_B, :], h)
    kvt_ref[r0:r0 + B_KV, :] = _rope_t(zt[:B_KV], ct, st, N_KV_B)
    kvt_ref[r0 + B_KV:r0 + KV_ROWS_B, :] = zt[B_KV:]
    for i in range(len(VPU_GROUPS)):
        r0 = KVT_ROWS + i * A_OUT
        zt = _dot_nt(wt_ref[r0:r0 + A_OUT, :], h)
        kvt_ref[r0:r0 + A_OUT, :] = _rope_t(zt, ct, st, HEADS_A) * SCALE


def _mix_in_sample(x, layer, g, w_q, w_kvt, nat_tabs, t_tabs, tm):
    n_tok, d = x.shape
    qw = w_q.shape[-1]
    tab_spec = pl.BlockSpec((tm, LANES), lambda i: (0, 0))
    ttab_spec = pl.BlockSpec((ROT_HALF, tm), lambda i: (0, 0))
    assert w_kvt.shape[1] == KVT_ROWS_Q
    return pl.pallas_call(
        _mix_in_sample_kernel,
        out_shape=[jax.ShapeDtypeStruct((n_tok, qw), _F32),
                   jax.ShapeDtypeStruct((KVT_ROWS_Q, n_tok), _F32)],
        grid=(n_tok // tm,),
        in_specs=[pl.BlockSpec((tm, d), lambda i: (i, 0))]
        + [_layer_block(a, layer) for a in (g, w_q, w_kvt)]
        + [tab_spec, tab_spec, tab_spec, ttab_spec, ttab_spec],
        out_specs=[pl.BlockSpec((tm, qw), lambda i: (i, 0)),
                   pl.BlockSpec((KVT_ROWS_Q, tm), lambda i: (0, i))],
        compiler_params=_cparams(1),
        name="mix_in_sample",
    )(x, g, w_q, w_kvt, *nat_tabs, *t_tabs)


def _lane_head(t, width):
    return lax.shift_right_logical(lax.broadcasted_iota(jnp.int32, (t, width), 1),
                                   HEAD_DIM.bit_length() - 1)


def _head_stack_a(q):
    lane_head = _lane_head(q.shape[0], A_OUT)
    return jnp.concatenate([jnp.where(lane_head == hh, q, 0.0) for hh in range(HEADS_A)], axis=0)


def _head_unstack_a(o_st, col_st, t):
    lane_head = _lane_head(t, A_OUT)
    out = jnp.zeros((t, A_OUT), _F32)
    for hh in range(HEADS_A):
        col = col_st[hh * t:(hh + 1) * t]
        blk = jnp.broadcast_to(col, (t, A_OUT)) if o_st is None else o_st[hh * t:(hh + 1) * t] * col
        out = jnp.where(lane_head == hh, blk, out)
    return out


def _head_stack_b(q):
    t = q.shape[0]
    lo = lax.broadcasted_iota(jnp.int32, (t, LANES), 1) < HEAD_DIM
    parts = []
    for c in range(B_Q // LANES):
        qc = q[:, c * LANES:(c + 1) * LANES]
        parts += [jnp.where(lo, qc, 0.0), jnp.where(lo, 0.0, qc)]
    return jnp.concatenate(parts, axis=0)


def _head_unstack_b(o_st, col_st, t):
    lo = lax.broadcasted_iota(jnp.int32, (t, LANES), 1) < HEAD_DIM
    cols = []
    for c in range(B_Q // LANES):
        r0 = 2 * c * t
        a = o_st[r0:r0 + t] * col_st[r0:r0 + t]
        b = o_st[r0 + t:r0 + 2 * t] * col_st[r0 + t:r0 + 2 * t]
        cols.append(jnp.where(lo, a, b))
    return jnp.concatenate(cols, axis=1)


def _sink_rows(sink_ref, t):
    return jnp.concatenate(
        [jnp.full((t, 1), sink_ref[p], _F32) for p in range(N_HEADS_B)], axis=0)


def _attn_prompt_kernel(sinkl_ref, q1_ref, k1_ref, v1_ref, q2_ref, k2_ref, v2_ref,
                        q3_ref, k3_ref, v3_ref, qb_ref, kb_ref, vb_ref,
                        oa_ref, ob_ref, acc_s, den_s, m_s, bias_s):
    seq = q1_ref.shape[0]
    n_blk = seq // BLK
    dil2, dil3 = DIL_PAIRS[1][1], DIL_PAIRS[2][1]
    per_class2 = seq // dil2 // BLK
    assert dil2 * per_class2 == dil3 == n_blk and seq // dil3 == BLK

    lane_head_a = _lane_head(BLK, A_OUT)
    head_a_bf = lane_head_a.astype(_F32).astype(_BF16)
    lane_head_b = _lane_head(BLK, LANES)
    head_b_bf = lane_head_b.astype(_F32).astype(_BF16)
    r_i = lax.broadcasted_iota(jnp.int32, (BLK, 2 * BLK), 0)
    c_i = lax.broadcasted_iota(jnp.int32, (BLK, 2 * BLK), 1)
    dist = BLK + r_i - c_i
    for i, n_back in enumerate((BLK, WINDOW_B - 1)):
        band = (dist >= 0) & (dist <= n_back)
        bias_s[2 * i] = jnp.where(band & (c_i >= BLK), 0.0, NEG)
        bias_s[2 * i + 1] = jnp.where(band, 0.0, NEG)
    causal = jnp.where(lax.broadcasted_iota(jnp.int32, (BLK, BLK), 0)
                       >= lax.broadcasted_iota(jnp.int32, (BLK, BLK), 1), 0.0, NEG)

    def with_prev(i, has_prev):
        return bias_s[2 * i + jnp.where(has_prev, 1, 0)]

    def window(ref, lead, r0):
        p0 = pl.multiple_of(jnp.maximum(r0 - BLK, 0), BLK)
        return jnp.concatenate([ref[lead + (pl.ds(p0, BLK),)], ref[lead + (pl.ds(r0, BLK),)]],
                               axis=0)

    def softmax_block(qs, kw, vw, bias, n_heads):
        n_keys = kw.shape[0]
        s = _dot_nt(qs, kw).reshape(n_heads, BLK, n_keys) + bias[None]
        m = jnp.max(s, axis=-1, keepdims=True)
        e = jnp.exp2(s - m)
        den = jnp.sum(e, axis=-1, keepdims=True)
        o = _dot(e.reshape(n_heads * BLK, n_keys).astype(_BF16), vw)
        return o, den.reshape(n_heads * BLK, 1), m.reshape(n_heads * BLK, 1)

    def block_a(q, kw, vw, bias):
        zero = jnp.zeros_like(q)
        qs = jnp.concatenate([jnp.where(head_a_bf == hh, q, zero) for hh in range(HEADS_A)], axis=0)
        o, den, m = softmax_block(qs, kw, vw, bias, HEADS_A)
        acc = o[:BLK]
        den_u = jnp.broadcast_to(den[:BLK], (BLK, A_OUT))
        m_u = jnp.broadcast_to(m[:BLK], (BLK, A_OUT))
        for hh in range(1, HEADS_A):
            sel = lane_head_a == hh
            rows = slice(hh * BLK, (hh + 1) * BLK)
            acc = jnp.where(sel, o[rows], acc)
            den_u = jnp.where(sel, jnp.broadcast_to(den[rows], (BLK, A_OUT)), den_u)
            m_u = jnp.where(sel, jnp.broadcast_to(m[rows], (BLK, A_OUT)), m_u)
        return acc, den_u, m_u

    def store(g, start, stride, acc, den_u, m_u):
        idx = pl.ds(start, BLK, stride=stride)
        for s in range(A_OUT // LANES):
            sl = slice(s * LANES, (s + 1) * LANES)
            acc_s[g, s, idx, :] = acc[:, sl]
            den_s[g, s, idx, :] = den_u[:, sl]
            m_s[g, s, idx, :] = m_u[:, sl]

    def classes_body(i, carry):
        r = lax.div(i, per_class2)
        sb = i - r * per_class2
        s0 = pl.multiple_of(sb * BLK, BLK)
        res = block_a(q2_ref[r, pl.ds(s0, BLK)], window(k2_ref, (r,), s0), window(v2_ref, (r,), s0),
                      with_prev(0, sb > 0))
        store(0, r + dil2 * s0, dil2, *res)
        res = block_a(q3_ref[i], k3_ref[i], v3_ref[i], causal)
        store(1, i, dil3, *res)
        return carry

    lax.fori_loop(0, n_blk, classes_body, 0, unroll=True)

    lo = lane_head_b == 0
    sink_lanes = sinkl_ref[...]

    def tokens_body(b, carry):
        r0 = pl.multiple_of(b * BLK, BLK)
        acc1, den1, m1 = block_a(q1_ref[pl.ds(r0, BLK)], window(k1_ref, (), r0),
                                 window(v1_ref, (), r0), with_prev(0, b > 0))
        cols = []
        for s in range(A_OUT // LANES):
            sl = slice(s * LANES, (s + 1) * LANES)
            rows = pl.ds(r0, BLK)
            ms = [m1[:, sl], m_s[0, s, rows, :], m_s[1, s, rows, :]]
            dens = [den1[:, sl], den_s[0, s, rows, :], den_s[1, s, rows, :]]
            accs = [acc1[:, sl], acc_s[0, s, rows, :], acc_s[1, s, rows, :]]
            top = jnp.maximum(jnp.maximum(ms[0], ms[1]), ms[2])
            ws = [jnp.exp2(mm - top) for mm in ms]
            num = ws[0] * accs[0] + ws[1] * accs[1] + ws[2] * accs[2]
            dn = ws[0] * dens[0] + ws[1] * dens[1] + ws[2] * dens[2]
            cols.append(num / dn)
        oa_ref[pl.ds(r0, BLK), :] = jnp.concatenate(cols, axis=1).astype(oa_ref.dtype)

        q = qb_ref[pl.ds(r0, BLK)]
        zero = jnp.zeros((BLK, LANES), q.dtype)
        parts = []
        for c in range(B_Q // LANES):
            qc = q[:, c * LANES:(c + 1) * LANES]
            parts += [jnp.where(head_b_bf == 0, qc, zero), jnp.where(head_b_bf == 1, qc, zero)]
        o, den, m = softmax_block(jnp.concatenate(parts, axis=0), window(kb_ref, (), r0),
                                  window(vb_ref, (), r0), with_prev(1, b > 0), N_HEADS_B)
        cols = []
        for c in range(B_Q // LANES):
            ra = slice(2 * c * BLK, (2 * c + 1) * BLK)
            rb = slice((2 * c + 1) * BLK, (2 * c + 2) * BLK)
            o_c = jnp.where(lo, o[ra], o[rb])
            den_c = jnp.where(lo, jnp.broadcast_to(den[ra], (BLK, LANES)),
                              jnp.broadcast_to(den[rb], (BLK, LANES)))
            m_c = jnp.where(lo, jnp.broadcast_to(m[ra], (BLK, LANES)),
                            jnp.broadcast_to(m[rb], (BLK, LANES)))
            sink_c = sink_lanes[:, c * LANES:(c + 1) * LANES]
            cols.append(o_c / (den_c + jnp.exp2(sink_c - m_c)))
        ob_ref[pl.ds(r0, BLK), :] = jnp.concatenate(cols, axis=1).astype(ob_ref.dtype)
        return carry

    lax.fori_loop(0, n_blk, tokens_body, 0, unroll=True)


def _attn_prompt(sinks_p, qkv, n_batch, seq):
    (q1, k1, v1, q2, k2, v2, q3, k3, v3, qb, kb, vb) = qkv
    tok = lambda w: pl.BlockSpec((seq, w), lambda n: (n, 0))
    cls = lambda dil: pl.BlockSpec((None, dil, seq // dil, A_OUT), lambda n: (n, 0, 0, 0))
    return pl.pallas_call(
        _attn_prompt_kernel,
        out_shape=[jax.ShapeDtypeStruct((n_batch * seq, A_OUT), _BF16),
                   jax.ShapeDtypeStruct((n_batch * seq, B_Q), _BF16)],
        grid=(n_batch,),
        in_specs=[_resident((1, B_Q))]
        + [tok(A_OUT)] * 3 + [cls(4)] * 3 + [cls(16)] * 3 + [tok(B_Q), tok(B_KV), tok(B_KV)],
        out_specs=[tok(A_OUT), tok(B_Q)],
        scratch_shapes=[pltpu.VMEM((N_GROUPS_A - 1, A_OUT // LANES, seq, LANES), _F32)] * 3
        + [pltpu.VMEM((4, BLK, 2 * BLK), _F32)],
        compiler_params=_cparams(1),
        name="attn_prompt",
    )(jnp.repeat(sinks_p * LOG2_E, HEAD_DIM)[None, :], q1, k1, v1, q2, k2, v2, q3, k3, v3,
      qb, kb, vb)


def _attn_sample_body(sink_ref, q_ref, kvt_ref, cache_refs, oa_ref, ob_ref, new_refs, n, t_new):
    c1_ref, c2_ref, c3_ref, cb_ref = cache_refs
    n1_ref, n2_ref, n3_ref, nb_ref = new_refs
    per_tile = LANES // t_new
    first_new = LANES - t_new
    shift = first_new - (n & (per_tile - 1)) * t_new
    new_t = pltpu.roll(kvt_ref[...], shift, 1)
    q = q_ref[...]

    def new_cache(old_ref, new_rows, out_ref):
        lb = old_ref.shape[1]
        rolled = pltpu.roll(old_ref[...], lb - t_new, 1)
        if lb > LANES:
            out_ref[:, :lb - LANES] = rolled[:, :lb - LANES]
        lane = lax.broadcasted_iota(jnp.int32, (old_ref.shape[0], LANES), 1)
        out_ref[:, lb - LANES:] = jnp.where(lane >= first_new, new_rows, rolled[:, lb - LANES:])

    def masks(rows, lb, window, dil, strict):
        t_old = lax.broadcasted_iota(jnp.int32, (rows, lb), 0) & (t_new - 1)
        dist = lb + t_old - lax.broadcasted_iota(jnp.int32, (rows, lb), 1)
        hi = (dist < window) if strict else (dist <= window)
        valid_old = (dist >= 0) & hi & ((dist & (dil - 1)) == 0)
        t_nw = lax.broadcasted_iota(jnp.int32, (rows, LANES), 0) & (t_new - 1)
        c_nw = lax.broadcasted_iota(jnp.int32, (rows, LANES), 1) - first_new
        dn = t_nw - c_nw
        hin = (dn < window) if strict else (dn <= window)
        valid_new = (c_nw >= 0) & (dn >= 0) & hin & ((dn & (dil - 1)) == 0)
        return valid_old, valid_new

    def attend(qs, k_old, v_old, k_new, v_new, valid_old, valid_new):
        k = jnp.concatenate([k_old, k_new], axis=1)
        v = jnp.concatenate([v_old, v_new], axis=1)
        s = jnp.where(jnp.concatenate([valid_old, valid_new], axis=1), _dot(qs, k), NEG)
        m = jnp.max(s, axis=-1, keepdims=True)
        e = jnp.exp(s - m)
        den = jnp.sum(e, axis=-1, keepdims=True)
        return _dot_nt(e.astype(_BF16), v), den, m

    def classes_reduce(x, dil, op):
        sh = dil
        while sh < LANES:
            x = op(x, pltpu.roll(x, sh, 1))
            sh *= 2
        return x

    def fold_tiles(x, op):
        acc = x[:, :LANES]
        for jt in range(1, x.shape[1] // LANES):
            acc = op(acc, x[:, jt * LANES:(jt + 1) * LANES])
        return acc

    def attend_lane_classes(old_ref, new_g, q_t, window, dil):
        lb = old_ref.shape[1]
        n_tiles = lb // LANES
        assert t_new <= dil and lb % dil == 0 and LANES % dil == 0 and lb <= window
        lane_q = lax.broadcasted_iota(jnp.int32, (A_OUT, LANES), 1)
        q_new = jnp.where(lane_q >= first_new, q_t, 0.0)
        q_cls = classes_reduce(pltpu.roll(q_new, t_new, 1), dil, jnp.add)
        lane_1 = lax.broadcasted_iota(jnp.int32, (1, LANES), 1)
        cls_ok = jnp.concatenate([(lane_1 & (dil - 1)) < t_new] * n_tiles, axis=1)
        o_rows, lse_rows = [], []
        for hh in range(HEADS_A):
            rows = slice(hh * HEAD_DIM, (hh + 1) * HEAD_DIM)
            q_h = jnp.concatenate([q_cls[rows]] * n_tiles, axis=1)
            s_old = jnp.sum(old_ref[rows, :] * q_h, axis=0, keepdims=True)
            s_old = jnp.where(cls_ok, s_old, NEG)
            s_new = jnp.sum(new_g[rows] * q_new[rows], axis=0, keepdims=True)
            s_new = jnp.where(lane_1 < t_new, pltpu.roll(s_new, t_new, 1), NEG)
            m = classes_reduce(jnp.maximum(fold_tiles(s_old, jnp.maximum), s_new), dil, jnp.maximum)
            e_old = jnp.where(cls_ok, jnp.exp(s_old - jnp.concatenate([m] * n_tiles, axis=1)), 0.0)
            e_new = jnp.where(lane_1 < t_new, jnp.exp(s_new - m), 0.0)
            den = classes_reduce(fold_tiles(e_old, jnp.add) + e_new, dil, jnp.add)
            den = jnp.where((lane_1 & (dil - 1)) < t_new, den, 1.0)
            v_rows = slice(A_OUT + hh * HEAD_DIM, A_OUT + (hh + 1) * HEAD_DIM)
            acc = fold_tiles(old_ref[v_rows, :] * e_old, jnp.add)
            acc = acc + pltpu.roll(new_g[v_rows], t_new, 1) * e_new
            acc = classes_reduce(acc, dil, jnp.add)
            o_rows.append(acc / den)
            lse_rows.append(jnp.broadcast_to(m + jnp.log(den), (HEAD_DIM, LANES)))
        out_t = jnp.concatenate(o_rows, axis=0)
        lse_t = jnp.concatenate(lse_rows, axis=0)
        return out_t.T[:t_new], lse_t.T[:t_new]

    outs, lses = [], []
    for g, (old_ref, out_ref) in enumerate(((c1_ref, n1_ref), (c2_ref, n2_ref), (c3_ref, n3_ref))):
        window, dil = DIL_PAIRS[g]
        lb = old_ref.shape[1]
        new_g = new_t[g * KV_ROWS_A:(g + 1) * KV_ROWS_A]
        if g in VPU_GROUPS:
            q_row0 = KVT_ROWS + VPU_GROUPS.index(g) * A_OUT
            out, lse = attend_lane_classes(old_ref, new_g, new_t[q_row0:q_row0 + A_OUT],
                                           window, dil)
            outs.append(out)
            lses.append(lse)
            new_cache(old_ref, new_g, out_ref)
            continue
        q0 = MXU_GROUPS.index(g) * A_OUT
        qs = _head_stack_a(q[:, q0:q0 + A_OUT]).astype(_BF16)
        valid_old, valid_new = masks(HEADS_A * t_new, lb, window, dil, False)
        o, den, m = attend(qs, old_ref[0:A_OUT, :].astype(_BF16), old_ref[A_OUT:, :].astype(_BF16),
                           new_g[:A_OUT].astype(_BF16), new_g[A_OUT:].astype(_BF16),
                           valid_old, valid_new)
        outs.append(_head_unstack_a(o, 1.0 / den, t_new))
        lses.append(_head_unstack_a(None, m + jnp.log(den), t_new))
        new_cache(old_ref, new_g, out_ref)
    m = jnp.maximum(jnp.maximum(lses[0], lses[1]), lses[2])
    ws = [jnp.exp(l - m) for l in lses]
    oa_ref[...] = (ws[0] * outs[0] + ws[1] * outs[1] + ws[2] * outs[2]) / (ws[0] + ws[1] + ws[2])

    lb = cb_ref.shape[1]
    new_b = new_t[N_GROUPS_A * KV_ROWS_A:KVT_ROWS]
    qs = _head_stack_b(q[:, len(MXU_GROUPS) * A_OUT:]).astype(_BF16)
    valid_old, valid_new = masks(N_HEADS_B * t_new, lb, WINDOW_B, 1, True)
    o, den, m = attend(qs, cb_ref[0:B_KV, :].astype(_BF16), cb_ref[B_KV:, :].astype(_BF16),
                       new_b[:B_KV].astype(_BF16), new_b[B_KV:].astype(_BF16),
                       valid_old, valid_new)
    scale = _sigmoid(m + jnp.log(den) - _sink_rows(sink_ref, t_new)) / den
    ob_ref[...] = _head_unstack_b(o, scale, t_new)
    new_cache(cb_ref, new_b, nb_ref)


def _mix_out_kernel(x_ref, g_ref, oa_ref, ob_ref, wga_ref, wgb_ref, wba_ref, wbb_ref, wout_ref,
                    o_ref):
    x = x_ref[...]
    h = _rms(x, g_ref[...]).astype(_BF16)
    gate_a = _dot(h, wga_ref[...])
    gate_b = _dot(h, wgb_ref[...])
    ya = _dot(oa_ref[...].astype(_BF16), wba_ref[...])
    yb = _dot(ob_ref[...].astype(_BF16), wbb_ref[...])
    merged = _sigmoid(gate_a) * ya + _sigmoid(gate_b) * yb
    o_ref[...] = x + _dot(merged.astype(_BF16), wout_ref[...])


def _mix_out(x, layer, g, oa, ob, w_in, w_ba, w_bb, w_out, tm):
    n_tok, d = x.shape
    assert QKV_COLS % d == 0 and w_in.shape[2] == QKV_COLS + 2 * d
    gate_specs = [_layer_block(w_in, layer, (d, d), (0, QKV_COLS // d + k)) for k in range(2)]
    tok = lambda w: pl.BlockSpec((tm, w), lambda i: (i, 0))
    return pl.pallas_call(
        _mix_out_kernel,
        out_shape=jax.ShapeDtypeStruct((n_tok, d), _F32),
        grid=(n_tok // tm,),
        in_specs=[tok(d), _layer_block(g, layer), tok(A_OUT), tok(B_Q)] + gate_specs
        + [_layer_block(a, layer) for a in (w_ba, w_bb, w_out)],
        out_specs=tok(d),
        compiler_params=_cparams(1),
        name="mix_out",
    )(x, g, oa, ob, w_in, w_in, w_ba, w_bb, w_out)


def _cache_view(c):
    dep, nb, rows = c.shape[:3]
    return jnp.transpose(c, (0, 1, 3, 4, 5, 2)).reshape(dep, nb, -1, rows)


def _cache_unview(c, heads):
    dep, nb, _, rows = c.shape
    return jnp.transpose(c.reshape(dep, nb, 2, heads, HEAD_DIM, rows), (0, 1, 5, 2, 3, 4))


def kernel(x_prompt, x_sample, cache_a1, cache_a2, cache_a3, cache_b, norm_ffn1, ffn1_gate, ffn1_up, ffn1_down, norm_mix, w_in, sinks, w_branch_a, w_branch_b, w_out, norm_ffn2, ffn2_gate, ffn2_up, ffn2_down, norm_final):
    n_batch, seq, d = x_prompt.shape
    dec_batch, t_new, _ = x_sample.shape
    depth = w_in.shape[0]
    n_sample = dec_batch * t_new
    tm_s = min(TM, n_sample)
    hosts_per_layer = 2
    n_hosted = dec_batch // hosts_per_layer
    tm_p = n_batch * seq // n_hosted
    assert seq == DIL_PAIRS[2][0] and LANES % t_new == 0 and n_sample % tm_s == 0
    assert tm_s % t_new == 0 and dec_batch % (LANES // t_new) == 0
    assert dec_batch % hosts_per_layer == 0 and (n_batch * seq) % n_hosted == 0 and tm_p % 8 == 0

    perm = jnp.array(B_HEAD_PERM)
    col_b = 3 * N_GROUPS_A * A_OUT

    cos_p, sin_p = _rope_tables(jnp.arange(seq))
    cos_s, sin_s = _rope_tables(PAST_LEN + jnp.arange(t_new))
    cos_s, sin_s = jnp.tile(cos_s, (tm_s // t_new, 1)), jnp.tile(sin_s, (tm_s // t_new, 1))
    nat_p, nat_s = _nat_tables(cos_p, sin_p), _nat_tables(cos_s, sin_s)
    t_p, t_s = (cos_p.T, sin_p.T), (cos_s.T, sin_s.T)

    caches = [_cache_view(c) for c in (cache_a1, cache_a2, cache_a3, cache_b)]

    bf = lambda w: w.astype(_BF16)
    w_in_bf = bf(w_in)
    cols = lambda lo, hi: w_in_bf[:, :, lo:hi]
    bq = cols(col_b, col_b + B_Q).reshape(depth, d, N_HEADS_B, HEAD_DIM)[:, :, perm]
    bq = bq.reshape(depth, d, B_Q)
    q_cols = [cols(3 * g * A_OUT, (3 * g + 1) * A_OUT) for g in range(N_GROUPS_A)]
    kv_cols = [cols((3 * g + 1) * A_OUT, (3 * g + 3) * A_OUT) for g in range(N_GROUPS_A)]
    kv_cols.append(cols(col_b + B_Q, QKV_COLS))
    w_qkv = jnp.concatenate([cols(0, col_b), bq, cols(col_b + B_Q, QKV_COLS)], axis=2)
    w_kvt = jnp.swapaxes(jnp.concatenate(kv_cols + [q_cols[g] for g in VPU_GROUPS], axis=2), 1, 2)
    w_q = jnp.concatenate([q_cols[g] for g in MXU_GROUPS] + [bq], axis=2)
    w_bb = bf(w_branch_b).reshape(depth, N_HEADS_B, HEAD_DIM, d)[:, perm].reshape(depth, B_Q, d)
    w_ba, w_o = bf(w_branch_a), bf(w_out)
    sinks_perm = sinks[:, perm]
    row = lambda v: v.reshape(-1, 1, d)
    f1 = (row(norm_ffn1), bf(ffn1_gate), bf(ffn1_up), bf(ffn1_down))
    f2 = (row(norm_ffn2), bf(ffn2_gate), bf(ffn2_up), bf(ffn2_down))
    g_mix = row(norm_mix)
    gf = norm_final.reshape(1, d)

    xp = x_prompt.reshape(n_batch * seq, d)
    xs = x_sample.reshape(dec_batch * t_new, d)
    prompt_bufs = []
    sample_bufs = None
    for l in range(depth):
        last = l == depth - 1
        sinks_p = sinks_perm[l]

        xs = _ffn(xs, l, *f1, gf, final_norm=False, tm=tm_s)[0]
        q_nat, kvt_new = _mix_in_sample(xs, l, g_mix, w_q, w_kvt, nat_s, t_s, tm_s)
        host = lambda n_base: (n_base, t_new, sinks_p, q_nat, kvt_new, caches, sample_bufs)

        xp, oa_0, ob_0, *sample_bufs = _ffn(xp, l, *f1, gf, final_norm=False, tm=tm_p,
                                            host=host(0))
        outs = _mix_in_prompt(xp, l, g_mix, w_qkv, w_kvt, nat_p, t_p, n_batch, seq, prompt_bufs)
        prompt_bufs = list(outs[12:])
        oa_p, ob_p = _attn_prompt(sinks_p, outs[:12], n_batch, seq)
        xp = _mix_out(xp, l, g_mix, oa_p, ob_p, w_in_bf, w_ba, w_bb, w_o, TM)
        xp, oa_1, ob_1, *sample_bufs = _ffn(xp, l, *f2, gf, final_norm=last, tm=tm_p,
                                            host=host(n_hosted))

        oa_s = jnp.concatenate([oa_0, oa_1], axis=0)
        ob_s = jnp.concatenate([ob_0, ob_1], axis=0)
        xs = _mix_out(xs, l, g_mix, oa_s, ob_s, w_in_bf, w_ba, w_bb, w_o, tm_s)
        xs = _ffn(xs, l, *f2, gf, final_norm=last, tm=tm_s)[0]

    heads = (HEADS_A, HEADS_A, HEADS_A, N_KV_B)
    new_p = [_cache_unview(prompt_bufs[i], heads[i]) for i in range(4)]
    new_s = [_cache_unview(sample_bufs[i], heads[i]) for i in range(4)]
    return (xp.reshape(n_batch, seq, d), xs.reshape(dec_batch, t_new, d), *new_p, *new_s)
```

```python
import functools

import jax
import jax.numpy as jnp
from jax import lax
from jax.experimental import pallas as pl
from jax.experimental.pallas import tpu as pltpu

PAST_LEN = 16384
HEAD_DIM = 64
ROT_DIM = HEAD_DIM // 4
ROT_HALF = ROT_DIM // 2
ROPE_THETA = 500000.0
DIL_PAIRS = ((128, 1), (512, 4), (2048, 16))
N_GROUPS_A = 3
HEADS_A = 4
A_OUT = HEADS_A * HEAD_DIM
N_HEADS_B = 8
N_KV_B = 2
WINDOW_B = 128
B_Q = N_HEADS_B * HEAD_DIM
B_KV = N_KV_B * HEAD_DIM
QKV_COLS = N_GROUPS_A * 3 * A_OUT + B_Q + 2 * B_KV
KV_ROWS_A = 2 * A_OUT
KV_ROWS_B = 2 * B_KV
KVT_ROWS = N_GROUPS_A * KV_ROWS_A + KV_ROWS_B
VPU_GROUPS = (2,)
MXU_GROUPS = tuple(g for g in range(N_GROUPS_A) if g not in VPU_GROUPS)
KVT_ROWS_Q = KVT_ROWS + len(VPU_GROUPS) * A_OUT
NORM_EPS = 1e-6
NEG = -1e30
SCALE = HEAD_DIM ** -0.5
LOG2_E = 1.4426950408889634
BLK = 128
LANES = 128
MXU_TILE = 256
TM = 512
FFN_CHUNKS = 1
B_HEAD_PERM = (0, 4, 1, 5, 2, 6, 3, 7)
V7X_SCOPED_VMEM_BYTES = 60000 * 1024

_BF16 = jnp.bfloat16
_F32 = jnp.float32


def _cparams(n_axes):
    return pltpu.CompilerParams(
        dimension_semantics=("arbitrary",) * n_axes,
        vmem_limit_bytes=V7X_SCOPED_VMEM_BYTES,
    )


def _resident(shape):
    nd = len(shape)
    return pl.BlockSpec(shape, lambda *_: (0,) * nd, pipeline_mode=pl.Buffered(1))


def _layer_block(arr, layer, block=None, index=None):
    block = tuple(arr.shape[1:]) if block is None else tuple(block)
    index = (0,) * len(block) if index is None else tuple(index)
    return pl.BlockSpec((None,) + block, lambda *_: (layer,) + index,
                        pipeline_mode=pl.Buffered(1))


def _rms(x, g):
    return x * lax.rsqrt(jnp.mean(x * x, axis=-1, keepdims=True) + NORM_EPS) * g


def _sigmoid(x):
    return 1.0 / (1.0 + jnp.exp(-x))


def _dot(a, b):
    return jnp.dot(a, b, preferred_element_type=_F32)


def _dot_nt(a, b):
    return lax.dot_general(a, b, (((1,), (1,)), ((), ())), preferred_element_type=_F32)


def _ffn_tile(x, g_ref, wg_ref, wu_ref, wd_ref, gf_ref, bounds, final_norm):
    h = _rms(x, g_ref[...]).astype(_BF16)
    acc = None
    for lo, hi in zip(bounds[:-1], bounds[1:]):
        gate = _dot(h, wg_ref[:, lo:hi])
        up = _dot(h, wu_ref[:, lo:hi])
        act = (gate * _sigmoid(gate) * up).astype(_BF16)
        part = _dot(act, wd_ref[lo:hi, :])
        acc = part if acc is None else acc + part
    y = x + 0.5 * acc
    return _rms(y, gf_ref[...]) if final_norm else y


def _ffn_kernel(x_ref, g_ref, wg_ref, wu_ref, wd_ref, gf_ref, sink_ref, q_ref, kvt_ref, *rest,
                bounds, final_norm, n_base, t_new):
    o_ref, oa_ref, ob_ref = rest[-7:-4]
    _attn_sample_body(sink_ref, q_ref, kvt_ref, rest[:4], oa_ref, ob_ref, rest[-4:],
                      n_base + pl.program_id(0), t_new)
    o_ref[...] = _ffn_tile(x_ref[...], g_ref, wg_ref, wu_ref, wd_ref, gf_ref, bounds, final_norm)


def _ffn_bounds(f):
    n_tiles = f // MXU_TILE
    cuts = [0, (n_tiles + 1) // 2 * MXU_TILE, f] if FFN_CHUNKS == 2 and n_tiles > 1 else [0, f]
    return tuple(cuts)


def _ffn_specs(ffn, layer, d):
    return [_layer_block(a, layer) for a in ffn[:4]] + [_resident((1, d))]


def _ffn(x, layer, ffn, *, final_norm, tm, n_base, t_new, sinks_p, q_nat, kvt_new, caches,
         prev_out):
    n_tok, d = x.shape
    steps = n_tok // tm
    per_tile = LANES // t_new
    tok_spec = pl.BlockSpec((tm, d), lambda i: (i, 0))
    cache_spec = lambda c: pl.BlockSpec((None, None) + c.shape[2:],
                                        lambda i: (layer, n_base + i, 0, 0))
    prev_out = [] if prev_out is None else list(prev_out)
    in_specs = ([tok_spec] + _ffn_specs(ffn, layer, d)
                + [pl.BlockSpec(memory_space=pltpu.SMEM),
                   pl.BlockSpec((t_new, q_nat.shape[1]), lambda i: (n_base + i, 0)),
                   pl.BlockSpec((KVT_ROWS_Q, LANES), lambda i: (0, (n_base + i) // per_tile))]
                + [cache_spec(c) for c in caches]
                + [pl.BlockSpec(memory_space=pl.ANY)] * len(prev_out))
    operands = [x, *ffn, sinks_p, q_nat, kvt_new, *caches, *prev_out]
    first_prev = len(operands) - len(prev_out)
    aliases = {first_prev + k: 3 + k for k in range(len(prev_out))}
    out_shape = ([jax.ShapeDtypeStruct((n_tok, d), _F32),
                  jax.ShapeDtypeStruct((steps * t_new, A_OUT), _F32),
                  jax.ShapeDtypeStruct((steps * t_new, B_Q), _F32)]
                 + [jax.ShapeDtypeStruct(c.shape, c.dtype) for c in caches])
    out_specs = ([tok_spec, pl.BlockSpec((t_new, A_OUT), lambda i: (i, 0)),
                  pl.BlockSpec((t_new, B_Q), lambda i: (i, 0))]
                 + [cache_spec(c) for c in caches])
    kern = functools.partial(_ffn_kernel, bounds=_ffn_bounds(ffn[1].shape[-1]),
                             final_norm=final_norm, n_base=n_base, t_new=t_new)
    return pl.pallas_call(
        kern,
        out_shape=out_shape,
        grid=(steps,),
        in_specs=in_specs,
        out_specs=out_specs,
        input_output_aliases=aliases,
        compiler_params=_cparams(1),
        name=("ffn_final" if final_norm else "ffn") + "_attn_sample",
    )(*operands)


def _rope_nat(x, c, s_lo, s_hi):
    return x * c + pltpu.roll(x, ROT_HALF, 1) * s_hi + pltpu.roll(x, LANES - ROT_HALF, 1) * s_lo


def _rope_t(k_t, cos_t, sin_t, n_heads):
    parts = []
    for hh in range(n_heads):
        b = hh * HEAD_DIM
        x1 = k_t[b:b + ROT_HALF]
        x2 = k_t[b + ROT_HALF:b + ROT_DIM]
        parts += [x1 * cos_t - x2 * sin_t, x2 * cos_t + x1 * sin_t, k_t[b + ROT_DIM:b + HEAD_DIM]]
    return jnp.concatenate(parts, axis=0)


def _rope_tables(pos):
    inv = ROPE_THETA ** (-(jnp.arange(ROT_HALF, dtype=_F32) * 2.0 / ROT_DIM))
    ang = pos.astype(_F32)[:, None] * inv[None, :]
    return jnp.cos(ang), jnp.sin(ang)


def _nat_tables(cos, sin):
    p = cos.shape[0]
    one = jnp.ones((p, HEAD_DIM - ROT_DIM), _F32)
    zero = jnp.zeros((p, HEAD_DIM - ROT_DIM), _F32)
    zh = jnp.zeros((p, ROT_HALF), _F32)
    c = jnp.concatenate([cos, cos, one], axis=1)
    s_hi = jnp.concatenate([zh, sin, zero], axis=1)
    s_lo = jnp.concatenate([-sin, zh, zero], axis=1)
    rep = LANES // HEAD_DIM
    return tuple(jnp.tile(t, (1, rep)) for t in (c, s_lo, s_hi))


def _mix_in_prompt_kernel(x_ref, g_ref, w_ref, wt_ref, c_ref, slo_ref, shi_ref, ct_ref, st_ref,
                          *rest, n_prev):
    prev = rest[:4] if n_prev else None
    (q1_ref, k1_ref, v1_ref, q2_ref, k2_ref, v2_ref, q3_ref, k3_ref, v3_ref,
     qb_ref, kb_ref, vb_ref, c1_ref, c2_ref, c3_ref, cb_ref, zs_ref) = rest[-17:]
    j = pl.program_id(1)
    tm = x_ref.shape[0]
    h = _rms(x_ref[...], g_ref[...]).astype(_BF16)
    c, s_lo, s_hi = c_ref[...], slo_ref[...], shi_ref[...]

    def proj(col0, width, kind):
        z = _dot(h, w_ref[:, col0:col0 + width])
        if kind == "v":
            return z
        slabs = []
        for s in range(width // LANES):
            zz = _rope_nat(z[:, s * LANES:(s + 1) * LANES], c, s_lo, s_hi)
            slabs.append(zz * (SCALE * LOG2_E) if kind == "q" else zz)
        return jnp.concatenate(slabs, axis=1)

    for i, (ref, kind) in enumerate(((q1_ref, "q"), (k1_ref, "k"), (v1_ref, "v"))):
        ref[...] = proj(i * A_OUT, A_OUT, kind).astype(_BF16)
    for g, refs in ((1, (q2_ref, k2_ref, v2_ref)), (2, (q3_ref, k3_ref, v3_ref))):
        dil = DIL_PAIRS[g][1]
        rows = tm // dil
        for i, (ref, kind) in enumerate(zip(refs, ("q", "k", "v"))):
            z = proj((3 * g + i) * A_OUT, A_OUT, kind)
            for s in range(A_OUT // LANES):
                zs_ref[s] = z[:, s * LANES:(s + 1) * LANES]
            for r in range(dil):
                for s in range(A_OUT // LANES):
                    ref[r, :, s * LANES:(s + 1) * LANES] = (
                        zs_ref[s, pl.ds(r, rows, stride=dil), :].astype(_BF16))
    col_b = 3 * N_GROUPS_A * A_OUT
    qb_ref[...] = proj(col_b, B_Q, "q").astype(_BF16)
    z = _dot(h, w_ref[:, col_b + B_Q:col_b + B_Q + 2 * B_KV])
    kb_ref[...] = _rope_nat(z[:, :B_KV], c, s_lo, s_hi).astype(_BF16)
    vb_ref[...] = z[:, B_KV:].astype(_BF16)

    def kv_t(row0, n_rows, h_tok, cos_t, sin_t, n_heads):
        z = _dot_nt(wt_ref[row0:row0 + n_rows, :], h_tok)
        half = n_rows // 2
        return jnp.concatenate([_rope_t(z[:half], cos_t, sin_t, n_heads), z[half:]], axis=0)

    def put(i, out_ref, new):
        if n_prev:
            out_ref[:n_prev] = prev[i][...]
        out_ref[n_prev] = new

    ct, st = ct_ref[...], st_ref[...]
    put(2, c3_ref, kv_t(2 * KV_ROWS_A, KV_ROWS_A, h, ct, st, HEADS_A))

    @pl.when(j == pl.num_programs(1) - 1)
    def _():
        put(1, c2_ref, kv_t(KV_ROWS_A, KV_ROWS_A, h, ct, st, HEADS_A))
        tail = tm - BLK
        put(0, c1_ref, kv_t(0, KV_ROWS_A, h[tail:], ct[:, tail:], st[:, tail:], HEADS_A))
        put(3, cb_ref, kv_t(3 * KV_ROWS_A, KV_ROWS_B, h[tail:], ct[:, tail:], st[:, tail:], N_KV_B))


def _mix_in_prompt(x, layer, g, w_qkv, w_kvt, nat_tabs, t_tabs, n_batch, seq, prev_bufs):
    d = x.shape[1]
    n_j = seq // TM
    n_prev = prev_bufs[0].shape[0] if prev_bufs else 0
    row = lambda n, j: (n * n_j + j, 0)
    tok_spec = lambda w: pl.BlockSpec((TM, w), row)
    cls_spec = lambda dil: pl.BlockSpec((None, dil, TM // dil, A_OUT), lambda n, j: (n, 0, j, 0))
    nat = lambda w: jax.ShapeDtypeStruct((n_batch * seq, w), _BF16)
    cls = lambda dil: jax.ShapeDtypeStruct((n_batch, dil, seq // dil, A_OUT), _BF16)
    keep = [min(w, seq) for w, _ in DIL_PAIRS]
    assert keep == [BLK, TM, seq] and seq % TM == 0
    bufs = ((KV_ROWS_A, keep[0], BLK, lambda j: 0), (KV_ROWS_A, keep[1], TM, lambda j: 0),
            (KV_ROWS_A, keep[2], TM, lambda j: j), (KV_ROWS_B, min(WINDOW_B, seq), BLK, lambda j: 0))

    def buf_spec(layers, rows, width, jmap):
        return pl.BlockSpec((layers, None, rows, width), lambda n, j: (0, n, 0, jmap(j)))

    def buf_shape(layers, rows, kept):
        return jax.ShapeDtypeStruct((layers, n_batch, rows, kept), _F32)

    n_out = n_prev + 1
    out_shape = ([nat(A_OUT)] * 3 + [cls(4)] * 3 + [cls(16)] * 3 + [nat(B_Q), nat(B_KV), nat(B_KV)]
                 + [buf_shape(n_out, r, k) for r, k, _, _ in bufs])
    out_specs = ([tok_spec(A_OUT)] * 3 + [cls_spec(4)] * 3 + [cls_spec(16)] * 3
                 + [tok_spec(B_Q), tok_spec(B_KV), tok_spec(B_KV)]
                 + [buf_spec(n_out, r, w, jm) for r, _, w, jm in bufs])
    prev_specs = [buf_spec(n_prev, r, w, jm) for r, _, w, jm in bufs] if n_prev else []
    tab_spec = pl.BlockSpec((TM, LANES), lambda n, j: (j, 0))
    ttab_spec = pl.BlockSpec((ROT_HALF, TM), lambda n, j: (0, j))
    return pl.pallas_call(
        functools.partial(_mix_in_prompt_kernel, n_prev=n_prev),
        out_shape=out_shape,
        grid=(n_batch, n_j),
        in_specs=[tok_spec(d)] + [_layer_block(a, layer) for a in (g, w_qkv, w_kvt)]
        + [tab_spec, tab_spec, tab_spec, ttab_spec, ttab_spec] + prev_specs,
        out_specs=out_specs,
        scratch_shapes=[pltpu.VMEM((A_OUT // LANES, TM, LANES), _F32)],
        compiler_params=_cparams(2),
        name="mix_in_prompt",
    )(x, g, w_qkv, w_kvt, *nat_tabs, *t_tabs, *prev_bufs)


def _sample_in_kernel(x_ref, gf1_ref, wg_ref, wu_ref, wd_ref, gf_ref,
                      g_ref, w_ref, wt_ref, c_ref, slo_ref, shi_ref, ct_ref, st_ref,
                      o_ref, q_ref, kvt_ref, *, bounds):
    x = _ffn_tile(x_ref[...], gf1_ref, wg_ref, wu_ref, wd_ref, gf_ref, bounds, False)
    o_ref[...] = x
    h = _rms(x, g_ref[...]).astype(_BF16)
    c, s_lo, s_hi = c_ref[...], slo_ref[...], shi_ref[...]
    z = _dot(h, w_ref[...])
    for s in range(z.shape[1] // LANES):
        sl = slice(s * LANES, (s + 1) * LANES)
        q_ref[:, sl] = _rope_nat(z[:, sl], c, s_lo, s_hi) * SCALE
    ct, st = ct_ref[...], st_ref[...]
    for g in range(N_GROUPS_A):
        r0 = g * KV_ROWS_A
        zt = _dot_nt(wt_ref[r0:r0 + KV_ROWS_A, :], h)
        kvt_ref[r0:r0 + A_OUT, :] = _rope_t(zt[:A_OUT], ct, st, HEADS_A)
        kvt_ref[r0 + A_OUT:r0 + KV_ROWS_A, :] = zt[A_OUT:]
    r0 = N_GROUPS_A * KV_ROWS_A
    zt = _dot_nt(wt_ref[r0:r0 + KV_ROWS_B, :], h)
    kvt_ref[r0:r0 + B_KV, :] = _rope_t(zt[:B_KV], ct, st, N_KV_B)
    kvt_ref[r0 + B_KV:r0 + KV_ROWS_B, :] = zt[B_KV:]
    for i in range(len(VPU_GROUPS)):
        r0 = KVT_ROWS + i * A_OUT
        zt = _dot_nt(wt_ref[r0:r0 + A_OUT, :], h)
        kvt_ref[r0:r0 + A_OUT, :] = _rope_t(zt, ct, st, HEADS_A) * SCALE


def _sample_in(x, layer, ffn, g, w_q, w_kvt, nat_tabs, t_tabs, tm):
    n_tok, d = x.shape
    qw = w_q.shape[-1]
    tok_spec = pl.BlockSpec((tm, d), lambda i: (i, 0))
    tab_spec = pl.BlockSpec((tm, LANES), lambda i: (0, 0))
    ttab_spec = pl.BlockSpec((ROT_HALF, tm), lambda i: (0, 0))
    assert w_kvt.shape[1] == KVT_ROWS_Q
    return pl.pallas_call(
        functools.partial(_sample_in_kernel, bounds=_ffn_bounds(ffn[1].shape[-1])),
        out_shape=[jax.ShapeDtypeStruct((n_tok, d), _F32),
                   jax.ShapeDtypeStruct((n_tok, qw), _F32),
                   jax.ShapeDtypeStruct((KVT_ROWS_Q, n_tok), _F32)],
        grid=(n_tok // tm,),
        in_specs=[tok_spec] + _ffn_specs(ffn, layer, d)
        + [_layer_block(a, layer) for a in (g, w_q, w_kvt)]
        + [tab_spec, tab_spec, tab_spec, ttab_spec, ttab_spec],
        out_specs=[tok_spec, pl.BlockSpec((tm, qw), lambda i: (i, 0)),
                   pl.BlockSpec((KVT_ROWS_Q, tm), lambda i: (0, i))],
        compiler_params=_cparams(1),
        name="sample_in",
    )(x, *ffn, g, w_q, w_kvt, *nat_tabs, *t_tabs)


def _lane_head(t, width):
    return lax.shift_right_logical(lax.broadcasted_iota(jnp.int32, (t, width), 1),
                                   HEAD_DIM.bit_length() - 1)


def _head_stack_a(q):
    lane_head = _lane_head(q.shape[0], A_OUT)
    return jnp.concatenate([jnp.where(lane_head == hh, q, 0.0) for hh in range(HEADS_A)], axis=0)


def _head_unstack_a(o_st, col_st, t):
    lane_head = _lane_head(t, A_OUT)
    out = jnp.zeros((t, A_OUT), _F32)
    for hh in range(HEADS_A):
        col = col_st[hh * t:(hh + 1) * t]
        blk = jnp.broadcast_to(col, (t, A_OUT)) if o_st is None else o_st[hh * t:(hh + 1) * t] * col
        out = jnp.where(lane_head == hh, blk, out)
    return out


def _head_stack_b(q):
    t = q.shape[0]
    lo = lax.broadcasted_iota(jnp.int32, (t, LANES), 1) < HEAD_DIM
    parts = []
    for c in range(B_Q // LANES):
        qc = q[:, c * LANES:(c + 1) * LANES]
        parts += [jnp.where(lo, qc, 0.0), jnp.where(lo, 0.0, qc)]
    return jnp.concatenate(parts, axis=0)


def _head_unstack_b(o_st, col_st, t):
    lo = lax.broadcasted_iota(jnp.int32, (t, LANES), 1) < HEAD_DIM
    cols = []
    for c in range(B_Q // LANES):
        r0 = 2 * c * t
        a = o_st[r0:r0 + t] * col_st[r0:r0 + t]
        b = o_st[r0 + t:r0 + 2 * t] * col_st[r0 + t:r0 + 2 * t]
        cols.append(jnp.where(lo, a, b))
    return jnp.concatenate(cols, axis=1)


def _sink_rows(sink_ref, t):
    return jnp.concatenate(
        [jnp.full((t, 1), sink_ref[p], _F32) for p in range(N_HEADS_B)], axis=0)


def _attn_prompt_kernel(sinkl_ref, q1_ref, k1_ref, v1_ref, q2_ref, k2_ref, v2_ref,
                        q3_ref, k3_ref, v3_ref, qb_ref, kb_ref, vb_ref,
                        oa_ref, ob_ref, acc_s, den_s, m_s, bias_s):
    seq = q1_ref.shape[0]
    n_blk = seq // BLK
    dil2, dil3 = DIL_PAIRS[1][1], DIL_PAIRS[2][1]
    per_class2 = seq // dil2 // BLK
    assert dil2 * per_class2 == dil3 == n_blk and seq // dil3 == BLK

    lane_head_a = _lane_head(BLK, A_OUT)
    head_a_bf = lane_head_a.astype(_F32).astype(_BF16)
    lane_head_b = _lane_head(BLK, LANES)
    head_b_bf = lane_head_b.astype(_F32).astype(_BF16)
    r_i = lax.broadcasted_iota(jnp.int32, (BLK, 2 * BLK), 0)
    c_i = lax.broadcasted_iota(jnp.int32, (BLK, 2 * BLK), 1)
    dist = BLK + r_i - c_i
    for i, n_back in enumerate((BLK, WINDOW_B - 1)):
        band = (dist >= 0) & (dist <= n_back)
        bias_s[2 * i] = jnp.where(band & (c_i >= BLK), 0.0, NEG)
        bias_s[2 * i + 1] = jnp.where(band, 0.0, NEG)
    causal = jnp.where(lax.broadcasted_iota(jnp.int32, (BLK, BLK), 0)
                       >= lax.broadcasted_iota(jnp.int32, (BLK, BLK), 1), 0.0, NEG)

    def with_prev(i, has_prev):
        return bias_s[2 * i + jnp.where(has_prev, 1, 0)]

    def window(ref, lead, r0):
        p0 = pl.multiple_of(jnp.maximum(r0 - BLK, 0), BLK)
        return jnp.concatenate([ref[lead + (pl.ds(p0, BLK),)], ref[lead + (pl.ds(r0, BLK),)]],
                               axis=0)

    def softmax_block(qs, kw, vw, bias, n_heads):
        n_keys = kw.shape[0]
        s = _dot_nt(qs, kw).reshape(n_heads, BLK, n_keys) + bias[None]
        m = jnp.max(s, axis=-1, keepdims=True)
        e = jnp.exp2(s - m)
        den = jnp.sum(e, axis=-1, keepdims=True)
        o = _dot(e.reshape(n_heads * BLK, n_keys).astype(_BF16), vw)
        return o, den.reshape(n_heads * BLK, 1), m.reshape(n_heads * BLK, 1)

    def block_a(q, kw, vw, bias):
        zero = jnp.zeros_like(q)
        qs = jnp.concatenate([jnp.where(head_a_bf == hh, q, zero) for hh in range(HEADS_A)], axis=0)
        o, den, m = softmax_block(qs, kw, vw, bias, HEADS_A)
        acc = o[:BLK]
        den_u = jnp.broadcast_to(den[:BLK], (BLK, A_OUT))
        m_u = jnp.broadcast_to(m[:BLK], (BLK, A_OUT))
        for hh in range(1, HEADS_A):
            sel = lane_head_a == hh
            rows = slice(hh * BLK, (hh + 1) * BLK)
            acc = jnp.where(sel, o[rows], acc)
            den_u = jnp.where(sel, jnp.broadcast_to(den[rows], (BLK, A_OUT)), den_u)
            m_u = jnp.where(sel, jnp.broadcast_to(m[rows], (BLK, A_OUT)), m_u)
        return acc, den_u, m_u

    def store(g, start, stride, acc, den_u, m_u):
        idx = pl.ds(start, BLK, stride=stride)
        for s in range(A_OUT // LANES):
            sl = slice(s * LANES, (s + 1) * LANES)
            acc_s[g, s, idx, :] = acc[:, sl]
            den_s[g, s, idx, :] = den_u[:, sl]
            m_s[g, s, idx, :] = m_u[:, sl]

    def classes_body(i, carry):
        r = lax.div(i, per_class2)
        sb = i - r * per_class2
        s0 = pl.multiple_of(sb * BLK, BLK)
        res = block_a(q2_ref[r, pl.ds(s0, BLK)], window(k2_ref, (r,), s0), window(v2_ref, (r,), s0),
                      with_prev(0, sb > 0))
        store(0, r + dil2 * s0, dil2, *res)
        res = block_a(q3_ref[i], k3_ref[i], v3_ref[i], causal)
        store(1, i, dil3, *res)
        return carry

    lax.fori_loop(0, n_blk, classes_body, 0, unroll=True)

    lo = lane_head_b == 0
    sink_lanes = sinkl_ref[...]

    def tokens_body(b, carry):
        r0 = pl.multiple_of(b * BLK, BLK)
        acc1, den1, m1 = block_a(q1_ref[pl.ds(r0, BLK)], window(k1_ref, (), r0),
                                 window(v1_ref, (), r0), with_prev(0, b > 0))
        cols = []
        for s in range(A_OUT // LANES):
            sl = slice(s * LANES, (s + 1) * LANES)
            rows = pl.ds(r0, BLK)
            ms = [m1[:, sl], m_s[0, s, rows, :], m_s[1, s, rows, :]]
            dens = [den1[:, sl], den_s[0, s, rows, :], den_s[1, s, rows, :]]
            accs = [acc1[:, sl], acc_s[0, s, rows, :], acc_s[1, s, rows, :]]
            top = jnp.maximum(jnp.maximum(ms[0], ms[1]), ms[2])
            ws = [jnp.exp2(mm - top) for mm in ms]
            num = ws[0] * accs[0] + ws[1] * accs[1] + ws[2] * accs[2]
            dn = ws[0] * dens[0] + ws[1] * dens[1] + ws[2] * dens[2]
            cols.append(num / dn)
        oa_ref[pl.ds(r0, BLK), :] = jnp.concatenate(cols, axis=1).astype(oa_ref.dtype)

        q = qb_ref[pl.ds(r0, BLK)]
        zero = jnp.zeros((BLK, LANES), q.dtype)
        parts = []
        for c in range(B_Q // LANES):
            qc = q[:, c * LANES:(c + 1) * LANES]
            parts += [jnp.where(head_b_bf == 0, qc, zero), jnp.where(head_b_bf == 1, qc, zero)]
        o, den, m = softmax_block(jnp.concatenate(parts, axis=0), window(kb_ref, (), r0),
                                  window(vb_ref, (), r0), with_prev(1, b > 0), N_HEADS_B)
        cols = []
        for c in range(B_Q // LANES):
            ra = slice(2 * c * BLK, (2 * c + 1) * BLK)
            rb = slice((2 * c + 1) * BLK, (2 * c + 2) * BLK)
            o_c = jnp.where(lo, o[ra], o[rb])
            den_c = jnp.where(lo, jnp.broadcast_to(den[ra], (BLK, LANES)),
                              jnp.broadcast_to(den[rb], (BLK, LANES)))
            m_c = jnp.where(lo, jnp.broadcast_to(m[ra], (BLK, LANES)),
                            jnp.broadcast_to(m[rb], (BLK, LANES)))
            sink_c = sink_lanes[:, c * LANES:(c + 1) * LANES]
            cols.append(o_c / (den_c + jnp.exp2(sink_c - m_c)))
        ob_ref[pl.ds(r0, BLK), :] = jnp.concatenate(cols, axis=1).astype(ob_ref.dtype)
        return carry

    lax.fori_loop(0, n_blk, tokens_body, 0, unroll=True)


def _attn_prompt(sinks_p, qkv, n_batch, seq):
    (q1, k1, v1, q2, k2, v2, q3, k3, v3, qb, kb, vb) = qkv
    tok = lambda w: pl.BlockSpec((seq, w), lambda n: (n, 0))
    cls = lambda dil: pl.BlockSpec((None, dil, seq // dil, A_OUT), lambda n: (n, 0, 0, 0))
    return pl.pallas_call(
        _attn_prompt_kernel,
        out_shape=[jax.ShapeDtypeStruct((n_batch * seq, A_OUT), _BF16),
                   jax.ShapeDtypeStruct((n_batch * seq, B_Q), _BF16)],
        grid=(n_batch,),
        in_specs=[_resident((1, B_Q))]
        + [tok(A_OUT)] * 3 + [cls(4)] * 3 + [cls(16)] * 3 + [tok(B_Q), tok(B_KV), tok(B_KV)],
        out_specs=[tok(A_OUT), tok(B_Q)],
        scratch_shapes=[pltpu.VMEM((N_GROUPS_A - 1, A_OUT // LANES, seq, LANES), _F32)] * 3
        + [pltpu.VMEM((4, BLK, 2 * BLK), _F32)],
        compiler_params=_cparams(1),
        name="attn_prompt",
    )(jnp.repeat(sinks_p * LOG2_E, HEAD_DIM)[None, :], q1, k1, v1, q2, k2, v2, q3, k3, v3,
      qb, kb, vb)


def _attn_sample_body(sink_ref, q_ref, kvt_ref, cache_refs, oa_ref, ob_ref, new_refs, n, t_new):
    c1_ref, c2_ref, c3_ref, cb_ref = cache_refs
    n1_ref, n2_ref, n3_ref, nb_ref = new_refs
    per_tile = LANES // t_new
    first_new = LANES - t_new
    shift = first_new - (n & (per_tile - 1)) * t_new
    new_t = pltpu.roll(kvt_ref[...], shift, 1)
    q = q_ref[...]

    def new_cache(old_ref, new_rows, out_ref):
        lb = old_ref.shape[1]
        rolled = pltpu.roll(old_ref[...], lb - t_new, 1)
        if lb > LANES:
            out_ref[:, :lb - LANES] = rolled[:, :lb - LANES]
        lane = lax.broadcasted_iota(jnp.int32, (old_ref.shape[0], LANES), 1)
        out_ref[:, lb - LANES:] = jnp.where(lane >= first_new, new_rows, rolled[:, lb - LANES:])

    def masks(rows, lb, window, dil, strict):
        t_old = lax.broadcasted_iota(jnp.int32, (rows, lb), 0) & (t_new - 1)
        dist = lb + t_old - lax.broadcasted_iota(jnp.int32, (rows, lb), 1)
        hi = (dist < window) if strict else (dist <= window)
        valid_old = (dist >= 0) & hi & ((dist & (dil - 1)) == 0)
        t_nw = lax.broadcasted_iota(jnp.int32, (rows, LANES), 0) & (t_new - 1)
        c_nw = lax.broadcasted_iota(jnp.int32, (rows, LANES), 1) - first_new
        dn = t_nw - c_nw
        hin = (dn < window) if strict else (dn <= window)
        valid_new = (c_nw >= 0) & (dn >= 0) & hin & ((dn & (dil - 1)) == 0)
        return valid_old, valid_new

    def attend(qs, k_old, v_old, k_new, v_new, valid_old, valid_new):
        k = jnp.concatenate([k_old, k_new], axis=1)
        v = jnp.concatenate([v_old, v_new], axis=1)
        s = jnp.where(jnp.concatenate([valid_old, valid_new], axis=1), _dot(qs, k), NEG)
        m = jnp.max(s, axis=-1, keepdims=True)
        e = jnp.exp(s - m)
        den = jnp.sum(e, axis=-1, keepdims=True)
        return _dot_nt(e.astype(_BF16), v), den, m

    def classes_reduce(x, dil, op):
        sh = dil
        while sh < LANES:
            x = op(x, pltpu.roll(x, sh, 1))
            sh *= 2
        return x

    def fold_tiles(x, op):
        acc = x[:, :LANES]
        for jt in range(1, x.shape[1] // LANES):
            acc = op(acc, x[:, jt * LANES:(jt + 1) * LANES])
        return acc

    def attend_lane_classes(old_ref, new_g, q_t, window, dil):
        lb = old_ref.shape[1]
        n_tiles = lb // LANES
        assert t_new <= dil and lb % dil == 0 and LANES % dil == 0 and lb <= window
        lane_q = lax.broadcasted_iota(jnp.int32, (A_OUT, LANES), 1)
        q_new = jnp.where(lane_q >= first_new, q_t, 0.0)
        q_cls = classes_reduce(pltpu.roll(q_new, t_new, 1), dil, jnp.add)
        lane_1 = lax.broadcasted_iota(jnp.int32, (1, LANES), 1)
        cls_ok = jnp.concatenate([(lane_1 & (dil - 1)) < t_new] * n_tiles, axis=1)
        o_rows, lse_rows = [], []
        for hh in range(HEADS_A):
            rows = slice(hh * HEAD_DIM, (hh + 1) * HEAD_DIM)
            q_h = jnp.concatenate([q_cls[rows]] * n_tiles, axis=1)
            s_old = jnp.sum(old_ref[rows, :] * q_h, axis=0, keepdims=True)
            s_old = jnp.where(cls_ok, s_old, NEG)
            s_new = jnp.sum(new_g[rows] * q_new[rows], axis=0, keepdims=True)
            s_new = jnp.where(lane_1 < t_new, pltpu.roll(s_new, t_new, 1), NEG)
            m = classes_reduce(jnp.maximum(fold_tiles(s_old, jnp.maximum), s_new), dil, jnp.maximum)
            e_old = jnp.where(cls_ok, jnp.exp(s_old - jnp.concatenate([m] * n_tiles, axis=1)), 0.0)
            e_new = jnp.where(lane_1 < t_new, jnp.exp(s_new - m), 0.0)
            den = classes_reduce(fold_tiles(e_old, jnp.add) + e_new, dil, jnp.add)
            den = jnp.where((lane_1 & (dil - 1)) < t_new, den, 1.0)
            v_rows = slice(A_OUT + hh * HEAD_DIM, A_OUT + (hh + 1) * HEAD_DIM)
            acc = fold_tiles(old_ref[v_rows, :] * e_old, jnp.add)
            acc = acc + pltpu.roll(new_g[v_rows], t_new, 1) * e_new
            acc = classes_reduce(acc, dil, jnp.add)
            o_rows.append(acc / den)
            lse_rows.append(jnp.broadcast_to(m + jnp.log(den), (HEAD_DIM, LANES)))
        out_t = jnp.concatenate(o_rows, axis=0)
        lse_t = jnp.concatenate(lse_rows, axis=0)
        return out_t.T[:t_new], lse_t.T[:t_new]

    outs, lses = [], []
    for g, (old_ref, out_ref) in enumerate(((c1_ref, n1_ref), (c2_ref, n2_ref), (c3_ref, n3_ref))):
        window, dil = DIL_PAIRS[g]
        lb = old_ref.shape[1]
        new_g = new_t[g * KV_ROWS_A:(g + 1) * KV_ROWS_A]
        if g in VPU_GROUPS:
            q_row0 = KVT_ROWS + VPU_GROUPS.index(g) * A_OUT
            out, lse = attend_lane_classes(old_ref, new_g, new_t[q_row0:q_row0 + A_OUT],
                                           window, dil)
            outs.append(out)
            lses.append(lse)
            new_cache(old_ref, new_g, out_ref)
            continue
        q0 = MXU_GROUPS.index(g) * A_OUT
        qs = _head_stack_a(q[:, q0:q0 + A_OUT]).astype(_BF16)
        valid_old, valid_new = masks(HEADS_A * t_new, lb, window, dil, False)
        o, den, m = attend(qs, old_ref[0:A_OUT, :].astype(_BF16), old_ref[A_OUT:, :].astype(_BF16),
                           new_g[:A_OUT].astype(_BF16), new_g[A_OUT:].astype(_BF16),
                           valid_old, valid_new)
        outs.append(_head_unstack_a(o, 1.0 / den, t_new))
        lses.append(_head_unstack_a(None, m + jnp.log(den), t_new))
        new_cache(old_ref, new_g, out_ref)
    m = jnp.maximum(jnp.maximum(lses[0], lses[1]), lses[2])
    ws = [jnp.exp(l - m) for l in lses]
    oa_ref[...] = (ws[0] * outs[0] + ws[1] * outs[1] + ws[2] * outs[2]) / (ws[0] + ws[1] + ws[2])

    lb = cb_ref.shape[1]
    new_b = new_t[N_GROUPS_A * KV_ROWS_A:KVT_ROWS]
    qs = _head_stack_b(q[:, len(MXU_GROUPS) * A_OUT:]).astype(_BF16)
    valid_old, valid_new = masks(N_HEADS_B * t_new, lb, WINDOW_B, 1, True)
    o, den, m = attend(qs, cb_ref[0:B_KV, :].astype(_BF16), cb_ref[B_KV:, :].astype(_BF16),
                       new_b[:B_KV].astype(_BF16), new_b[B_KV:].astype(_BF16),
                       valid_old, valid_new)
    scale = _sigmoid(m + jnp.log(den) - _sink_rows(sink_ref, t_new)) / den
    ob_ref[...] = _head_unstack_b(o, scale, t_new)
    new_cache(cb_ref, new_b, nb_ref)


def _mix_out_tile(x, g_ref, oa_ref, ob_ref, wga_ref, wgb_ref, wba_ref, wbb_ref, wout_ref):
    h = _rms(x, g_ref[...]).astype(_BF16)
    gate_a = _dot(h, wga_ref[...])
    gate_b = _dot(h, wgb_ref[...])
    ya = _dot(oa_ref[...].astype(_BF16), wba_ref[...])
    yb = _dot(ob_ref[...].astype(_BF16), wbb_ref[...])
    merged = _sigmoid(gate_a) * ya + _sigmoid(gate_b) * yb
    return x + _dot(merged.astype(_BF16), wout_ref[...])


def _mix_out_kernel(x_ref, *refs, n_mix, bounds, final_norm):
    y = _mix_out_tile(x_ref[...], *refs[:n_mix])
    if bounds is not None:
        y = _ffn_tile(y, *refs[n_mix:n_mix + 5], bounds, final_norm)
    refs[-1][...] = y


def _mix_out(x, layer, g, oa, ob, w_in, w_ba, w_bb, w_out, tm, ffn=None, final_norm=False):
    n_tok, d = x.shape
    assert QKV_COLS % d == 0 and w_in.shape[2] == QKV_COLS + 2 * d
    gate_specs = [_layer_block(w_in, layer, (d, d), (0, QKV_COLS // d + k)) for k in range(2)]
    tok = lambda w: pl.BlockSpec((tm, w), lambda i: (i, 0))
    mix_specs = ([_layer_block(g, layer), tok(A_OUT), tok(B_Q)] + gate_specs
                 + [_layer_block(a, layer) for a in (w_ba, w_bb, w_out)])
    mix_ops = [g, oa, ob, w_in, w_in, w_ba, w_bb, w_out]
    ffn_specs, ffn_ops, bounds = [], [], None
    if ffn is not None:
        ffn_specs, ffn_ops, bounds = _ffn_specs(ffn, layer, d), list(ffn), _ffn_bounds(ffn[1].shape[-1])
    return pl.pallas_call(
        functools.partial(_mix_out_kernel, n_mix=len(mix_ops), bounds=bounds,
                          final_norm=final_norm),
        out_shape=jax.ShapeDtypeStruct((n_tok, d), _F32),
        grid=(n_tok // tm,),
        in_specs=[tok(d)] + mix_specs + ffn_specs,
        out_specs=tok(d),
        compiler_params=_cparams(1),
        name="mix_out" if ffn is None else "sample_out",
    )(x, *mix_ops, *ffn_ops)


def _cache_view(c):
    dep, nb, rows = c.shape[:3]
    return jnp.transpose(c, (0, 1, 3, 4, 5, 2)).reshape(dep, nb, -1, rows)


def _cache_unview(c, heads):
    dep, nb, _, rows = c.shape
    return jnp.transpose(c.reshape(dep, nb, 2, heads, HEAD_DIM, rows), (0, 1, 5, 2, 3, 4))


def kernel(x_prompt, x_sample, cache_a1, cache_a2, cache_a3, cache_b, norm_ffn1, ffn1_gate, ffn1_up, ffn1_down, norm_mix, w_in, sinks, w_branch_a, w_branch_b, w_out, norm_ffn2, ffn2_gate, ffn2_up, ffn2_down, norm_final):
    n_batch, seq, d = x_prompt.shape
    dec_batch, t_new, _ = x_sample.shape
    depth = w_in.shape[0]
    n_sample = dec_batch * t_new
    tm_s = min(TM, n_sample)
    hosts_per_layer = 2
    n_hosted = dec_batch // hosts_per_layer
    tm_p = n_batch * seq // n_hosted
    assert seq == DIL_PAIRS[2][0] and LANES % t_new == 0 and n_sample % tm_s == 0
    assert tm_s % t_new == 0 and dec_batch % (LANES // t_new) == 0
    assert dec_batch % hosts_per_layer == 0 and (n_batch * seq) % n_hosted == 0 and tm_p % 8 == 0

    perm = jnp.array(B_HEAD_PERM)
    col_b = 3 * N_GROUPS_A * A_OUT

    cos_p, sin_p = _rope_tables(jnp.arange(seq))
    cos_s, sin_s = _rope_tables(PAST_LEN + jnp.arange(t_new))
    cos_s, sin_s = jnp.tile(cos_s, (tm_s // t_new, 1)), jnp.tile(sin_s, (tm_s // t_new, 1))
    nat_p, nat_s = _nat_tables(cos_p, sin_p), _nat_tables(cos_s, sin_s)
    t_p, t_s = (cos_p.T, sin_p.T), (cos_s.T, sin_s.T)

    caches = [_cache_view(c) for c in (cache_a1, cache_a2, cache_a3, cache_b)]

    bf = lambda w: w.astype(_BF16)
    w_in_bf = bf(w_in)
    cols = lambda lo, hi: w_in_bf[:, :, lo:hi]
    bq = cols(col_b, col_b + B_Q).reshape(depth, d, N_HEADS_B, HEAD_DIM)[:, :, perm]
    bq = bq.reshape(depth, d, B_Q)
    q_cols = [cols(3 * g * A_OUT, (3 * g + 1) * A_OUT) for g in range(N_GROUPS_A)]
    kv_cols = [cols((3 * g + 1) * A_OUT, (3 * g + 3) * A_OUT) for g in range(N_GROUPS_A)]
    kv_cols.append(cols(col_b + B_Q, QKV_COLS))
    w_qkv = jnp.concatenate([cols(0, col_b), bq, cols(col_b + B_Q, QKV_COLS)], axis=2)
    w_kvt = jnp.swapaxes(jnp.concatenate(kv_cols + [q_cols[g] for g in VPU_GROUPS], axis=2), 1, 2)
    w_q = jnp.concatenate([q_cols[g] for g in MXU_GROUPS] + [bq], axis=2)
    w_bb = bf(w_branch_b).reshape(depth, N_HEADS_B, HEAD_DIM, d)[:, perm].reshape(depth, B_Q, d)
    w_ba, w_o = bf(w_branch_a), bf(w_out)
    sinks_perm = sinks[:, perm]
    row = lambda v: v.reshape(-1, 1, d)
    gf = norm_final.reshape(1, d)
    f1 = (row(norm_ffn1), bf(ffn1_gate), bf(ffn1_up), bf(ffn1_down), gf)
    f2 = (row(norm_ffn2), bf(ffn2_gate), bf(ffn2_up), bf(ffn2_down), gf)
    g_mix = row(norm_mix)

    xp = x_prompt.reshape(n_batch * seq, d)
    xs = x_sample.reshape(dec_batch * t_new, d)
    prompt_bufs = []
    sample_bufs = None
    for l in range(depth):
        last = l == depth - 1
        sinks_p = sinks_perm[l]

        xs, q_nat, kvt_new = _sample_in(xs, l, f1, g_mix, w_q, w_kvt, nat_s, t_s, tm_s)
        host = lambda n_base: dict(n_base=n_base, t_new=t_new, sinks_p=sinks_p, q_nat=q_nat,
                                   kvt_new=kvt_new, caches=caches, prev_out=sample_bufs)

        xp, oa_0, ob_0, *sample_bufs = _ffn(xp, l, f1, final_norm=False, tm=tm_p, **host(0))
        outs = _mix_in_prompt(xp, l, g_mix, w_qkv, w_kvt, nat_p, t_p, n_batch, seq, prompt_bufs)
        prompt_bufs = list(outs[12:])
        oa_p, ob_p = _attn_prompt(sinks_p, outs[:12], n_batch, seq)
        xp = _mix_out(xp, l, g_mix, oa_p, ob_p, w_in_bf, w_ba, w_bb, w_o, TM)
        xp, oa_1, ob_1, *sample_bufs = _ffn(xp, l, f2, final_norm=last, tm=tm_p,
                                            **host(n_hosted))

        oa_s = jnp.concatenate([oa_0, oa_1], axis=0)
        ob_s = jnp.concatenate([ob_0, ob_1], axis=0)
        xs = _mix_out(xs, l, g_mix, oa_s, ob_s, w_in_bf, w_ba, w_bb, w_o, tm_s, ffn=f2,
                      final_norm=last)

    heads = (HEADS_A, HEADS_A, HEADS_A, N_KV_B)
    new_p = [_cache_unview(prompt_bufs[i], heads[i]) for i in range(4)]
    new_s = [_cache_unview(sample_bufs[i], heads[i]) for i in range(4)]
    return (xp.reshape(n_batch, seq, d), xs.reshape(dec_batch, t_new, d), *new_p, *new_s)
```

```python
import functools

import jax
import jax.numpy as jnp
from jax import lax
from jax.experimental import pallas as pl
from jax.experimental.pallas import tpu as pltpu

PAST_LEN = 16384
HEAD_DIM = 64
ROT_DIM = HEAD_DIM // 4
ROT_HALF = ROT_DIM // 2
ROPE_THETA = 500000.0
DIL_PAIRS = ((128, 1), (512, 4), (2048, 16))
N_GROUPS_A = 3
HEADS_A = 4
A_OUT = HEADS_A * HEAD_DIM
N_HEADS_B = 8
N_KV_B = 2
WINDOW_B = 128
B_Q = N_HEADS_B * HEAD_DIM
B_KV = N_KV_B * HEAD_DIM
QKV_COLS = N_GROUPS_A * 3 * A_OUT + B_Q + 2 * B_KV
KV_ROWS_A = 2 * A_OUT
KV_ROWS_B = 2 * B_KV
KVT_ROWS = N_GROUPS_A * KV_ROWS_A + KV_ROWS_B
VPU_GROUPS = (2,)
MXU_GROUPS = tuple(g for g in range(N_GROUPS_A) if g not in VPU_GROUPS)
KVT_ROWS_Q = KVT_ROWS + len(VPU_GROUPS) * A_OUT
NORM_EPS = 1e-6
NEG = -1e30
SCALE = HEAD_DIM ** -0.5
LOG2_E = 1.4426950408889634
BLK = 128
LANES = 128
MXU_TILE = 256
TM = 512
FFN_CHUNKS = 1
B_HEAD_PERM = (0, 4, 1, 5, 2, 6, 3, 7)
V7X_SCOPED_VMEM_BYTES = 60000 * 1024

_BF16 = jnp.bfloat16
_F32 = jnp.float32


def _cparams(n_axes):
    return pltpu.CompilerParams(
        dimension_semantics=("arbitrary",) * n_axes,
        vmem_limit_bytes=V7X_SCOPED_VMEM_BYTES,
    )


def _resident(shape):
    nd = len(shape)
    return pl.BlockSpec(shape, lambda *_: (0,) * nd, pipeline_mode=pl.Buffered(1))


def _layer_block(arr, layer, block=None, index=None):
    block = tuple(arr.shape[1:]) if block is None else tuple(block)
    index = (0,) * len(block) if index is None else tuple(index)
    return pl.BlockSpec((None,) + block, lambda *_: (layer,) + index,
                        pipeline_mode=pl.Buffered(1))


def _rms(x, g):
    return x * lax.rsqrt(jnp.mean(x * x, axis=-1, keepdims=True) + NORM_EPS) * g


def _sigmoid(x):
    return 1.0 / (1.0 + jnp.exp(-x))


def _dot(a, b):
    return jnp.dot(a, b, preferred_element_type=_F32)


def _dot_nt(a, b):
    return lax.dot_general(a, b, (((1,), (1,)), ((), ())), preferred_element_type=_F32)


def _ffn_tile(x, g_ref, wg_ref, wu_ref, wd_ref, gf_ref, bounds, final_norm):
    h = _rms(x, g_ref[...]).astype(_BF16)
    acc = None
    for lo, hi in zip(bounds[:-1], bounds[1:]):
        gate = _dot(h, wg_ref[:, lo:hi])
        up = _dot(h, wu_ref[:, lo:hi])
        act = (gate * _sigmoid(gate) * up).astype(_BF16)
        part = _dot(act, wd_ref[lo:hi, :])
        acc = part if acc is None else acc + part
    y = x + 0.5 * acc
    return _rms(y, gf_ref[...]) if final_norm else y


def _ffn_kernel(x_ref, g_ref, wg_ref, wu_ref, wd_ref, gf_ref, sink_ref, q_ref, kvt_ref, *rest,
                bounds, final_norm, n_base, t_new):
    o_ref, oa_ref, ob_ref = rest[-7:-4]
    _attn_sample_body(sink_ref, q_ref, kvt_ref, rest[:4], oa_ref, ob_ref, rest[-4:],
                      n_base + pl.program_id(0), t_new)
    o_ref[...] = _ffn_tile(x_ref[...], g_ref, wg_ref, wu_ref, wd_ref, gf_ref, bounds, final_norm)


def _ffn_bounds(f):
    n_tiles = f // MXU_TILE
    cuts = [0, (n_tiles + 1) // 2 * MXU_TILE, f] if FFN_CHUNKS == 2 and n_tiles > 1 else [0, f]
    return tuple(cuts)


def _ffn_specs(ffn, layer, d):
    return [_layer_block(a, layer) for a in ffn[:4]] + [_resident((1, d))]


def _ffn(x, layer, ffn, *, final_norm, tm, n_base, t_new, sinks_p, q_nat, kvt_new, caches,
         prev_out):
    n_tok, d = x.shape
    steps = n_tok // tm
    per_tile = LANES // t_new
    tok_spec = pl.BlockSpec((tm, d), lambda i: (i, 0))
    cache_spec = lambda c: pl.BlockSpec((None, None) + c.shape[2:],
                                        lambda i: (layer, n_base + i, 0, 0))
    prev_out = [] if prev_out is None else list(prev_out)
    in_specs = ([tok_spec] + _ffn_specs(ffn, layer, d)
                + [pl.BlockSpec(memory_space=pltpu.SMEM),
                   pl.BlockSpec((t_new, q_nat.shape[1]), lambda i: (n_base + i, 0)),
                   pl.BlockSpec((KVT_ROWS_Q, LANES), lambda i: (0, (n_base + i) // per_tile))]
                + [cache_spec(c) for c in caches]
                + [pl.BlockSpec(memory_space=pl.ANY)] * len(prev_out))
    operands = [x, *ffn, sinks_p, q_nat, kvt_new, *caches, *prev_out]
    first_prev = len(operands) - len(prev_out)
    aliases = {first_prev + k: 3 + k for k in range(len(prev_out))}
    out_shape = ([jax.ShapeDtypeStruct((n_tok, d), _F32),
                  jax.ShapeDtypeStruct((steps * t_new, A_OUT), _F32),
                  jax.ShapeDtypeStruct((steps * t_new, B_Q), _F32)]
                 + [jax.ShapeDtypeStruct(c.shape, c.dtype) for c in caches])
    out_specs = ([tok_spec, pl.BlockSpec((t_new, A_OUT), lambda i: (i, 0)),
                  pl.BlockSpec((t_new, B_Q), lambda i: (i, 0))]
                 + [cache_spec(c) for c in caches])
    kern = functools.partial(_ffn_kernel, bounds=_ffn_bounds(ffn[1].shape[-1]),
                             final_norm=final_norm, n_base=n_base, t_new=t_new)
    return pl.pallas_call(
        kern,
        out_shape=out_shape,
        grid=(steps,),
        in_specs=in_specs,
        out_specs=out_specs,
        input_output_aliases=aliases,
        compiler_params=_cparams(1),
        name=("ffn_final" if final_norm else "ffn") + "_attn_sample",
    )(*operands)


def _rope_nat(x, c, s_lo, s_hi):
    return x * c + pltpu.roll(x, ROT_HALF, 1) * s_hi + pltpu.roll(x, LANES - ROT_HALF, 1) * s_lo


def _rope_t(k_t, cos_t, sin_t, n_heads):
    parts = []
    for hh in range(n_heads):
        b = hh * HEAD_DIM
        x1 = k_t[b:b + ROT_HALF]
        x2 = k_t[b + ROT_HALF:b + ROT_DIM]
        parts += [x1 * cos_t - x2 * sin_t, x2 * cos_t + x1 * sin_t, k_t[b + ROT_DIM:b + HEAD_DIM]]
    return jnp.concatenate(parts, axis=0)


def _rope_tables(pos):
    inv = ROPE_THETA ** (-(jnp.arange(ROT_HALF, dtype=_F32) * 2.0 / ROT_DIM))
    ang = pos.astype(_F32)[:, None] * inv[None, :]
    return jnp.cos(ang), jnp.sin(ang)


def _nat_tables(cos, sin):
    p = cos.shape[0]
    one = jnp.ones((p, HEAD_DIM - ROT_DIM), _F32)
    zero = jnp.zeros((p, HEAD_DIM - ROT_DIM), _F32)
    zh = jnp.zeros((p, ROT_HALF), _F32)
    c = jnp.concatenate([cos, cos, one], axis=1)
    s_hi = jnp.concatenate([zh, sin, zero], axis=1)
    s_lo = jnp.concatenate([-sin, zh, zero], axis=1)
    rep = LANES // HEAD_DIM
    return tuple(jnp.tile(t, (1, rep)) for t in (c, s_lo, s_hi))


def _mix_in_prompt_kernel(x_ref, g_ref, wa_ref, wbq_ref, wbkv_ref, wt_ref, c_ref, slo_ref, shi_ref,
                          ct_ref, st_ref, *rest, n_prev):
    prev = rest[:4] if n_prev else None
    (q1_ref, k1_ref, v1_ref, q2_ref, k2_ref, v2_ref, q3_ref, k3_ref, v3_ref,
     qb_ref, kb_ref, vb_ref, c1_ref, c2_ref, c3_ref, cb_ref, zs_ref) = rest[-17:]
    j = pl.program_id(1)
    tm = x_ref.shape[0]
    h = _rms(x_ref[...], g_ref[...]).astype(_BF16)
    c, s_lo, s_hi = c_ref[...], slo_ref[...], shi_ref[...]

    def proj(w_ref, col0, width, kind):
        z = _dot(h, w_ref[:, col0:col0 + width])
        if kind == "v":
            return z
        slabs = []
        for s in range(width // LANES):
            zz = _rope_nat(z[:, s * LANES:(s + 1) * LANES], c, s_lo, s_hi)
            slabs.append(zz * (SCALE * LOG2_E) if kind == "q" else zz)
        return jnp.concatenate(slabs, axis=1)

    for i, (ref, kind) in enumerate(((q1_ref, "q"), (k1_ref, "k"), (v1_ref, "v"))):
        ref[...] = proj(wa_ref, i * A_OUT, A_OUT, kind).astype(_BF16)
    for g, refs in ((1, (q2_ref, k2_ref, v2_ref)), (2, (q3_ref, k3_ref, v3_ref))):
        dil = DIL_PAIRS[g][1]
        rows = tm // dil
        for i, (ref, kind) in enumerate(zip(refs, ("q", "k", "v"))):
            z = proj(wa_ref, (3 * g + i) * A_OUT, A_OUT, kind)
            for s in range(A_OUT // LANES):
                zs_ref[s] = z[:, s * LANES:(s + 1) * LANES]
            for r in range(dil):
                for s in range(A_OUT // LANES):
                    ref[r, :, s * LANES:(s + 1) * LANES] = (
                        zs_ref[s, pl.ds(r, rows, stride=dil), :].astype(_BF16))
    qb_ref[...] = proj(wbq_ref, 0, B_Q, "q").astype(_BF16)
    z = _dot(h, wbkv_ref[...])
    kb_ref[...] = _rope_nat(z[:, :B_KV], c, s_lo, s_hi).astype(_BF16)
    vb_ref[...] = z[:, B_KV:].astype(_BF16)

    def kv_t(row0, n_rows, h_tok, cos_t, sin_t, n_heads):
        z = _dot_nt(wt_ref[row0:row0 + n_rows, :], h_tok)
        half = n_rows // 2
        return jnp.concatenate([_rope_t(z[:half], cos_t, sin_t, n_heads), z[half:]], axis=0)

    def put(i, out_ref, new):
        if n_prev:
            out_ref[:n_prev] = prev[i][...]
        out_ref[n_prev] = new

    ct, st = ct_ref[...], st_ref[...]
    put(2, c3_ref, kv_t(2 * KV_ROWS_A, KV_ROWS_A, h, ct, st, HEADS_A))

    @pl.when(j == pl.num_programs(1) - 1)
    def _():
        put(1, c2_ref, kv_t(KV_ROWS_A, KV_ROWS_A, h, ct, st, HEADS_A))
        tail = tm - BLK
        put(0, c1_ref, kv_t(0, KV_ROWS_A, h[tail:], ct[:, tail:], st[:, tail:], HEADS_A))
        put(3, cb_ref, kv_t(3 * KV_ROWS_A, KV_ROWS_B, h[tail:], ct[:, tail:], st[:, tail:], N_KV_B))


def _mix_in_prompt(x, layer, g, w_in, w_bq, w_kvt, nat_tabs, t_tabs, n_batch, seq, prev_bufs):
    d = x.shape[1]
    n_j = seq // TM
    n_prev = prev_bufs[0].shape[0] if prev_bufs else 0
    col_b = 3 * N_GROUPS_A * A_OUT
    assert (col_b + B_Q) % (2 * B_KV) == 0
    row = lambda n, j: (n * n_j + j, 0)
    tok_spec = lambda w: pl.BlockSpec((TM, w), row)
    cls_spec = lambda dil: pl.BlockSpec((None, dil, TM // dil, A_OUT), lambda n, j: (n, 0, j, 0))
    nat = lambda w: jax.ShapeDtypeStruct((n_batch * seq, w), _BF16)
    cls = lambda dil: jax.ShapeDtypeStruct((n_batch, dil, seq // dil, A_OUT), _BF16)
    keep = [min(w, seq) for w, _ in DIL_PAIRS]
    assert keep == [BLK, TM, seq] and seq % TM == 0
    bufs = ((KV_ROWS_A, keep[0], BLK, lambda j: 0), (KV_ROWS_A, keep[1], TM, lambda j: 0),
            (KV_ROWS_A, keep[2], TM, lambda j: j), (KV_ROWS_B, min(WINDOW_B, seq), BLK, lambda j: 0))

    def buf_spec(layers, rows, width, jmap):
        return pl.BlockSpec((layers, None, rows, width), lambda n, j: (0, n, 0, jmap(j)))

    def buf_shape(layers, rows, kept):
        return jax.ShapeDtypeStruct((layers, n_batch, rows, kept), _F32)

    n_out = n_prev + 1
    out_shape = ([nat(A_OUT)] * 3 + [cls(4)] * 3 + [cls(16)] * 3 + [nat(B_Q), nat(B_KV), nat(B_KV)]
                 + [buf_shape(n_out, r, k) for r, k, _, _ in bufs])
    out_specs = ([tok_spec(A_OUT)] * 3 + [cls_spec(4)] * 3 + [cls_spec(16)] * 3
                 + [tok_spec(B_Q), tok_spec(B_KV), tok_spec(B_KV)]
                 + [buf_spec(n_out, r, w, jm) for r, _, w, jm in bufs])
    prev_specs = [buf_spec(n_prev, r, w, jm) for r, _, w, jm in bufs] if n_prev else []
    tab_spec = pl.BlockSpec((TM, LANES), lambda n, j: (j, 0))
    ttab_spec = pl.BlockSpec((ROT_HALF, TM), lambda n, j: (0, j))
    return pl.pallas_call(
        functools.partial(_mix_in_prompt_kernel, n_prev=n_prev),
        out_shape=out_shape,
        grid=(n_batch, n_j),
        in_specs=[tok_spec(d), _layer_block(g, layer),
                  _layer_block(w_in, layer, (d, col_b), (0, 0)), _layer_block(w_bq, layer),
                  _layer_block(w_in, layer, (d, 2 * B_KV), (0, (col_b + B_Q) // (2 * B_KV))),
                  _layer_block(w_kvt, layer)]
        + [tab_spec, tab_spec, tab_spec, ttab_spec, ttab_spec] + prev_specs,
        out_specs=out_specs,
        scratch_shapes=[pltpu.VMEM((A_OUT // LANES, TM, LANES), _F32)],
        compiler_params=_cparams(2),
        name="mix_in_prompt",
    )(x, g, w_in, w_bq, w_in, w_kvt, *nat_tabs, *t_tabs, *prev_bufs)


def _sample_in_kernel(x_ref, gf1_ref, wg_ref, wu_ref, wd_ref, gf_ref,
                      g_ref, w_ref, wt_ref, c_ref, slo_ref, shi_ref, ct_ref, st_ref,
                      o_ref, q_ref, kvt_ref, *, bounds):
    x = _ffn_tile(x_ref[...], gf1_ref, wg_ref, wu_ref, wd_ref, gf_ref, bounds, False)
    o_ref[...] = x
    h = _rms(x, g_ref[...]).astype(_BF16)
    c, s_lo, s_hi = c_ref[...], slo_ref[...], shi_ref[...]
    z = _dot(h, w_ref[...])
    for s in range(z.shape[1] // LANES):
        sl = slice(s * LANES, (s + 1) * LANES)
        q_ref[:, sl] = _rope_nat(z[:, sl], c, s_lo, s_hi) * SCALE
    ct, st = ct_ref[...], st_ref[...]
    for g in range(N_GROUPS_A):
        r0 = g * KV_ROWS_A
        zt = _dot_nt(wt_ref[r0:r0 + KV_ROWS_A, :], h)
        kvt_ref[r0:r0 + A_OUT, :] = _rope_t(zt[:A_OUT], ct, st, HEADS_A)
        kvt_ref[r0 + A_OUT:r0 + KV_ROWS_A, :] = zt[A_OUT:]
    r0 = N_GROUPS_A * KV_ROWS_A
    zt = _dot_nt(wt_ref[r0:r0 + KV_ROWS_B, :], h)
    kvt_ref[r0:r0 + B_KV, :] = _rope_t(zt[:B_KV], ct, st, N_KV_B)
    kvt_ref[r0 + B_KV:r0 + KV_ROWS_B, :] = zt[B_KV:]
    for i in range(len(VPU_GROUPS)):
        r0 = KVT_ROWS + i * A_OUT
        zt = _dot_nt(wt_ref[r0:r0 + A_OUT, :], h)
        kvt_ref[r0:r0 + A_OUT, :] = _rope_t(zt, ct, st, HEADS_A) * SCALE


def _sample_in(x, layer, ffn, g, w_q, w_kvt, nat_tabs, t_tabs, tm):
    n_tok, d = x.shape
    qw = w_q.shape[-1]
    tok_spec = pl.BlockSpec((tm, d), lambda i: (i, 0))
    tab_spec = pl.BlockSpec((tm, LANES), lambda i: (0, 0))
    ttab_spec = pl.BlockSpec((ROT_HALF, tm), lambda i: (0, 0))
    assert w_kvt.shape[1] == KVT_ROWS_Q
    return pl.pallas_call(
        functools.partial(_sample_in_kernel, bounds=_ffn_bounds(ffn[1].shape[-1])),
        out_shape=[jax.ShapeDtypeStruct((n_tok, d), _F32),
                   jax.ShapeDtypeStruct((n_tok, qw), _F32),
                   jax.ShapeDtypeStruct((KVT_ROWS_Q, n_tok), _F32)],
        grid=(n_tok // tm,),
        in_specs=[tok_spec] + _ffn_specs(ffn, layer, d)
        + [_layer_block(a, layer) for a in (g, w_q, w_kvt)]
        + [tab_spec, tab_spec, tab_spec, ttab_spec, ttab_spec],
        out_specs=[tok_spec, pl.BlockSpec((tm, qw), lambda i: (i, 0)),
                   pl.BlockSpec((KVT_ROWS_Q, tm), lambda i: (0, i))],
        compiler_params=_cparams(1),
        name="sample_in",
    )(x, *ffn, g, w_q, w_kvt, *nat_tabs, *t_tabs)


def _lane_head(t, width):
    return lax.shift_right_logical(lax.broadcasted_iota(jnp.int32, (t, width), 1),
                                   HEAD_DIM.bit_length() - 1)


def _head_stack_a(q):
    lane_head = _lane_head(q.shape[0], A_OUT)
    return jnp.concatenate([jnp.where(lane_head == hh, q, 0.0) for hh in range(HEADS_A)], axis=0)


def _head_unstack_a(o_st, col_st, t):
    lane_head = _lane_head(t, A_OUT)
    out = jnp.zeros((t, A_OUT), _F32)
    for hh in range(HEADS_A):
        col = col_st[hh * t:(hh + 1) * t]
        blk = jnp.broadcast_to(col, (t, A_OUT)) if o_st is None else o_st[hh * t:(hh + 1) * t] * col
        out = jnp.where(lane_head == hh, blk, out)
    return out


def _head_stack_b(q):
    t = q.shape[0]
    lo = lax.broadcasted_iota(jnp.int32, (t, LANES), 1) < HEAD_DIM
    parts = []
    for c in range(B_Q // LANES):
        qc = q[:, c * LANES:(c + 1) * LANES]
        parts += [jnp.where(lo, qc, 0.0), jnp.where(lo, 0.0, qc)]
    return jnp.concatenate(parts, axis=0)


def _head_unstack_b(o_st, col_st, t):
    lo = lax.broadcasted_iota(jnp.int32, (t, LANES), 1) < HEAD_DIM
    cols = []
    for c in range(B_Q // LANES):
        r0 = 2 * c * t
        a = o_st[r0:r0 + t] * col_st[r0:r0 + t]
        b = o_st[r0 + t:r0 + 2 * t] * col_st[r0 + t:r0 + 2 * t]
        cols.append(jnp.where(lo, a, b))
    return jnp.concatenate(cols, axis=1)


def _sink_rows(sink_ref, t):
    return jnp.concatenate(
        [jnp.full((t, 1), sink_ref[p], _F32) for p in range(N_HEADS_B)], axis=0)


def _attn_prompt_kernel(sinkl_ref, q1_ref, k1_ref, v1_ref, q2_ref, k2_ref, v2_ref,
                        q3_ref, k3_ref, v3_ref, qb_ref, kb_ref, vb_ref,
                        oa_ref, ob_ref, acc_s, den_s, m_s, bias_s):
    seq = q1_ref.shape[0]
    n_blk = seq // BLK
    dil2, dil3 = DIL_PAIRS[1][1], DIL_PAIRS[2][1]
    per_class2 = seq // dil2 // BLK
    assert dil2 * per_class2 == dil3 == n_blk and seq // dil3 == BLK

    lane_head_a = _lane_head(BLK, A_OUT)
    head_a_bf = lane_head_a.astype(_F32).astype(_BF16)
    lane_head_b = _lane_head(BLK, LANES)
    head_b_bf = lane_head_b.astype(_F32).astype(_BF16)
    r_i = lax.broadcasted_iota(jnp.int32, (BLK, 2 * BLK), 0)
    c_i = lax.broadcasted_iota(jnp.int32, (BLK, 2 * BLK), 1)
    dist = BLK + r_i - c_i
    for i, n_back in enumerate((BLK, WINDOW_B - 1)):
        band = (dist >= 0) & (dist <= n_back)
        bias_s[2 * i] = jnp.where(band & (c_i >= BLK), 0.0, NEG)
        bias_s[2 * i + 1] = jnp.where(band, 0.0, NEG)
    causal = jnp.where(lax.broadcasted_iota(jnp.int32, (BLK, BLK), 0)
                       >= lax.broadcasted_iota(jnp.int32, (BLK, BLK), 1), 0.0, NEG)

    def with_prev(i, has_prev):
        return bias_s[2 * i + jnp.where(has_prev, 1, 0)]

    def window(ref, lead, r0):
        p0 = pl.multiple_of(jnp.maximum(r0 - BLK, 0), BLK)
        return jnp.concatenate([ref[lead + (pl.ds(p0, BLK),)], ref[lead + (pl.ds(r0, BLK),)]],
                               axis=0)

    def softmax_block(qs, kw, vw, bias, n_heads):
        n_keys = kw.shape[0]
        s = _dot_nt(qs, kw).reshape(n_heads, BLK, n_keys) + bias[None]
        m = jnp.max(s, axis=-1, keepdims=True)
        e = jnp.exp2(s - m)
        den = jnp.sum(e, axis=-1, keepdims=True)
        o = _dot(e.reshape(n_heads * BLK, n_keys).astype(_BF16), vw)
        return o, den.reshape(n_heads * BLK, 1), m.reshape(n_heads * BLK, 1)

    def block_a(q, kw, vw, bias):
        zero = jnp.zeros_like(q)
        qs = jnp.concatenate([jnp.where(head_a_bf == hh, q, zero) for hh in range(HEADS_A)], axis=0)
        o, den, m = softmax_block(qs, kw, vw, bias, HEADS_A)
        acc = o[:BLK]
        den_u = jnp.broadcast_to(den[:BLK], (BLK, A_OUT))
        m_u = jnp.broadcast_to(m[:BLK], (BLK, A_OUT))
        for hh in range(1, HEADS_A):
            sel = lane_head_a == hh
            rows = slice(hh * BLK, (hh + 1) * BLK)
            acc = jnp.where(sel, o[rows], acc)
            den_u = jnp.where(sel, jnp.broadcast_to(den[rows], (BLK, A_OUT)), den_u)
            m_u = jnp.where(sel, jnp.broadcast_to(m[rows], (BLK, A_OUT)), m_u)
        return acc, den_u, m_u

    def store(g, start, stride, acc, den_u, m_u):
        idx = pl.ds(start, BLK, stride=stride)
        for s in range(A_OUT // LANES):
            sl = slice(s * LANES, (s + 1) * LANES)
            acc_s[g, s, idx, :] = acc[:, sl]
            den_s[g, s, idx, :] = den_u[:, sl]
            m_s[g, s, idx, :] = m_u[:, sl]

    def classes_body(i, carry):
        r = lax.div(i, per_class2)
        sb = i - r * per_class2
        s0 = pl.multiple_of(sb * BLK, BLK)
        res = block_a(q2_ref[r, pl.ds(s0, BLK)], window(k2_ref, (r,), s0), window(v2_ref, (r,), s0),
                      with_prev(0, sb > 0))
        store(0, r + dil2 * s0, dil2, *res)
        res = block_a(q3_ref[i], k3_ref[i], v3_ref[i], causal)
        store(1, i, dil3, *res)
        return carry

    lax.fori_loop(0, n_blk, classes_body, 0, unroll=True)

    lo = lane_head_b == 0
    sink_lanes = sinkl_ref[...]

    def tokens_body(b, carry):
        r0 = pl.multiple_of(b * BLK, BLK)
        acc1, den1, m1 = block_a(q1_ref[pl.ds(r0, BLK)], window(k1_ref, (), r0),
                                 window(v1_ref, (), r0), with_prev(0, b > 0))
        cols = []
        for s in range(A_OUT // LANES):
            sl = slice(s * LANES, (s + 1) * LANES)
            rows = pl.ds(r0, BLK)
            ms = [m1[:, sl], m_s[0, s, rows, :], m_s[1, s, rows, :]]
            dens = [den1[:, sl], den_s[0, s, rows, :], den_s[1, s, rows, :]]
            accs = [acc1[:, sl], acc_s[0, s, rows, :], acc_s[1, s, rows, :]]
            top = jnp.maximum(jnp.maximum(ms[0], ms[1]), ms[2])
            ws = [jnp.exp2(mm - top) for mm in ms]
            num = ws[0] * accs[0] + ws[1] * accs[1] + ws[2] * accs[2]
            dn = ws[0] * dens[0] + ws[1] * dens[1] + ws[2] * dens[2]
            cols.append(num / dn)
        oa_ref[pl.ds(r0, BLK), :] = jnp.concatenate(cols, axis=1).astype(oa_ref.dtype)

        q = qb_ref[pl.ds(r0, BLK)]
        zero = jnp.zeros((BLK, LANES), q.dtype)
        parts = []
        for c in range(B_Q // LANES):
            qc = q[:, c * LANES:(c + 1) * LANES]
            parts += [jnp.where(head_b_bf == 0, qc, zero), jnp.where(head_b_bf == 1, qc, zero)]
        o, den, m = softmax_block(jnp.concatenate(parts, axis=0), window(kb_ref, (), r0),
                                  window(vb_ref, (), r0), with_prev(1, b > 0), N_HEADS_B)
        cols = []
        for c in range(B_Q // LANES):
            ra = slice(2 * c * BLK, (2 * c + 1) * BLK)
            rb = slice((2 * c + 1) * BLK, (2 * c + 2) * BLK)
            o_c = jnp.where(lo, o[ra], o[rb])
            den_c = jnp.where(lo, jnp.broadcast_to(den[ra], (BLK, LANES)),
                              jnp.broadcast_to(den[rb], (BLK, LANES)))
            m_c = jnp.where(lo, jnp.broadcast_to(m[ra], (BLK, LANES)),
                            jnp.broadcast_to(m[rb], (BLK, LANES)))
            sink_c = sink_lanes[:, c * LANES:(c + 1) * LANES]
            cols.append(o_c / (den_c + jnp.exp2(sink_c - m_c)))
        ob_ref[pl.ds(r0, BLK), :] = jnp.concatenate(cols, axis=1).astype(ob_ref.dtype)
        return carry

    lax.fori_loop(0, n_blk, tokens_body, 0, unroll=True)


def _attn_prompt(sinks_p, qkv, n_batch, seq):
    (q1, k1, v1, q2, k2, v2, q3, k3, v3, qb, kb, vb) = qkv
    tok = lambda w: pl.BlockSpec((seq, w), lambda n: (n, 0))
    cls = lambda dil: pl.BlockSpec((None, dil, seq // dil, A_OUT), lambda n: (n, 0, 0, 0))
    return pl.pallas_call(
        _attn_prompt_kernel,
        out_shape=[jax.ShapeDtypeStruct((n_batch * seq, A_OUT), _BF16),
                   jax.ShapeDtypeStruct((n_batch * seq, B_Q), _BF16)],
        grid=(n_batch,),
        in_specs=[_resident((1, B_Q))]
        + [tok(A_OUT)] * 3 + [cls(4)] * 3 + [cls(16)] * 3 + [tok(B_Q), tok(B_KV), tok(B_KV)],
        out_specs=[tok(A_OUT), tok(B_Q)],
        scratch_shapes=[pltpu.VMEM((N_GROUPS_A - 1, A_OUT // LANES, seq, LANES), _F32)] * 3
        + [pltpu.VMEM((4, BLK, 2 * BLK), _F32)],
        compiler_params=_cparams(1),
        name="attn_prompt",
    )(jnp.repeat(sinks_p * LOG2_E, HEAD_DIM)[None, :], q1, k1, v1, q2, k2, v2, q3, k3, v3,
      qb, kb, vb)


def _attn_sample_body(sink_ref, q_ref, kvt_ref, cache_refs, oa_ref, ob_ref, new_refs, n, t_new):
    c1_ref, c2_ref, c3_ref, cb_ref = cache_refs
    n1_ref, n2_ref, n3_ref, nb_ref = new_refs
    per_tile = LANES // t_new
    first_new = LANES - t_new
    shift = first_new - (n & (per_tile - 1)) * t_new
    new_t = pltpu.roll(kvt_ref[...], shift, 1)
    q = q_ref[...]

    def new_cache(old_ref, new_rows, out_ref):
        lb = old_ref.shape[1]
        rolled = pltpu.roll(old_ref[...], lb - t_new, 1)
        if lb > LANES:
            out_ref[:, :lb - LANES] = rolled[:, :lb - LANES]
        lane = lax.broadcasted_iota(jnp.int32, (old_ref.shape[0], LANES), 1)
        out_ref[:, lb - LANES:] = jnp.where(lane >= first_new, new_rows, rolled[:, lb - LANES:])

    def masks(rows, lb, window, dil, strict):
        t_old = lax.broadcasted_iota(jnp.int32, (rows, lb), 0) & (t_new - 1)
        dist = lb + t_old - lax.broadcasted_iota(jnp.int32, (rows, lb), 1)
        hi = (dist < window) if strict else (dist <= window)
        valid_old = (dist >= 0) & hi & ((dist & (dil - 1)) == 0)
        t_nw = lax.broadcasted_iota(jnp.int32, (rows, LANES), 0) & (t_new - 1)
        c_nw = lax.broadcasted_iota(jnp.int32, (rows, LANES), 1) - first_new
        dn = t_nw - c_nw
        hin = (dn < window) if strict else (dn <= window)
        valid_new = (c_nw >= 0) & (dn >= 0) & hin & ((dn & (dil - 1)) == 0)
        return valid_old, valid_new

    def attend(qs, k_old, v_old, k_new, v_new, valid_old, valid_new):
        k = jnp.concatenate([k_old, k_new], axis=1)
        v = jnp.concatenate([v_old, v_new], axis=1)
        s = jnp.where(jnp.concatenate([valid_old, valid_new], axis=1), _dot(qs, k), NEG)
        m = jnp.max(s, axis=-1, keepdims=True)
        e = jnp.exp(s - m)
        den = jnp.sum(e, axis=-1, keepdims=True)
        return _dot_nt(e.astype(_BF16), v), den, m

    def classes_reduce(x, dil, op):
        sh = dil
        while sh < LANES:
            x = op(x, pltpu.roll(x, sh, 1))
            sh *= 2
        return x

    def fold_tiles(x, op):
        acc = x[:, :LANES]
        for jt in range(1, x.shape[1] // LANES):
            acc = op(acc, x[:, jt * LANES:(jt + 1) * LANES])
        return acc

    def attend_lane_classes(old_ref, new_g, q_t, window, dil):
        lb = old_ref.shape[1]
        n_tiles = lb // LANES
        assert t_new <= dil and lb % dil == 0 and LANES % dil == 0 and lb <= window
        lane_q = lax.broadcasted_iota(jnp.int32, (A_OUT, LANES), 1)
        q_new = jnp.where(lane_q >= first_new, q_t, 0.0)
        q_cls = classes_reduce(pltpu.roll(q_new, t_new, 1), dil, jnp.add)
        lane_1 = lax.broadcasted_iota(jnp.int32, (1, LANES), 1)
        cls_ok = jnp.concatenate([(lane_1 & (dil - 1)) < t_new] * n_tiles, axis=1)
        o_rows, lse_rows = [], []
        for hh in range(HEADS_A):
            rows = slice(hh * HEAD_DIM, (hh + 1) * HEAD_DIM)
            q_h = jnp.concatenate([q_cls[rows]] * n_tiles, axis=1)
            s_old = jnp.sum(old_ref[rows, :] * q_h, axis=0, keepdims=True)
            s_old = jnp.where(cls_ok, s_old, NEG)
            s_new = jnp.sum(new_g[rows] * q_new[rows], axis=0, keepdims=True)
            s_new = jnp.where(lane_1 < t_new, pltpu.roll(s_new, t_new, 1), NEG)
            m = classes_reduce(jnp.maximum(fold_tiles(s_old, jnp.maximum), s_new), dil, jnp.maximum)
            e_old = jnp.where(cls_ok, jnp.exp(s_old - jnp.concatenate([m] * n_tiles, axis=1)), 0.0)
            e_new = jnp.where(lane_1 < t_new, jnp.exp(s_new - m), 0.0)
            den = classes_reduce(fold_tiles(e_old, jnp.add) + e_new, dil, jnp.add)
            den = jnp.where((lane_1 & (dil - 1)) < t_new, den, 1.0)
            v_rows = slice(A_OUT + hh * HEAD_DIM, A_OUT + (hh + 1) * HEAD_DIM)
            acc = fold_tiles(old_ref[v_rows, :] * e_old, jnp.add)
            acc = acc + pltpu.roll(new_g[v_rows], t_new, 1) * e_new
            acc = classes_reduce(acc, dil, jnp.add)
            o_rows.append(acc / den)
            lse_rows.append(jnp.broadcast_to(m + jnp.log(den), (HEAD_DIM, LANES)))
        out_t = jnp.concatenate(o_rows, axis=0)
        lse_t = jnp.concatenate(lse_rows, axis=0)
        return out_t.T[:t_new], lse_t.T[:t_new]

    outs, lses = [], []
    for g, (old_ref, out_ref) in enumerate(((c1_ref, n1_ref), (c2_ref, n2_ref), (c3_ref, n3_ref))):
        window, dil = DIL_PAIRS[g]
        lb = old_ref.shape[1]
        new_g = new_t[g * KV_ROWS_A:(g + 1) * KV_ROWS_A]
        if g in VPU_GROUPS:
            q_row0 = KVT_ROWS + VPU_GROUPS.index(g) * A_OUT
            out, lse = attend_lane_classes(old_ref, new_g, new_t[q_row0:q_row0 + A_OUT],
                                           window, dil)
            outs.append(out)
            lses.append(lse)
            new_cache(old_ref, new_g, out_ref)
            continue
        q0 = MXU_GROUPS.index(g) * A_OUT
        qs = _head_stack_a(q[:, q0:q0 + A_OUT]).astype(_BF16)
        valid_old, valid_new = masks(HEADS_A * t_new, lb, window, dil, False)
        o, den, m = attend(qs, old_ref[0:A_OUT, :].astype(_BF16), old_ref[A_OUT:, :].astype(_BF16),
                           new_g[:A_OUT].astype(_BF16), new_g[A_OUT:].astype(_BF16),
                           valid_old, valid_new)
        outs.append(_head_unstack_a(o, 1.0 / den, t_new))
        lses.append(_head_unstack_a(None, m + jnp.log(den), t_new))
        new_cache(old_ref, new_g, out_ref)
    m = jnp.maximum(jnp.maximum(lses[0], lses[1]), lses[2])
    ws = [jnp.exp(l - m) for l in lses]
    oa_ref[...] = (ws[0] * outs[0] + ws[1] * outs[1] + ws[2] * outs[2]) / (ws[0] + ws[1] + ws[2])

    lb = cb_ref.shape[1]
    new_b = new_t[N_GROUPS_A * KV_ROWS_A:KVT_ROWS]
    qs = _head_stack_b(q[:, len(MXU_GROUPS) * A_OUT:]).astype(_BF16)
    valid_old, valid_new = masks(N_HEADS_B * t_new, lb, WINDOW_B, 1, True)
    o, den, m = attend(qs, cb_ref[0:B_KV, :].astype(_BF16), cb_ref[B_KV:, :].astype(_BF16),
                       new_b[:B_KV].astype(_BF16), new_b[B_KV:].astype(_BF16),
                       valid_old, valid_new)
    scale = _sigmoid(m + jnp.log(den) - _sink_rows(sink_ref, t_new)) / den
    ob_ref[...] = _head_unstack_b(o, scale, t_new)
    new_cache(cb_ref, new_b, nb_ref)


def _mix_out_tile(x, g_ref, oa_ref, ob_ref, wga_ref, wgb_ref, wba_ref, wbb_ref, wout_ref):
    h = _rms(x, g_ref[...]).astype(_BF16)
    gate_a = _dot(h, wga_ref[...])
    gate_b = _dot(h, wgb_ref[...])
    ya = _dot(oa_ref[...].astype(_BF16), wba_ref[...])
    yb = _dot(ob_ref[...].astype(_BF16), wbb_ref[...])
    merged = _sigmoid(gate_a) * ya + _sigmoid(gate_b) * yb
    return x + _dot(merged.astype(_BF16), wout_ref[...])


def _mix_out_kernel(x_ref, *refs, n_mix, bounds, final_norm):
    y = _mix_out_tile(x_ref[...], *refs[:n_mix])
    if bounds is not None:
        y = _ffn_tile(y, *refs[n_mix:n_mix + 5], bounds, final_norm)
    refs[-1][...] = y


def _mix_out(x, layer, g, oa, ob, w_in, w_ba, w_bb, w_out, tm, ffn=None, final_norm=False):
    n_tok, d = x.shape
    assert QKV_COLS % d == 0 and w_in.shape[2] == QKV_COLS + 2 * d
    gate_specs = [_layer_block(w_in, layer, (d, d), (0, QKV_COLS // d + k)) for k in range(2)]
    tok = lambda w: pl.BlockSpec((tm, w), lambda i: (i, 0))
    mix_specs = ([_layer_block(g, layer), tok(A_OUT), tok(B_Q)] + gate_specs
                 + [_layer_block(a, layer) for a in (w_ba, w_bb, w_out)])
    mix_ops = [g, oa, ob, w_in, w_in, w_ba, w_bb, w_out]
    ffn_specs, ffn_ops, bounds = [], [], None
    if ffn is not None:
        ffn_specs, ffn_ops, bounds = _ffn_specs(ffn, layer, d), list(ffn), _ffn_bounds(ffn[1].shape[-1])
    return pl.pallas_call(
        functools.partial(_mix_out_kernel, n_mix=len(mix_ops), bounds=bounds,
                          final_norm=final_norm),
        out_shape=jax.ShapeDtypeStruct((n_tok, d), _F32),
        grid=(n_tok // tm,),
        in_specs=[tok(d)] + mix_specs + ffn_specs,
        out_specs=tok(d),
        compiler_params=_cparams(1),
        name="mix_out" if ffn is None else "sample_out",
    )(x, *mix_ops, *ffn_ops)


def _cache_view(c):
    dep, nb, rows = c.shape[:3]
    return jnp.transpose(c, (0, 1, 3, 4, 5, 2)).reshape(dep, nb, -1, rows)


def _cache_unview(c, heads):
    dep, nb, _, rows = c.shape
    return jnp.transpose(c.reshape(dep, nb, 2, heads, HEAD_DIM, rows), (0, 1, 5, 2, 3, 4))


def kernel(x_prompt, x_sample, cache_a1, cache_a2, cache_a3, cache_b, norm_ffn1, ffn1_gate, ffn1_up, ffn1_down, norm_mix, w_in, sinks, w_branch_a, w_branch_b, w_out, norm_ffn2, ffn2_gate, ffn2_up, ffn2_down, norm_final):
    n_batch, seq, d = x_prompt.shape
    dec_batch, t_new, _ = x_sample.shape
    depth = w_in.shape[0]
    n_sample = dec_batch * t_new
    tm_s = min(TM, n_sample)
    hosts_per_layer = 2
    n_hosted = dec_batch // hosts_per_layer
    tm_p = n_batch * seq // n_hosted
    assert seq == DIL_PAIRS[2][0] and LANES % t_new == 0 and n_sample % tm_s == 0
    assert tm_s % t_new == 0 and dec_batch % (LANES // t_new) == 0
    assert dec_batch % hosts_per_layer == 0 and (n_batch * seq) % n_hosted == 0 and tm_p % 8 == 0

    perm = jnp.array(B_HEAD_PERM)
    col_b = 3 * N_GROUPS_A * A_OUT

    cos_p, sin_p = _rope_tables(jnp.arange(seq))
    cos_s, sin_s = _rope_tables(PAST_LEN + jnp.arange(t_new))
    cos_s, sin_s = jnp.tile(cos_s, (tm_s // t_new, 1)), jnp.tile(sin_s, (tm_s // t_new, 1))
    nat_p, nat_s = _nat_tables(cos_p, sin_p), _nat_tables(cos_s, sin_s)
    t_p, t_s = (cos_p.T, sin_p.T), (cos_s.T, sin_s.T)

    caches = [_cache_view(c) for c in (cache_a1, cache_a2, cache_a3, cache_b)]

    bf = lambda w: w.astype(_BF16)
    w_in_bf = bf(w_in)
    cols = lambda lo, hi: w_in_bf[:, :, lo:hi]
    bq = cols(col_b, col_b + B_Q).reshape(depth, d, N_HEADS_B, HEAD_DIM)[:, :, perm]
    bq = bq.reshape(depth, d, B_Q)
    q_cols = [cols(3 * g * A_OUT, (3 * g + 1) * A_OUT) for g in range(N_GROUPS_A)]
    kv_cols = [cols((3 * g + 1) * A_OUT, (3 * g + 3) * A_OUT) for g in range(N_GROUPS_A)]
    kv_cols.append(cols(col_b + B_Q, QKV_COLS))
    w_kvt = jnp.swapaxes(jnp.concatenate(kv_cols + [q_cols[g] for g in VPU_GROUPS], axis=2), 1, 2)
    w_q = jnp.concatenate([q_cols[g] for g in MXU_GROUPS] + [bq], axis=2)
    w_bb = bf(w_branch_b).reshape(depth, N_HEADS_B, HEAD_DIM, d)[:, perm].reshape(depth, B_Q, d)
    w_ba, w_o = bf(w_branch_a), bf(w_out)
    sinks_perm = sinks[:, perm]
    row = lambda v: v.reshape(-1, 1, d)
    gf = norm_final.reshape(1, d)
    f1 = (row(norm_ffn1), bf(ffn1_gate), bf(ffn1_up), bf(ffn1_down), gf)
    f2 = (row(norm_ffn2), bf(ffn2_gate), bf(ffn2_up), bf(ffn2_down), gf)
    g_mix = row(norm_mix)

    xp = x_prompt.reshape(n_batch * seq, d)
    xs = x_sample.reshape(dec_batch * t_new, d)
    prompt_bufs = []
    sample_bufs = None
    for l in range(depth):
        last = l == depth - 1
        sinks_p = sinks_perm[l]

        xs, q_nat, kvt_new = _sample_in(xs, l, f1, g_mix, w_q, w_kvt, nat_s, t_s, tm_s)
        host = lambda n_base: dict(n_base=n_base, t_new=t_new, sinks_p=sinks_p, q_nat=q_nat,
                                   kvt_new=kvt_new, caches=caches, prev_out=sample_bufs)

        xp, oa_0, ob_0, *sample_bufs = _ffn(xp, l, f1, final_norm=False, tm=tm_p, **host(0))
        outs = _mix_in_prompt(xp, l, g_mix, w_in_bf, bq, w_kvt, nat_p, t_p, n_batch, seq,
                              prompt_bufs)
        prompt_bufs = list(outs[12:])
        oa_p, ob_p = _attn_prompt(sinks_p, outs[:12], n_batch, seq)
        xp = _mix_out(xp, l, g_mix, oa_p, ob_p, w_in_bf, w_ba, w_bb, w_o, TM)
        xp, oa_1, ob_1, *sample_bufs = _ffn(xp, l, f2, final_norm=last, tm=tm_p,
                                            **host(n_hosted))

        oa_s = jnp.concatenate([oa_0, oa_1], axis=0)
        ob_s = jnp.concatenate([ob_0, ob_1], axis=0)
        xs = _mix_out(xs, l, g_mix, oa_s, ob_s, w_in_bf, w_ba, w_bb, w_o, tm_s, ffn=f2,
                      final_norm=last)

    heads = (HEADS_A, HEADS_A, HEADS_A, N_KV_B)
    new_p = [_cache_unview(prompt_bufs[i], heads[i]) for i in range(4)]
    new_s = [_cache_unview(sample_bufs[i], heads[i]) for i in range(4)]
    return (xp.reshape(n_batch, seq, d), xs.reshape(dec_batch, t_new, d), *new_p, *new_s)
```

```python
import functools

import jax
import jax.numpy as jnp
from jax import lax
from jax.experimental import pallas as pl
from jax.experimental.pallas import tpu as pltpu

PAST_LEN = 16384
HEAD_DIM = 64
ROT_DIM = HEAD_DIM // 4
ROT_HALF = ROT_DIM // 2
ROPE_THETA = 500000.0
DIL_PAIRS = ((128, 1), (512, 4), (2048, 16))
N_GROUPS_A = 3
HEADS_A = 4
A_OUT = HEADS_A * HEAD_DIM
N_HEADS_B = 8
N_KV_B = 2
WINDOW_B = 128
B_Q = N_HEADS_B * HEAD_DIM
B_KV = N_KV_B * HEAD_DIM
QKV_COLS = N_GROUPS_A * 3 * A_OUT + B_Q + 2 * B_KV
KV_ROWS_A = 2 * A_OUT
KV_ROWS_B = 2 * B_KV
KVT_ROWS = N_GROUPS_A * KV_ROWS_A + KV_ROWS_B
VPU_GROUPS = (2,)
MXU_GROUPS = tuple(g for g in range(N_GROUPS_A) if g not in VPU_GROUPS)
KVT_ROWS_Q = KVT_ROWS + len(VPU_GROUPS) * A_OUT
NORM_EPS = 1e-6
NEG = -1e30
SCALE = HEAD_DIM ** -0.5
LOG2_E = 1.4426950408889634
BLK = 128
LANES = 128
MXU_TILE = 256
TM = 512
FFN_CHUNKS = 2
B_HEAD_PERM = (0, 4, 1, 5, 2, 6, 3, 7)
V7X_SCOPED_VMEM_BYTES = 60000 * 1024

_BF16 = jnp.bfloat16
_F32 = jnp.float32


def _cparams(n_axes):
    return pltpu.CompilerParams(
        dimension_semantics=("arbitrary",) * n_axes,
        vmem_limit_bytes=V7X_SCOPED_VMEM_BYTES,
    )


def _resident(shape):
    nd = len(shape)
    return pl.BlockSpec(shape, lambda *_: (0,) * nd, pipeline_mode=pl.Buffered(1))


def _layer_block(arr, layer, block=None, index=None):
    block = tuple(arr.shape[1:]) if block is None else tuple(block)
    index = (0,) * len(block) if index is None else tuple(index)
    return pl.BlockSpec((None,) + block, lambda *_: (layer,) + index,
                        pipeline_mode=pl.Buffered(1))


def _rms(x, g):
    return x * lax.rsqrt(jnp.mean(x * x, axis=-1, keepdims=True) + NORM_EPS) * g


def _sigmoid(x):
    return 1.0 / (1.0 + jnp.exp(-x))


def _dot(a, b):
    return jnp.dot(a, b, preferred_element_type=_F32)


def _dot_nt(a, b):
    return lax.dot_general(a, b, (((1,), (1,)), ((), ())), preferred_element_type=_F32)


def _ffn_tile(x, g_ref, wg_ref, wu_ref, wd_ref, gf_ref, bounds, final_norm):
    h = _rms(x, g_ref[...]).astype(_BF16)
    acc = None
    for lo, hi in zip(bounds[:-1], bounds[1:]):
        gate = _dot(h, wg_ref[:, lo:hi])
        up = _dot(h, wu_ref[:, lo:hi])
        act = (gate * _sigmoid(gate) * up).astype(_BF16)
        part = _dot(act, wd_ref[lo:hi, :])
        acc = part if acc is None else acc + part
    y = x + 0.5 * acc
    return _rms(y, gf_ref[...]) if final_norm else y


def _ffn_kernel(x_ref, g_ref, wg_ref, wu_ref, wd_ref, gf_ref, sink_ref, q_ref, kvt_ref, *rest,
                bounds, final_norm, n_base, t_new):
    o_ref, oa_ref, ob_ref = rest[-7:-4]
    _attn_sample_body(sink_ref, q_ref, kvt_ref, rest[:4], oa_ref, ob_ref, rest[-4:],
                      n_base + pl.program_id(0), t_new)
    o_ref[...] = _ffn_tile(x_ref[...], g_ref, wg_ref, wu_ref, wd_ref, gf_ref, bounds, final_norm)


def _ffn_bounds(f):
    n_tiles = f // MXU_TILE
    cuts = [0, (n_tiles + 1) // 2 * MXU_TILE, f] if FFN_CHUNKS == 2 and n_tiles > 1 else [0, f]
    return tuple(cuts)


def _ffn_specs(ffn, layer, d):
    return [_layer_block(a, layer) for a in ffn[:4]] + [_resident((1, d))]


def _ffn(x, layer, ffn, *, final_norm, tm, n_base, t_new, sinks_p, q_nat, kvt_new, caches,
         prev_out):
    n_tok, d = x.shape
    steps = n_tok // tm
    per_tile = LANES // t_new
    tok_spec = pl.BlockSpec((tm, d), lambda i: (i, 0))
    cache_spec = lambda c: pl.BlockSpec((None, None) + c.shape[2:],
                                        lambda i: (layer, n_base + i, 0, 0))
    prev_out = [] if prev_out is None else list(prev_out)
    in_specs = ([tok_spec] + _ffn_specs(ffn, layer, d)
                + [pl.BlockSpec(memory_space=pltpu.SMEM),
                   pl.BlockSpec((t_new, q_nat.shape[1]), lambda i: (n_base + i, 0)),
                   pl.BlockSpec((KVT_ROWS_Q, LANES), lambda i: (0, (n_base + i) // per_tile))]
                + [cache_spec(c) for c in caches]
                + [pl.BlockSpec(memory_space=pl.ANY)] * len(prev_out))
    operands = [x, *ffn, sinks_p, q_nat, kvt_new, *caches, *prev_out]
    first_prev = len(operands) - len(prev_out)
    aliases = {first_prev + k: 3 + k for k in range(len(prev_out))}
    out_shape = ([jax.ShapeDtypeStruct((n_tok, d), _F32),
                  jax.ShapeDtypeStruct((steps * t_new, A_OUT), _F32),
                  jax.ShapeDtypeStruct((steps * t_new, B_Q), _F32)]
                 + [jax.ShapeDtypeStruct(c.shape, c.dtype) for c in caches])
    out_specs = ([tok_spec, pl.BlockSpec((t_new, A_OUT), lambda i: (i, 0)),
                  pl.BlockSpec((t_new, B_Q), lambda i: (i, 0))]
                 + [cache_spec(c) for c in caches])
    kern = functools.partial(_ffn_kernel, bounds=_ffn_bounds(ffn[1].shape[-1]),
                             final_norm=final_norm, n_base=n_base, t_new=t_new)
    return pl.pallas_call(
        kern,
        out_shape=out_shape,
        grid=(steps,),
        in_specs=in_specs,
        out_specs=out_specs,
        input_output_aliases=aliases,
        compiler_params=_cparams(1),
        name=("ffn_final" if final_norm else "ffn") + "_attn_sample",
    )(*operands)


def _rope_nat(x, c, s_lo, s_hi):
    return x * c + pltpu.roll(x, ROT_HALF, 1) * s_hi + pltpu.roll(x, LANES - ROT_HALF, 1) * s_lo


def _rope_t(k_t, cos_t, sin_t, n_heads):
    parts = []
    for hh in range(n_heads):
        b = hh * HEAD_DIM
        x1 = k_t[b:b + ROT_HALF]
        x2 = k_t[b + ROT_HALF:b + ROT_DIM]
        parts += [x1 * cos_t - x2 * sin_t, x2 * cos_t + x1 * sin_t, k_t[b + ROT_DIM:b + HEAD_DIM]]
    return jnp.concatenate(parts, axis=0)


def _rope_tables(pos):
    inv = ROPE_THETA ** (-(jnp.arange(ROT_HALF, dtype=_F32) * 2.0 / ROT_DIM))
    ang = pos.astype(_F32)[:, None] * inv[None, :]
    return jnp.cos(ang), jnp.sin(ang)


def _nat_tables(cos, sin):
    p = cos.shape[0]
    one = jnp.ones((p, HEAD_DIM - ROT_DIM), _F32)
    zero = jnp.zeros((p, HEAD_DIM - ROT_DIM), _F32)
    zh = jnp.zeros((p, ROT_HALF), _F32)
    c = jnp.concatenate([cos, cos, one], axis=1)
    s_hi = jnp.concatenate([zh, sin, zero], axis=1)
    s_lo = jnp.concatenate([-sin, zh, zero], axis=1)
    rep = LANES // HEAD_DIM
    return tuple(jnp.tile(t, (1, rep)) for t in (c, s_lo, s_hi))


def _mix_in_prompt_kernel(x_ref, g_ref, wa_ref, wbq_ref, wbkv_ref, wt_ref, c_ref, slo_ref, shi_ref,
                          ct_ref, st_ref, *rest, n_prev):
    prev = rest[:4] if n_prev else None
    (q1_ref, k1_ref, v1_ref, q2_ref, k2_ref, v2_ref, q3_ref, k3_ref, v3_ref,
     qb_ref, kb_ref, vb_ref, c1_ref, c2_ref, c3_ref, cb_ref, zs_ref) = rest[-17:]
    j = pl.program_id(1)
    tm = x_ref.shape[0]
    h = _rms(x_ref[...], g_ref[...]).astype(_BF16)
    c, s_lo, s_hi = c_ref[...], slo_ref[...], shi_ref[...]

    def proj(w_ref, col0, width, kind):
        z = _dot(h, w_ref[:, col0:col0 + width])
        if kind == "v":
            return z
        slabs = []
        for s in range(width // LANES):
            zz = _rope_nat(z[:, s * LANES:(s + 1) * LANES], c, s_lo, s_hi)
            slabs.append(zz * (SCALE * LOG2_E) if kind == "q" else zz)
        return jnp.concatenate(slabs, axis=1)

    for i, (ref, kind) in enumerate(((q1_ref, "q"), (k1_ref, "k"), (v1_ref, "v"))):
        ref[...] = proj(wa_ref, i * A_OUT, A_OUT, kind).astype(_BF16)
    for g, refs in ((1, (q2_ref, k2_ref, v2_ref)), (2, (q3_ref, k3_ref, v3_ref))):
        dil = DIL_PAIRS[g][1]
        rows = tm // dil
        for i, (ref, kind) in enumerate(zip(refs, ("q", "k", "v"))):
            z = proj(wa_ref, (3 * g + i) * A_OUT, A_OUT, kind)
            for s in range(A_OUT // LANES):
                zs_ref[s] = z[:, s * LANES:(s + 1) * LANES]
            for r in range(dil):
                for s in range(A_OUT // LANES):
                    ref[r, :, s * LANES:(s + 1) * LANES] = (
                        zs_ref[s, pl.ds(r, rows, stride=dil), :].astype(_BF16))
    qb_ref[...] = proj(wbq_ref, 0, B_Q, "q").astype(_BF16)
    z = _dot(h, wbkv_ref[...])
    kb_ref[...] = _rope_nat(z[:, :B_KV], c, s_lo, s_hi).astype(_BF16)
    vb_ref[...] = z[:, B_KV:].astype(_BF16)

    def kv_t(row0, n_rows, h_tok, cos_t, sin_t, n_heads):
        z = _dot_nt(wt_ref[row0:row0 + n_rows, :], h_tok)
        half = n_rows // 2
        return jnp.concatenate([_rope_t(z[:half], cos_t, sin_t, n_heads), z[half:]], axis=0)

    def put(i, out_ref, new):
        if n_prev:
            out_ref[:n_prev] = prev[i][...]
        out_ref[n_prev] = new

    ct, st = ct_ref[...], st_ref[...]
    put(2, c3_ref, kv_t(2 * KV_ROWS_A, KV_ROWS_A, h, ct, st, HEADS_A))

    @pl.when(j == pl.num_programs(1) - 1)
    def _():
        put(1, c2_ref, kv_t(KV_ROWS_A, KV_ROWS_A, h, ct, st, HEADS_A))
        tail = tm - BLK
        put(0, c1_ref, kv_t(0, KV_ROWS_A, h[tail:], ct[:, tail:], st[:, tail:], HEADS_A))
        put(3, cb_ref, kv_t(3 * KV_ROWS_A, KV_ROWS_B, h[tail:], ct[:, tail:], st[:, tail:], N_KV_B))


def _mix_in_prompt(x, layer, g, w_in, w_bq, w_kvt, nat_tabs, t_tabs, n_batch, seq, prev_bufs):
    d = x.shape[1]
    n_j = seq // TM
    n_prev = prev_bufs[0].shape[0] if prev_bufs else 0
    col_b = 3 * N_GROUPS_A * A_OUT
    assert (col_b + B_Q) % (2 * B_KV) == 0
    row = lambda n, j: (n * n_j + j, 0)
    tok_spec = lambda w: pl.BlockSpec((TM, w), row)
    cls_spec = lambda dil: pl.BlockSpec((None, dil, TM // dil, A_OUT), lambda n, j: (n, 0, j, 0))
    nat = lambda w: jax.ShapeDtypeStruct((n_batch * seq, w), _BF16)
    cls = lambda dil: jax.ShapeDtypeStruct((n_batch, dil, seq // dil, A_OUT), _BF16)
    keep = [min(w, seq) for w, _ in DIL_PAIRS]
    assert keep == [BLK, TM, seq] and seq % TM == 0
    bufs = ((KV_ROWS_A, keep[0], BLK, lambda j: 0), (KV_ROWS_A, keep[1], TM, lambda j: 0),
            (KV_ROWS_A, keep[2], TM, lambda j: j), (KV_ROWS_B, min(WINDOW_B, seq), BLK, lambda j: 0))

    def buf_spec(layers, rows, width, jmap):
        return pl.BlockSpec((layers, None, rows, width), lambda n, j: (0, n, 0, jmap(j)))

    def buf_shape(layers, rows, kept):
        return jax.ShapeDtypeStruct((layers, n_batch, rows, kept), _F32)

    n_out = n_prev + 1
    out_shape = ([nat(A_OUT)] * 3 + [cls(4)] * 3 + [cls(16)] * 3 + [nat(B_Q), nat(B_KV), nat(B_KV)]
                 + [buf_shape(n_out, r, k) for r, k, _, _ in bufs])
    out_specs = ([tok_spec(A_OUT)] * 3 + [cls_spec(4)] * 3 + [cls_spec(16)] * 3
                 + [tok_spec(B_Q), tok_spec(B_KV), tok_spec(B_KV)]
                 + [buf_spec(n_out, r, w, jm) for r, _, w, jm in bufs])
    prev_specs = [buf_spec(n_prev, r, w, jm) for r, _, w, jm in bufs] if n_prev else []
    tab_spec = pl.BlockSpec((TM, LANES), lambda n, j: (j, 0))
    ttab_spec = pl.BlockSpec((ROT_HALF, TM), lambda n, j: (0, j))
    return pl.pallas_call(
        functools.partial(_mix_in_prompt_kernel, n_prev=n_prev),
        out_shape=out_shape,
        grid=(n_batch, n_j),
        in_specs=[tok_spec(d), _layer_block(g, layer),
                  _layer_block(w_in, layer, (d, col_b), (0, 0)), _layer_block(w_bq, layer),
                  _layer_block(w_in, layer, (d, 2 * B_KV), (0, (col_b + B_Q) // (2 * B_KV))),
                  _layer_block(w_kvt, layer)]
        + [tab_spec, tab_spec, tab_spec, ttab_spec, ttab_spec] + prev_specs,
        out_specs=out_specs,
        scratch_shapes=[pltpu.VMEM((A_OUT // LANES, TM, LANES), _F32)],
        compiler_params=_cparams(2),
        name="mix_in_prompt",
    )(x, g, w_in, w_bq, w_in, w_kvt, *nat_tabs, *t_tabs, *prev_bufs)


def _sample_in_kernel(x_ref, gf1_ref, wg_ref, wu_ref, wd_ref, gf_ref,
                      g_ref, w_ref, wt_ref, c_ref, slo_ref, shi_ref, ct_ref, st_ref,
                      o_ref, q_ref, kvt_ref, *, bounds):
    x = _ffn_tile(x_ref[...], gf1_ref, wg_ref, wu_ref, wd_ref, gf_ref, bounds, False)
    o_ref[...] = x
    h = _rms(x, g_ref[...]).astype(_BF16)
    c, s_lo, s_hi = c_ref[...], slo_ref[...], shi_ref[...]
    z = _dot(h, w_ref[...])
    for s in range(z.shape[1] // LANES):
        sl = slice(s * LANES, (s + 1) * LANES)
        q_ref[:, sl] = _rope_nat(z[:, sl], c, s_lo, s_hi) * SCALE
    ct, st = ct_ref[...], st_ref[...]
    for g in range(N_GROUPS_A):
        r0 = g * KV_ROWS_A
        zt = _dot_nt(wt_ref[r0:r0 + KV_ROWS_A, :], h)
        kvt_ref[r0:r0 + A_OUT, :] = _rope_t(zt[:A_OUT], ct, st, HEADS_A)
        kvt_ref[r0 + A_OUT:r0 + KV_ROWS_A, :] = zt[A_OUT:]
    r0 = N_GROUPS_A * KV_ROWS_A
    zt = _dot_nt(wt_ref[r0:r0 + KV_ROWS_B, :], h)
    kvt_ref[r0:r0 + B_KV, :] = _rope_t(zt[:B_KV], ct, st, N_KV_B)
    kvt_ref[r0 + B_KV:r0 + KV_ROWS_B, :] = zt[B_KV:]
    for i in range(len(VPU_GROUPS)):
        r0 = KVT_ROWS + i * A_OUT
        zt = _dot_nt(wt_ref[r0:r0 + A_OUT, :], h)
        kvt_ref[r0:r0 + A_OUT, :] = _rope_t(zt, ct, st, HEADS_A) * SCALE


def _sample_in(x, layer, ffn, g, w_q, w_kvt, nat_tabs, t_tabs, tm):
    n_tok, d = x.shape
    qw = w_q.shape[-1]
    tok_spec = pl.BlockSpec((tm, d), lambda i: (i, 0))
    tab_spec = pl.BlockSpec((tm, LANES), lambda i: (0, 0))
    ttab_spec = pl.BlockSpec((ROT_HALF, tm), lambda i: (0, 0))
    assert w_kvt.shape[1] == KVT_ROWS_Q
    return pl.pallas_call(
        functools.partial(_sample_in_kernel, bounds=_ffn_bounds(ffn[1].shape[-1])),
        out_shape=[jax.ShapeDtypeStruct((n_tok, d), _F32),
                   jax.ShapeDtypeStruct((n_tok, qw), _F32),
                   jax.ShapeDtypeStruct((KVT_ROWS_Q, n_tok), _F32)],
        grid=(n_tok // tm,),
        in_specs=[tok_spec] + _ffn_specs(ffn, layer, d)
        + [_layer_block(a, layer) for a in (g, w_q, w_kvt)]
        + [tab_spec, tab_spec, tab_spec, ttab_spec, ttab_spec],
        out_specs=[tok_spec, pl.BlockSpec((tm, qw), lambda i: (i, 0)),
                   pl.BlockSpec((KVT_ROWS_Q, tm), lambda i: (0, i))],
        compiler_params=_cparams(1),
        name="sample_in",
    )(x, *ffn, g, w_q, w_kvt, *nat_tabs, *t_tabs)


def _lane_head(t, width):
    return lax.shift_right_logical(lax.broadcasted_iota(jnp.int32, (t, width), 1),
                                   HEAD_DIM.bit_length() - 1)


def _head_stack_a(q):
    lane_head = _lane_head(q.shape[0], A_OUT)
    return jnp.concatenate([jnp.where(lane_head == hh, q, 0.0) for hh in range(HEADS_A)], axis=0)


def _head_unstack_a(o_st, col_st, t):
    lane_head = _lane_head(t, A_OUT)
    out = jnp.zeros((t, A_OUT), _F32)
    for hh in range(HEADS_A):
        col = col_st[hh * t:(hh + 1) * t]
        blk = jnp.broadcast_to(col, (t, A_OUT)) if o_st is None else o_st[hh * t:(hh + 1) * t] * col
        out = jnp.where(lane_head == hh, blk, out)
    return out


def _head_stack_b(q):
    t = q.shape[0]
    lo = lax.broadcasted_iota(jnp.int32, (t, LANES), 1) < HEAD_DIM
    parts = []
    for c in range(B_Q // LANES):
        qc = q[:, c * LANES:(c + 1) * LANES]
        parts += [jnp.where(lo, qc, 0.0), jnp.where(lo, 0.0, qc)]
    return jnp.concatenate(parts, axis=0)


def _head_unstack_b(o_st, col_st, t):
    lo = lax.broadcasted_iota(jnp.int32, (t, LANES), 1) < HEAD_DIM
    cols = []
    for c in range(B_Q // LANES):
        r0 = 2 * c * t
        a = o_st[r0:r0 + t] * col_st[r0:r0 + t]
        b = o_st[r0 + t:r0 + 2 * t] * col_st[r0 + t:r0 + 2 * t]
        cols.append(jnp.where(lo, a, b))
    return jnp.concatenate(cols, axis=1)


def _sink_rows(sink_ref, t):
    return jnp.concatenate(
        [jnp.full((t, 1), sink_ref[p], _F32) for p in range(N_HEADS_B)], axis=0)


def _attn_prompt_kernel(sinkl_ref, q1_ref, k1_ref, v1_ref, q2_ref, k2_ref, v2_ref,
                        q3_ref, k3_ref, v3_ref, qb_ref, kb_ref, vb_ref,
                        oa_ref, ob_ref, acc_s, den_s, m_s, bias_s):
    seq = q1_ref.shape[0]
    n_blk = seq // BLK
    dil2, dil3 = DIL_PAIRS[1][1], DIL_PAIRS[2][1]
    per_class2 = seq // dil2 // BLK
    assert dil2 * per_class2 == dil3 == n_blk and seq // dil3 == BLK

    lane_head_a = _lane_head(BLK, A_OUT)
    head_a_bf = lane_head_a.astype(_F32).astype(_BF16)
    lane_head_b = _lane_head(BLK, LANES)
    head_b_bf = lane_head_b.astype(_F32).astype(_BF16)
    r_i = lax.broadcasted_iota(jnp.int32, (BLK, 2 * BLK), 0)
    c_i = lax.broadcasted_iota(jnp.int32, (BLK, 2 * BLK), 1)
    dist = BLK + r_i - c_i
    for i, n_back in enumerate((BLK, WINDOW_B - 1)):
        band = (dist >= 0) & (dist <= n_back)
        bias_s[2 * i] = jnp.where(band & (c_i >= BLK), 0.0, NEG)
        bias_s[2 * i + 1] = jnp.where(band, 0.0, NEG)
    causal = jnp.where(lax.broadcasted_iota(jnp.int32, (BLK, BLK), 0)
                       >= lax.broadcasted_iota(jnp.int32, (BLK, BLK), 1), 0.0, NEG)

    def with_prev(i, has_prev):
        return bias_s[2 * i + jnp.where(has_prev, 1, 0)]

    def window(ref, lead, r0):
        p0 = pl.multiple_of(jnp.maximum(r0 - BLK, 0), BLK)
        return jnp.concatenate([ref[lead + (pl.ds(p0, BLK),)], ref[lead + (pl.ds(r0, BLK),)]],
                               axis=0)

    def softmax_block(qs, kw, vw, bias, n_heads):
        n_keys = kw.shape[0]
        s = _dot_nt(qs, kw).reshape(n_heads, BLK, n_keys) + bias[None]
        m = jnp.max(s, axis=-1, keepdims=True)
        e = jnp.exp2(s - m)
        den = jnp.sum(e, axis=-1, keepdims=True)
        o = _dot(e.reshape(n_heads * BLK, n_keys).astype(_BF16), vw)
        return o, den.reshape(n_heads * BLK, 1), m.reshape(n_heads * BLK, 1)

    def block_a(q, kw, vw, bias):
        zero = jnp.zeros_like(q)
        qs = jnp.concatenate([jnp.where(head_a_bf == hh, q, zero) for hh in range(HEADS_A)], axis=0)
        o, den, m = softmax_block(qs, kw, vw, bias, HEADS_A)
        acc = o[:BLK]
        den_u = jnp.broadcast_to(den[:BLK], (BLK, A_OUT))
        m_u = jnp.broadcast_to(m[:BLK], (BLK, A_OUT))
        for hh in range(1, HEADS_A):
            sel = lane_head_a == hh
            rows = slice(hh * BLK, (hh + 1) * BLK)
            acc = jnp.where(sel, o[rows], acc)
            den_u = jnp.where(sel, jnp.broadcast_to(den[rows], (BLK, A_OUT)), den_u)
            m_u = jnp.where(sel, jnp.broadcast_to(m[rows], (BLK, A_OUT)), m_u)
        return acc, den_u, m_u

    def store(g, start, stride, acc, den_u, m_u):
        idx = pl.ds(start, BLK, stride=stride)
        for s in range(A_OUT // LANES):
            sl = slice(s * LANES, (s + 1) * LANES)
            acc_s[g, s, idx, :] = acc[:, sl]
            den_s[g, s, idx, :] = den_u[:, sl]
            m_s[g, s, idx, :] = m_u[:, sl]

    def classes_body(i, carry):
        r = lax.div(i, per_class2)
        sb = i - r * per_class2
        s0 = pl.multiple_of(sb * BLK, BLK)
        res = block_a(q2_ref[r, pl.ds(s0, BLK)], window(k2_ref, (r,), s0), window(v2_ref, (r,), s0),
                      with_prev(0, sb > 0))
        store(0, r + dil2 * s0, dil2, *res)
        res = block_a(q3_ref[i], k3_ref[i], v3_ref[i], causal)
        store(1, i, dil3, *res)
        return carry

    lax.fori_loop(0, n_blk, classes_body, 0, unroll=True)

    lo = lane_head_b == 0
    sink_lanes = sinkl_ref[...]

    def tokens_body(b, carry):
        r0 = pl.multiple_of(b * BLK, BLK)
        acc1, den1, m1 = block_a(q1_ref[pl.ds(r0, BLK)], window(k1_ref, (), r0),
                                 window(v1_ref, (), r0), with_prev(0, b > 0))
        cols = []
        for s in range(A_OUT // LANES):
            sl = slice(s * LANES, (s + 1) * LANES)
            rows = pl.ds(r0, BLK)
            ms = [m1[:, sl], m_s[0, s, rows, :], m_s[1, s, rows, :]]
            dens = [den1[:, sl], den_s[0, s, rows, :], den_s[1, s, rows, :]]
            accs = [acc1[:, sl], acc_s[0, s, rows, :], acc_s[1, s, rows, :]]
            top = jnp.maximum(jnp.maximum(ms[0], ms[1]), ms[2])
            ws = [jnp.exp2(mm - top) for mm in ms]
            num = ws[0] * accs[0] + ws[1] * accs[1] + ws[2] * accs[2]
            dn = ws[0] * dens[0] + ws[1] * dens[1] + ws[2] * dens[2]
            cols.append(num / dn)
        oa_ref[pl.ds(r0, BLK), :] = jnp.concatenate(cols, axis=1).astype(oa_ref.dtype)

        q = qb_ref[pl.ds(r0, BLK)]
        zero = jnp.zeros((BLK, LANES), q.dtype)
        parts = []
        for c in range(B_Q // LANES):
            qc = q[:, c * LANES:(c + 1) * LANES]
            parts += [jnp.where(head_b_bf == 0, qc, zero), jnp.where(head_b_bf == 1, qc, zero)]
        o, den, m = softmax_block(jnp.concatenate(parts, axis=0), window(kb_ref, (), r0),
                                  window(vb_ref, (), r0), with_prev(1, b > 0), N_HEADS_B)
        cols = []
        for c in range(B_Q // LANES):
            ra = slice(2 * c * BLK, (2 * c + 1) * BLK)
            rb = slice((2 * c + 1) * BLK, (2 * c + 2) * BLK)
            o_c = jnp.where(lo, o[ra], o[rb])
            den_c = jnp.where(lo, jnp.broadcast_to(den[ra], (BLK, LANES)),
                              jnp.broadcast_to(den[rb], (BLK, LANES)))
            m_c = jnp.where(lo, jnp.broadcast_to(m[ra], (BLK, LANES)),
                            jnp.broadcast_to(m[rb], (BLK, LANES)))
            sink_c = sink_lanes[:, c * LANES:(c + 1) * LANES]
            cols.append(o_c / (den_c + jnp.exp2(sink_c - m_c)))
        ob_ref[pl.ds(r0, BLK), :] = jnp.concatenate(cols, axis=1).astype(ob_ref.dtype)
        return carry

    lax.fori_loop(0, n_blk, tokens_body, 0, unroll=True)


def _attn_prompt(sinks_p, qkv, n_batch, seq):
    (q1, k1, v1, q2, k2, v2, q3, k3, v3, qb, kb, vb) = qkv
    tok = lambda w: pl.BlockSpec((seq, w), lambda n: (n, 0))
    cls = lambda dil: pl.BlockSpec((None, dil, seq // dil, A_OUT), lambda n: (n, 0, 0, 0))
    return pl.pallas_call(
        _attn_prompt_kernel,
        out_shape=[jax.ShapeDtypeStruct((n_batch * seq, A_OUT), _BF16),
                   jax.ShapeDtypeStruct((n_batch * seq, B_Q), _BF16)],
        grid=(n_batch,),
        in_specs=[_resident((1, B_Q))]
        + [tok(A_OUT)] * 3 + [cls(4)] * 3 + [cls(16)] * 3 + [tok(B_Q), tok(B_KV), tok(B_KV)],
        out_specs=[tok(A_OUT), tok(B_Q)],
        scratch_shapes=[pltpu.VMEM((N_GROUPS_A - 1, A_OUT // LANES, seq, LANES), _F32)] * 3
        + [pltpu.VMEM((4, BLK, 2 * BLK), _F32)],
        compiler_params=_cparams(1),
        name="attn_prompt",
    )(jnp.repeat(sinks_p * LOG2_E, HEAD_DIM)[None, :], q1, k1, v1, q2, k2, v2, q3, k3, v3,
      qb, kb, vb)


def _attn_sample_body(sink_ref, q_ref, kvt_ref, cache_refs, oa_ref, ob_ref, new_refs, n, t_new):
    c1_ref, c2_ref, c3_ref, cb_ref = cache_refs
    n1_ref, n2_ref, n3_ref, nb_ref = new_refs
    per_tile = LANES // t_new
    first_new = LANES - t_new
    shift = first_new - (n & (per_tile - 1)) * t_new
    new_t = pltpu.roll(kvt_ref[...], shift, 1)
    q = q_ref[...]

    def new_cache(old_ref, new_rows, out_ref):
        lb = old_ref.shape[1]
        rolled = pltpu.roll(old_ref[...], lb - t_new, 1)
        if lb > LANES:
            out_ref[:, :lb - LANES] = rolled[:, :lb - LANES]
        lane = lax.broadcasted_iota(jnp.int32, (old_ref.shape[0], LANES), 1)
        out_ref[:, lb - LANES:] = jnp.where(lane >= first_new, new_rows, rolled[:, lb - LANES:])

    def masks(rows, lb, window, dil, strict):
        t_old = lax.broadcasted_iota(jnp.int32, (rows, lb), 0) & (t_new - 1)
        dist = lb + t_old - lax.broadcasted_iota(jnp.int32, (rows, lb), 1)
        hi = (dist < window) if strict else (dist <= window)
        valid_old = (dist >= 0) & hi & ((dist & (dil - 1)) == 0)
        t_nw = lax.broadcasted_iota(jnp.int32, (rows, LANES), 0) & (t_new - 1)
        c_nw = lax.broadcasted_iota(jnp.int32, (rows, LANES), 1) - first_new
        dn = t_nw - c_nw
        hin = (dn < window) if strict else (dn <= window)
        valid_new = (c_nw >= 0) & (dn >= 0) & hin & ((dn & (dil - 1)) == 0)
        return valid_old, valid_new

    def attend(qs, k_old, v_old, k_new, v_new, valid_old, valid_new):
        k = jnp.concatenate([k_old, k_new], axis=1)
        v = jnp.concatenate([v_old, v_new], axis=1)
        s = jnp.where(jnp.concatenate([valid_old, valid_new], axis=1), _dot(qs, k), NEG)
        m = jnp.max(s, axis=-1, keepdims=True)
        e = jnp.exp(s - m)
        den = jnp.sum(e, axis=-1, keepdims=True)
        return _dot_nt(e.astype(_BF16), v), den, m

    def classes_reduce(x, dil, op):
        sh = dil
        while sh < LANES:
            x = op(x, pltpu.roll(x, sh, 1))
            sh *= 2
        return x

    def fold_tiles(x, op):
        acc = x[:, :LANES]
        for jt in range(1, x.shape[1] // LANES):
            acc = op(acc, x[:, jt * LANES:(jt + 1) * LANES])
        return acc

    def attend_lane_classes(old_ref, new_g, q_t, window, dil):
        lb = old_ref.shape[1]
        n_tiles = lb // LANES
        assert t_new <= dil and lb % dil == 0 and LANES % dil == 0 and lb <= window
        lane_q = lax.broadcasted_iota(jnp.int32, (A_OUT, LANES), 1)
        q_new = jnp.where(lane_q >= first_new, q_t, 0.0)
        q_cls = classes_reduce(pltpu.roll(q_new, t_new, 1), dil, jnp.add)
        lane_1 = lax.broadcasted_iota(jnp.int32, (1, LANES), 1)
        cls_ok = jnp.concatenate([(lane_1 & (dil - 1)) < t_new] * n_tiles, axis=1)
        o_rows, lse_rows = [], []
        for hh in range(HEADS_A):
            rows = slice(hh * HEAD_DIM, (hh + 1) * HEAD_DIM)
            q_h = jnp.concatenate([q_cls[rows]] * n_tiles, axis=1)
            s_old = jnp.sum(old_ref[rows, :] * q_h, axis=0, keepdims=True)
            s_old = jnp.where(cls_ok, s_old, NEG)
            s_new = jnp.sum(new_g[rows] * q_new[rows], axis=0, keepdims=True)
            s_new = jnp.where(lane_1 < t_new, pltpu.roll(s_new, t_new, 1), NEG)
            m = classes_reduce(jnp.maximum(fold_tiles(s_old, jnp.maximum), s_new), dil, jnp.maximum)
            e_old = jnp.where(cls_ok, jnp.exp(s_old - jnp.concatenate([m] * n_tiles, axis=1)), 0.0)
            e_new = jnp.where(lane_1 < t_new, jnp.exp(s_new - m), 0.0)
            den = classes_reduce(fold_tiles(e_old, jnp.add) + e_new, dil, jnp.add)
            den = jnp.where((lane_1 & (dil - 1)) < t_new, den, 1.0)
            v_rows = slice(A_OUT + hh * HEAD_DIM, A_OUT + (hh + 1) * HEAD_DIM)
            acc = fold_tiles(old_ref[v_rows, :] * e_old, jnp.add)
            acc = acc + pltpu.roll(new_g[v_rows], t_new, 1) * e_new
            acc = classes_reduce(acc, dil, jnp.add)
            o_rows.append(acc / den)
            lse_rows.append(jnp.broadcast_to(m + jnp.log(den), (HEAD_DIM, LANES)))
        out_t = jnp.concatenate(o_rows, axis=0)
        lse_t = jnp.concatenate(lse_rows, axis=0)
        return out_t.T[:t_new], lse_t.T[:t_new]

    outs, lses = [], []
    for g, (old_ref, out_ref) in enumerate(((c1_ref, n1_ref), (c2_ref, n2_ref), (c3_ref, n3_ref))):
        window, dil = DIL_PAIRS[g]
        lb = old_ref.shape[1]
        new_g = new_t[g * KV_ROWS_A:(g + 1) * KV_ROWS_A]
        if g in VPU_GROUPS:
            q_row0 = KVT_ROWS + VPU_GROUPS.index(g) * A_OUT
            out, lse = attend_lane_classes(old_ref, new_g, new_t[q_row0:q_row0 + A_OUT],
                                           window, dil)
            outs.append(out)
            lses.append(lse)
            new_cache(old_ref, new_g, out_ref)
            continue
        q0 = MXU_GROUPS.index(g) * A_OUT
        qs = _head_stack_a(q[:, q0:q0 + A_OUT]).astype(_BF16)
        valid_old, valid_new = masks(HEADS_A * t_new, lb, window, dil, False)
        o, den, m = attend(qs, old_ref[0:A_OUT, :].astype(_BF16), old_ref[A_OUT:, :].astype(_BF16),
                           new_g[:A_OUT].astype(_BF16), new_g[A_OUT:].astype(_BF16),
                           valid_old, valid_new)
        outs.append(_head_unstack_a(o, 1.0 / den, t_new))
        lses.append(_head_unstack_a(None, m + jnp.log(den), t_new))
        new_cache(old_ref, new_g, out_ref)
    m = jnp.maximum(jnp.maximum(lses[0], lses[1]), lses[2])
    ws = [jnp.exp(l - m) for l in lses]
    oa_ref[...] = (ws[0] * outs[0] + ws[1] * outs[1] + ws[2] * outs[2]) / (ws[0] + ws[1] + ws[2])

    lb = cb_ref.shape[1]
    new_b = new_t[N_GROUPS_A * KV_ROWS_A:KVT_ROWS]
    qs = _head_stack_b(q[:, len(MXU_GROUPS) * A_OUT:]).astype(_BF16)
    valid_old, valid_new = masks(N_HEADS_B * t_new, lb, WINDOW_B, 1, True)
    o, den, m = attend(qs, cb_ref[0:B_KV, :].astype(_BF16), cb_ref[B_KV:, :].astype(_BF16),
                       new_b[:B_KV].astype(_BF16), new_b[B_KV:].astype(_BF16),
                       valid_old, valid_new)
    scale = _sigmoid(m + jnp.log(den) - _sink_rows(sink_ref, t_new)) / den
    ob_ref[...] = _head_unstack_b(o, scale, t_new)
    new_cache(cb_ref, new_b, nb_ref)


def _mix_out_tile(x, g_ref, oa_ref, ob_ref, wga_ref, wgb_ref, wba_ref, wbb_ref, wout_ref):
    h = _rms(x, g_ref[...]).astype(_BF16)
    gate_a = _dot(h, wga_ref[...])
    gate_b = _dot(h, wgb_ref[...])
    ya = _dot(oa_ref[...].astype(_BF16), wba_ref[...])
    yb = _dot(ob_ref[...].astype(_BF16), wbb_ref[...])
    merged = _sigmoid(gate_a) * ya + _sigmoid(gate_b) * yb
    return x + _dot(merged.astype(_BF16), wout_ref[...])


def _mix_out_kernel(x_ref, *refs, n_mix, bounds, final_norm):
    y = _mix_out_tile(x_ref[...], *refs[:n_mix])
    if bounds is not None:
        y = _ffn_tile(y, *refs[n_mix:n_mix + 5], bounds, final_norm)
    refs[-1][...] = y


def _mix_out(x, layer, g, oa, ob, w_in, w_ba, w_bb, w_out, tm, ffn=None, final_norm=False):
    n_tok, d = x.shape
    assert QKV_COLS % d == 0 and w_in.shape[2] == QKV_COLS + 2 * d
    gate_specs = [_layer_block(w_in, layer, (d, d), (0, QKV_COLS // d + k)) for k in range(2)]
    tok = lambda w: pl.BlockSpec((tm, w), lambda i: (i, 0))
    mix_specs = ([_layer_block(g, layer), tok(A_OUT), tok(B_Q)] + gate_specs
                 + [_layer_block(a, layer) for a in (w_ba, w_bb, w_out)])
    mix_ops = [g, oa, ob, w_in, w_in, w_ba, w_bb, w_out]
    ffn_specs, ffn_ops, bounds = [], [], None
    if ffn is not None:
        ffn_specs, ffn_ops, bounds = _ffn_specs(ffn, layer, d), list(ffn), _ffn_bounds(ffn[1].shape[-1])
    return pl.pallas_call(
        functools.partial(_mix_out_kernel, n_mix=len(mix_ops), bounds=bounds,
                          final_norm=final_norm),
        out_shape=jax.ShapeDtypeStruct((n_tok, d), _F32),
        grid=(n_tok // tm,),
        in_specs=[tok(d)] + mix_specs + ffn_specs,
        out_specs=tok(d),
        compiler_params=_cparams(1),
        name="mix_out" if ffn is None else "sample_out",
    )(x, *mix_ops, *ffn_ops)


def _cache_view(c):
    dep, nb, rows = c.shape[:3]
    return jnp.transpose(c, (0, 1, 3, 4, 5, 2)).reshape(dep, nb, -1, rows)


def _cache_unview(c, heads):
    dep, nb, _, rows = c.shape
    return jnp.transpose(c.reshape(dep, nb, 2, heads, HEAD_DIM, rows), (0, 1, 5, 2, 3, 4))


def kernel(x_prompt, x_sample, cache_a1, cache_a2, cache_a3, cache_b, norm_ffn1, ffn1_gate, ffn1_up, ffn1_down, norm_mix, w_in, sinks, w_branch_a, w_branch_b, w_out, norm_ffn2, ffn2_gate, ffn2_up, ffn2_down, norm_final):
    n_batch, seq, d = x_prompt.shape
    dec_batch, t_new, _ = x_sample.shape
    depth = w_in.shape[0]
    n_sample = dec_batch * t_new
    tm_s = min(TM, n_sample)
    hosts_per_layer = 2
    n_hosted = dec_batch // hosts_per_layer
    tm_p = n_batch * seq // n_hosted
    assert seq == DIL_PAIRS[2][0] and LANES % t_new == 0 and n_sample % tm_s == 0
    assert tm_s % t_new == 0 and dec_batch % (LANES // t_new) == 0
    assert dec_batch % hosts_per_layer == 0 and (n_batch * seq) % n_hosted == 0 and tm_p % 8 == 0

    perm = jnp.array(B_HEAD_PERM)
    col_b = 3 * N_GROUPS_A * A_OUT

    cos_p, sin_p = _rope_tables(jnp.arange(seq))
    cos_s, sin_s = _rope_tables(PAST_LEN + jnp.arange(t_new))
    cos_s, sin_s = jnp.tile(cos_s, (tm_s // t_new, 1)), jnp.tile(sin_s, (tm_s // t_new, 1))
    nat_p, nat_s = _nat_tables(cos_p, sin_p), _nat_tables(cos_s, sin_s)
    t_p, t_s = (cos_p.T, sin_p.T), (cos_s.T, sin_s.T)

    caches = [_cache_view(c) for c in (cache_a1, cache_a2, cache_a3, cache_b)]

    bf = lambda w: w.astype(_BF16)
    w_in_bf = bf(w_in)
    cols = lambda lo, hi: w_in_bf[:, :, lo:hi]
    bq = cols(col_b, col_b + B_Q).reshape(depth, d, N_HEADS_B, HEAD_DIM)[:, :, perm]
    bq = bq.reshape(depth, d, B_Q)
    q_cols = [cols(3 * g * A_OUT, (3 * g + 1) * A_OUT) for g in range(N_GROUPS_A)]
    kv_cols = [cols((3 * g + 1) * A_OUT, (3 * g + 3) * A_OUT) for g in range(N_GROUPS_A)]
    kv_cols.append(cols(col_b + B_Q, QKV_COLS))
    w_kvt = jnp.swapaxes(jnp.concatenate(kv_cols + [q_cols[g] for g in VPU_GROUPS], axis=2), 1, 2)
    w_q = jnp.concatenate([q_cols[g] for g in MXU_GROUPS] + [bq], axis=2)
    w_bb = bf(w_branch_b).reshape(depth, N_HEADS_B, HEAD_DIM, d)[:, perm].reshape(depth, B_Q, d)
    w_ba, w_o = bf(w_branch_a), bf(w_out)
    sinks_perm = sinks[:, perm]
    row = lambda v: v.reshape(-1, 1, d)
    gf = norm_final.reshape(1, d)
    f1 = (row(norm_ffn1), bf(ffn1_gate), bf(ffn1_up), bf(ffn1_down), gf)
    f2 = (row(norm_ffn2), bf(ffn2_gate), bf(ffn2_up), bf(ffn2_down), gf)
    g_mix = row(norm_mix)

    xp = x_prompt.reshape(n_batch * seq, d)
    xs = x_sample.reshape(dec_batch * t_new, d)
    prompt_bufs = []
    sample_bufs = None
    for l in range(depth):
        last = l == depth - 1
        sinks_p = sinks_perm[l]

        xs, q_nat, kvt_new = _sample_in(xs, l, f1, g_mix, w_q, w_kvt, nat_s, t_s, tm_s)
        host = lambda n_base: dict(n_base=n_base, t_new=t_new, sinks_p=sinks_p, q_nat=q_nat,
                                   kvt_new=kvt_new, caches=caches, prev_out=sample_bufs)

        xp, oa_0, ob_0, *sample_bufs = _ffn(xp, l, f1, final_norm=False, tm=tm_p, **host(0))
        outs = _mix_in_prompt(xp, l, g_mix, w_in_bf, bq, w_kvt, nat_p, t_p, n_batch, seq,
                              prompt_bufs)
        prompt_bufs = list(outs[12:])
        oa_p, ob_p = _attn_prompt(sinks_p, outs[:12], n_batch, seq)
        xp = _mix_out(xp, l, g_mix, oa_p, ob_p, w_in_bf, w_ba, w_bb, w_o, TM)
        xp, oa_1, ob_1, *sample_bufs = _ffn(xp, l, f2, final_norm=last, tm=tm_p,
                                            **host(n_hosted))

        oa_s = jnp.concatenate([oa_0, oa_1], axis=0)
        ob_s = jnp.concatenate([ob_0, ob_1], axis=0)
        xs = _mix_out(xs, l, g_mix, oa_s, ob_s, w_in_bf, w_ba, w_bb, w_o, tm_s, ffn=f2,
                      final_norm=last)

    heads = (HEADS_A, HEADS_A, HEADS_A, N_KV_B)
    new_p = [_cache_unview(prompt_bufs[i], heads[i]) for i in range(4)]
    new_s = [_cache_unview(sample_bufs[i], heads[i]) for i in range(4)]
    return (xp.reshape(n_batch, seq, d), xs.reshape(dec_batch, t_new, d), *new_p, *new_s)
```

```python
import functools

import jax
import jax.numpy as jnp
from jax import lax
from jax.experimental import pallas as pl
from jax.experimental.pallas import tpu as pltpu

PAST_LEN = 16384
HEAD_DIM = 64
ROT_DIM = HEAD_DIM // 4
ROT_HALF = ROT_DIM // 2
ROPE_THETA = 500000.0
DIL_PAIRS = ((128, 1), (512, 4), (2048, 16))
N_GROUPS_A = 3
HEADS_A = 4
A_OUT = HEADS_A * HEAD_DIM
N_HEADS_B = 8
N_KV_B = 2
WINDOW_B = 128
B_Q = N_HEADS_B * HEAD_DIM
B_KV = N_KV_B * HEAD_DIM
QKV_COLS = N_GROUPS_A * 3 * A_OUT + B_Q + 2 * B_KV
KV_ROWS_A = 2 * A_OUT
KV_ROWS_B = 2 * B_KV
KVT_ROWS = N_GROUPS_A * KV_ROWS_A + KV_ROWS_B
VPU_GROUPS = (2,)
MXU_GROUPS = tuple(g for g in range(N_GROUPS_A) if g not in VPU_GROUPS)
KVT_ROWS_Q = KVT_ROWS + len(VPU_GROUPS) * A_OUT
NORM_EPS = 1e-6
NEG = -1e30
SCALE = HEAD_DIM ** -0.5
LOG2_E = 1.4426950408889634
BLK = 128
LANES = 128
MXU_TILE = 256
TM = 512
FFN_CHUNKS = 1
B_HEAD_PERM = (0, 4, 1, 5, 2, 6, 3, 7)
V7X_SCOPED_VMEM_BYTES = 60000 * 1024

_BF16 = jnp.bfloat16
_F32 = jnp.float32


def _cparams(n_axes):
    return pltpu.CompilerParams(
        dimension_semantics=("arbitrary",) * n_axes,
        vmem_limit_bytes=V7X_SCOPED_VMEM_BYTES,
    )


def _resident(shape):
    nd = len(shape)
    return pl.BlockSpec(shape, lambda *_: (0,) * nd, pipeline_mode=pl.Buffered(1))


def _layer_block(arr, layer, block=None, index=None):
    block = tuple(arr.shape[1:]) if block is None else tuple(block)
    index = (0,) * len(block) if index is None else tuple(index)
    return pl.BlockSpec((None,) + block, lambda *_: (layer,) + index,
                        pipeline_mode=pl.Buffered(1))


def _rms(x, g):
    return x * lax.rsqrt(jnp.mean(x * x, axis=-1, keepdims=True) + NORM_EPS) * g


def _sigmoid(x):
    return 1.0 / (1.0 + jnp.exp(-x))


def _dot(a, b):
    return jnp.dot(a, b, preferred_element_type=_F32)


def _dot_nt(a, b):
    return lax.dot_general(a, b, (((1,), (1,)), ((), ())), preferred_element_type=_F32)


def _ffn_tile(x, g_ref, wg_ref, wu_ref, wd_ref, gf_ref, bounds, final_norm):
    h = _rms(x, g_ref[...]).astype(_BF16)
    acc = None
    for lo, hi in zip(bounds[:-1], bounds[1:]):
        gate = _dot(h, wg_ref[:, lo:hi])
        up = _dot(h, wu_ref[:, lo:hi])
        act = (gate * _sigmoid(gate) * up).astype(_BF16)
        part = _dot(act, wd_ref[lo:hi, :])
        acc = part if acc is None else acc + part
    y = x + 0.5 * acc
    return _rms(y, gf_ref[...]) if final_norm else y


def _ffn_kernel(x_ref, g_ref, wg_ref, wu_ref, wd_ref, gf_ref, sink_ref, q_ref, kvt_ref, *rest,
                bounds, final_norm, n_base, t_new):
    o_ref, oa_ref, ob_ref = rest[-7:-4]
    _attn_sample_body(sink_ref, q_ref, kvt_ref, rest[:4], oa_ref, ob_ref, rest[-4:],
                      n_base + pl.program_id(0), t_new)
    o_ref[...] = _ffn_tile(x_ref[...], g_ref, wg_ref, wu_ref, wd_ref, gf_ref, bounds, final_norm)


def _ffn_bounds(f):
    n_tiles = f // MXU_TILE
    cuts = [0, (n_tiles + 1) // 2 * MXU_TILE, f] if FFN_CHUNKS == 2 and n_tiles > 1 else [0, f]
    return tuple(cuts)


def _ffn_specs(ffn, layer, d):
    return [_layer_block(a, layer) for a in ffn[:4]] + [_resident((1, d))]


def _ffn(x, layer, ffn, *, final_norm, tm, n_base, t_new, sinks_p, q_nat, kvt_new, caches,
         prev_out):
    n_tok, d = x.shape
    steps = n_tok // tm
    per_tile = LANES // t_new
    tok_spec = pl.BlockSpec((tm, d), lambda i: (i, 0))
    cache_spec = lambda c: pl.BlockSpec((None, None) + c.shape[2:],
                                        lambda i: (layer, n_base + i, 0, 0))
    prev_out = [] if prev_out is None else list(prev_out)
    in_specs = ([tok_spec] + _ffn_specs(ffn, layer, d)
                + [pl.BlockSpec(memory_space=pltpu.SMEM),
                   pl.BlockSpec((t_new, q_nat.shape[1]), lambda i: (n_base + i, 0)),
                   pl.BlockSpec((KVT_ROWS_Q, LANES), lambda i: (0, (n_base + i) // per_tile))]
                + [cache_spec(c) for c in caches]
                + [pl.BlockSpec(memory_space=pl.ANY)] * len(prev_out))
    operands = [x, *ffn, sinks_p, q_nat, kvt_new, *caches, *prev_out]
    first_prev = len(operands) - len(prev_out)
    aliases = {first_prev + k: 3 + k for k in range(len(prev_out))}
    out_shape = ([jax.ShapeDtypeStruct((n_tok, d), _F32),
                  jax.ShapeDtypeStruct((steps * t_new, A_OUT), _F32),
                  jax.ShapeDtypeStruct((steps * t_new, B_Q), _F32)]
                 + [jax.ShapeDtypeStruct(c.shape, c.dtype) for c in caches])
    out_specs = ([tok_spec, pl.BlockSpec((t_new, A_OUT), lambda i: (i, 0)),
                  pl.BlockSpec((t_new, B_Q), lambda i: (i, 0))]
                 + [cache_spec(c) for c in caches])
    kern = functools.partial(_ffn_kernel, bounds=_ffn_bounds(ffn[1].shape[-1]),
                             final_norm=final_norm, n_base=n_base, t_new=t_new)
    return pl.pallas_call(
        kern,
        out_shape=out_shape,
        grid=(steps,),
        in_specs=in_specs,
        out_specs=out_specs,
        input_output_aliases=aliases,
        compiler_params=_cparams(1),
        name=("ffn_final" if final_norm else "ffn") + "_attn_sample",
    )(*operands)


def _rope_nat(x, c, s_lo, s_hi):
    return x * c + pltpu.roll(x, ROT_HALF, 1) * s_hi + pltpu.roll(x, LANES - ROT_HALF, 1) * s_lo


def _rope_t(k_t, cos_t, sin_t, n_heads):
    parts = []
    for hh in range(n_heads):
        b = hh * HEAD_DIM
        x1 = k_t[b:b + ROT_HALF]
        x2 = k_t[b + ROT_HALF:b + ROT_DIM]
        parts += [x1 * cos_t - x2 * sin_t, x2 * cos_t + x1 * sin_t, k_t[b + ROT_DIM:b + HEAD_DIM]]
    return jnp.concatenate(parts, axis=0)


def _rope_tables(pos):
    inv = ROPE_THETA ** (-(jnp.arange(ROT_HALF, dtype=_F32) * 2.0 / ROT_DIM))
    ang = pos.astype(_F32)[:, None] * inv[None, :]
    return jnp.cos(ang), jnp.sin(ang)


def _nat_tables(cos, sin):
    p = cos.shape[0]
    one = jnp.ones((p, HEAD_DIM - ROT_DIM), _F32)
    zero = jnp.zeros((p, HEAD_DIM - ROT_DIM), _F32)
    zh = jnp.zeros((p, ROT_HALF), _F32)
    c = jnp.concatenate([cos, cos, one], axis=1)
    s_hi = jnp.concatenate([zh, sin, zero], axis=1)
    s_lo = jnp.concatenate([-sin, zh, zero], axis=1)
    rep = LANES // HEAD_DIM
    return tuple(jnp.tile(t, (1, rep)) for t in (c, s_lo, s_hi))


def _mix_in_prompt_kernel(x_ref, g_ref, wa_ref, wbq_ref, wbkv_ref, wt_ref, c_ref, slo_ref, shi_ref,
                          ct_ref, st_ref, *rest, n_prev):
    prev = rest[:4] if n_prev else None
    (q1_ref, k1_ref, v1_ref, q2_ref, k2_ref, v2_ref, q3_ref, k3_ref, v3_ref,
     qb_ref, kb_ref, vb_ref, c1_ref, c2_ref, c3_ref, cb_ref, zs_ref) = rest[-17:]
    j = pl.program_id(1)
    tm = x_ref.shape[0]
    h = _rms(x_ref[...], g_ref[...]).astype(_BF16)
    c, s_lo, s_hi = c_ref[...], slo_ref[...], shi_ref[...]

    def proj(w_ref, col0, width, kind):
        z = _dot(h, w_ref[:, col0:col0 + width])
        if kind == "v":
            return z
        slabs = []
        for s in range(width // LANES):
            zz = _rope_nat(z[:, s * LANES:(s + 1) * LANES], c, s_lo, s_hi)
            slabs.append(zz * (SCALE * LOG2_E) if kind == "q" else zz)
        return jnp.concatenate(slabs, axis=1)

    for i, (ref, kind) in enumerate(((q1_ref, "q"), (k1_ref, "k"), (v1_ref, "v"))):
        ref[...] = proj(wa_ref, i * A_OUT, A_OUT, kind).astype(_BF16)
    for g, refs in ((1, (q2_ref, k2_ref, v2_ref)), (2, (q3_ref, k3_ref, v3_ref))):
        dil = DIL_PAIRS[g][1]
        rows = tm // dil
        for i, (ref, kind) in enumerate(zip(refs, ("q", "k", "v"))):
            z = proj(wa_ref, (3 * g + i) * A_OUT, A_OUT, kind)
            for s in range(A_OUT // LANES):
                zs_ref[s] = z[:, s * LANES:(s + 1) * LANES]
            for r in range(dil):
                for s in range(A_OUT // LANES):
                    ref[r, :, s * LANES:(s + 1) * LANES] = (
                        zs_ref[s, pl.ds(r, rows, stride=dil), :].astype(_BF16))
    qb_ref[...] = proj(wbq_ref, 0, B_Q, "q").astype(_BF16)
    z = _dot(h, wbkv_ref[...])
    kb_ref[...] = _rope_nat(z[:, :B_KV], c, s_lo, s_hi).astype(_BF16)
    vb_ref[...] = z[:, B_KV:].astype(_BF16)

    def kv_t(row0, n_rows, h_tok, cos_t, sin_t, n_heads):
        z = _dot_nt(wt_ref[row0:row0 + n_rows, :], h_tok)
        half = n_rows // 2
        return jnp.concatenate([_rope_t(z[:half], cos_t, sin_t, n_heads), z[half:]], axis=0)

    def put(i, out_ref, new):
        if n_prev:
            out_ref[:n_prev] = prev[i][...]
        out_ref[n_prev] = new

    ct, st = ct_ref[...], st_ref[...]
    put(2, c3_ref, kv_t(2 * KV_ROWS_A, KV_ROWS_A, h, ct, st, HEADS_A))

    @pl.when(j == pl.num_programs(1) - 1)
    def _():
        put(1, c2_ref, kv_t(KV_ROWS_A, KV_ROWS_A, h, ct, st, HEADS_A))
        tail = tm - BLK
        put(0, c1_ref, kv_t(0, KV_ROWS_A, h[tail:], ct[:, tail:], st[:, tail:], HEADS_A))
        put(3, cb_ref, kv_t(3 * KV_ROWS_A, KV_ROWS_B, h[tail:], ct[:, tail:], st[:, tail:], N_KV_B))


def _mix_in_prompt(x, layer, g, w_in, w_bq, w_kvt, nat_tabs, t_tabs, n_batch, seq, prev_bufs):
    d = x.shape[1]
    n_j = seq // TM
    n_prev = prev_bufs[0].shape[0] if prev_bufs else 0
    col_b = 3 * N_GROUPS_A * A_OUT
    assert (col_b + B_Q) % (2 * B_KV) == 0
    row = lambda n, j: (n * n_j + j, 0)
    tok_spec = lambda w: pl.BlockSpec((TM, w), row)
    cls_spec = lambda dil: pl.BlockSpec((None, dil, TM // dil, A_OUT), lambda n, j: (n, 0, j, 0))
    nat = lambda w: jax.ShapeDtypeStruct((n_batch * seq, w), _BF16)
    cls = lambda dil: jax.ShapeDtypeStruct((n_batch, dil, seq // dil, A_OUT), _BF16)
    keep = [min(w, seq) for w, _ in DIL_PAIRS]
    assert keep == [BLK, TM, seq] and seq % TM == 0
    bufs = ((KV_ROWS_A, keep[0], BLK, lambda j: 0), (KV_ROWS_A, keep[1], TM, lambda j: 0),
            (KV_ROWS_A, keep[2], TM, lambda j: j), (KV_ROWS_B, min(WINDOW_B, seq), BLK, lambda j: 0))

    def buf_spec(layers, rows, width, jmap):
        return pl.BlockSpec((layers, None, rows, width), lambda n, j: (0, n, 0, jmap(j)))

    def buf_shape(layers, rows, kept):
        return jax.ShapeDtypeStruct((layers, n_batch, rows, kept), _F32)

    n_out = n_prev + 1
    out_shape = ([nat(A_OUT)] * 3 + [cls(4)] * 3 + [cls(16)] * 3 + [nat(B_Q), nat(B_KV), nat(B_KV)]
                 + [buf_shape(n_out, r, k) for r, k, _, _ in bufs])
    out_specs = ([tok_spec(A_OUT)] * 3 + [cls_spec(4)] * 3 + [cls_spec(16)] * 3
                 + [tok_spec(B_Q), tok_spec(B_KV), tok_spec(B_KV)]
                 + [buf_spec(n_out, r, w, jm) for r, _, w, jm in bufs])
    prev_specs = [buf_spec(n_prev, r, w, jm) for r, _, w, jm in bufs] if n_prev else []
    tab_spec = pl.BlockSpec((TM, LANES), lambda n, j: (j, 0))
    ttab_spec = pl.BlockSpec((ROT_HALF, TM), lambda n, j: (0, j))
    return pl.pallas_call(
        functools.partial(_mix_in_prompt_kernel, n_prev=n_prev),
        out_shape=out_shape,
        grid=(n_batch, n_j),
        in_specs=[tok_spec(d), _layer_block(g, layer),
                  _layer_block(w_in, layer, (d, col_b), (0, 0)), _layer_block(w_bq, layer),
                  _layer_block(w_in, layer, (d, 2 * B_KV), (0, (col_b + B_Q) // (2 * B_KV))),
                  _layer_block(w_kvt, layer)]
        + [tab_spec, tab_spec, tab_spec, ttab_spec, ttab_spec] + prev_specs,
        out_specs=out_specs,
        scratch_shapes=[pltpu.VMEM((A_OUT // LANES, TM, LANES), _F32)],
        compiler_params=_cparams(2),
        name="mix_in_prompt",
    )(x, g, w_in, w_bq, w_in, w_kvt, *nat_tabs, *t_tabs, *prev_bufs)


def _sample_in_kernel(x_ref, gf1_ref, wg_ref, wu_ref, wd_ref, gf_ref,
                      g_ref, w_ref, wt_ref, c_ref, slo_ref, shi_ref, ct_ref, st_ref,
                      o_ref, q_ref, kvt_ref, *, bounds):
    x = _ffn_tile(x_ref[...], gf1_ref, wg_ref, wu_ref, wd_ref, gf_ref, bounds, False)
    o_ref[...] = x
    h = _rms(x, g_ref[...]).astype(_BF16)
    c, s_lo, s_hi = c_ref[...], slo_ref[...], shi_ref[...]
    z = _dot(h, w_ref[...])
    for s in range(z.shape[1] // LANES):
        sl = slice(s * LANES, (s + 1) * LANES)
        q_ref[:, sl] = _rope_nat(z[:, sl], c, s_lo, s_hi) * SCALE
    ct, st = ct_ref[...], st_ref[...]
    for g in range(N_GROUPS_A):
        r0 = g * KV_ROWS_A
        zt = _dot_nt(wt_ref[r0:r0 + KV_ROWS_A, :], h)
        kvt_ref[r0:r0 + A_OUT, :] = _rope_t(zt[:A_OUT], ct, st, HEADS_A)
        kvt_ref[r0 + A_OUT:r0 + KV_ROWS_A, :] = zt[A_OUT:]
    r0 = N_GROUPS_A * KV_ROWS_A
    zt = _dot_nt(wt_ref[r0:r0 + KV_ROWS_B, :], h)
    kvt_ref[r0:r0 + B_KV, :] = _rope_t(zt[:B_KV], ct, st, N_KV_B)
    kvt_ref[r0 + B_KV:r0 + KV_ROWS_B, :] = zt[B_KV:]
    for i in range(len(VPU_GROUPS)):
        r0 = KVT_ROWS + i * A_OUT
        zt = _dot_nt(wt_ref[r0:r0 + A_OUT, :], h)
        kvt_ref[r0:r0 + A_OUT, :] = _rope_t(zt, ct, st, HEADS_A) * SCALE


def _sample_in(x, layer, ffn, g, w_q, w_kvt, nat_tabs, t_tabs, tm):
    n_tok, d = x.shape
    qw = w_q.shape[-1]
    tok_spec = pl.BlockSpec((tm, d), lambda i: (i, 0))
    tab_spec = pl.BlockSpec((tm, LANES), lambda i: (0, 0))
    ttab_spec = pl.BlockSpec((ROT_HALF, tm), lambda i: (0, 0))
    assert w_kvt.shape[1] == KVT_ROWS_Q
    return pl.pallas_call(
        functools.partial(_sample_in_kernel, bounds=_ffn_bounds(ffn[1].shape[-1])),
        out_shape=[jax.ShapeDtypeStruct((n_tok, d), _F32),
                   jax.ShapeDtypeStruct((n_tok, qw), _F32),
                   jax.ShapeDtypeStruct((KVT_ROWS_Q, n_tok), _F32)],
        grid=(n_tok // tm,),
        in_specs=[tok_spec] + _ffn_specs(ffn, layer, d)
        + [_layer_block(a, layer) for a in (g, w_q, w_kvt)]
        + [tab_spec, tab_spec, tab_spec, ttab_spec, ttab_spec],
        out_specs=[tok_spec, pl.BlockSpec((tm, qw), lambda i: (i, 0)),
                   pl.BlockSpec((KVT_ROWS_Q, tm), lambda i: (0, i))],
        compiler_params=_cparams(1),
        name="sample_in",
    )(x, *ffn, g, w_q, w_kvt, *nat_tabs, *t_tabs)


def _lane_head(t, width):
    return lax.shift_right_logical(lax.broadcasted_iota(jnp.int32, (t, width), 1),
                                   HEAD_DIM.bit_length() - 1)


def _head_stack_a(q):
    lane_head = _lane_head(q.shape[0], A_OUT)
    return jnp.concatenate([jnp.where(lane_head == hh, q, 0.0) for hh in range(HEADS_A)], axis=0)


def _head_unstack_a(o_st, col_st, t):
    lane_head = _lane_head(t, A_OUT)
    out = jnp.zeros((t, A_OUT), _F32)
    for hh in range(HEADS_A):
        col = col_st[hh * t:(hh + 1) * t]
        blk = jnp.broadcast_to(col, (t, A_OUT)) if o_st is None else o_st[hh * t:(hh + 1) * t] * col
        out = jnp.where(lane_head == hh, blk, out)
    return out


def _head_stack_b(q):
    t = q.shape[0]
    lo = lax.broadcasted_iota(jnp.int32, (t, LANES), 1) < HEAD_DIM
    parts = []
    for c in range(B_Q // LANES):
        qc = q[:, c * LANES:(c + 1) * LANES]
        parts += [jnp.where(lo, qc, 0.0), jnp.where(lo, 0.0, qc)]
    return jnp.concatenate(parts, axis=0)


def _head_unstack_b(o_st, col_st, t):
    lo = lax.broadcasted_iota(jnp.int32, (t, LANES), 1) < HEAD_DIM
    cols = []
    for c in range(B_Q // LANES):
        r0 = 2 * c * t
        a = o_st[r0:r0 + t] * col_st[r0:r0 + t]
        b = o_st[r0 + t:r0 + 2 * t] * col_st[r0 + t:r0 + 2 * t]
        cols.append(jnp.where(lo, a, b))
    return jnp.concatenate(cols, axis=1)


def _sink_rows(sink_ref, t):
    return jnp.concatenate(
        [jnp.full((t, 1), sink_ref[p], _F32) for p in range(N_HEADS_B)], axis=0)


def _attn_prompt_kernel(sinkl_ref, q1_ref, k1_ref, v1_ref, q2_ref, k2_ref, v2_ref,
                        q3_ref, k3_ref, v3_ref, qb_ref, kb_ref, vb_ref,
                        oa_ref, ob_ref, acc_s, den_s, m_s, bias_s):
    seq = q1_ref.shape[0]
    n_blk = seq // BLK
    dil2, dil3 = DIL_PAIRS[1][1], DIL_PAIRS[2][1]
    per_class2 = seq // dil2 // BLK
    assert dil2 * per_class2 == dil3 == n_blk and seq // dil3 == BLK

    lane_head_a = _lane_head(BLK, A_OUT)
    head_a_bf = lane_head_a.astype(_F32).astype(_BF16)
    lane_head_b = _lane_head(BLK, LANES)
    head_b_bf = lane_head_b.astype(_F32).astype(_BF16)
    r_i = lax.broadcasted_iota(jnp.int32, (BLK, 2 * BLK), 0)
    c_i = lax.broadcasted_iota(jnp.int32, (BLK, 2 * BLK), 1)
    dist = BLK + r_i - c_i
    for i, n_back in enumerate((BLK, WINDOW_B - 1)):
        band = (dist >= 0) & (dist <= n_back)
        bias_s[2 * i] = jnp.where(band & (c_i >= BLK), 0.0, NEG)
        bias_s[2 * i + 1] = jnp.where(band, 0.0, NEG)
    causal = jnp.where(lax.broadcasted_iota(jnp.int32, (BLK, BLK), 0)
                       >= lax.broadcasted_iota(jnp.int32, (BLK, BLK), 1), 0.0, NEG)

    def with_prev(i, has_prev):
        return bias_s[2 * i + jnp.where(has_prev, 1, 0)]

    def window(ref, lead, r0):
        p0 = pl.multiple_of(jnp.maximum(r0 - BLK, 0), BLK)
        return jnp.concatenate([ref[lead + (pl.ds(p0, BLK),)], ref[lead + (pl.ds(r0, BLK),)]],
                               axis=0)

    def softmax_block(qs, kw, vw, bias, n_heads):
        n_keys = kw.shape[0]
        s = _dot_nt(qs, kw).reshape(n_heads, BLK, n_keys) + bias[None]
        m = jnp.max(s, axis=-1, keepdims=True)
        e = jnp.exp2(s - m)
        den = jnp.sum(e, axis=-1, keepdims=True)
        o = _dot(e.reshape(n_heads * BLK, n_keys).astype(_BF16), vw)
        return o, den.reshape(n_heads * BLK, 1), m.reshape(n_heads * BLK, 1)

    def block_a(q, kw, vw, bias):
        zero = jnp.zeros_like(q)
        qs = jnp.concatenate([jnp.where(head_a_bf == hh, q, zero) for hh in range(HEADS_A)], axis=0)
        o, den, m = softmax_block(qs, kw, vw, bias, HEADS_A)
        acc = o[:BLK]
        den_u = jnp.broadcast_to(den[:BLK], (BLK, A_OUT))
        m_u = jnp.broadcast_to(m[:BLK], (BLK, A_OUT))
        for hh in range(1, HEADS_A):
            sel = lane_head_a == hh
            rows = slice(hh * BLK, (hh + 1) * BLK)
            acc = jnp.where(sel, o[rows], acc)
            den_u = jnp.where(sel, jnp.broadcast_to(den[rows], (BLK, A_OUT)), den_u)
            m_u = jnp.where(sel, jnp.broadcast_to(m[rows], (BLK, A_OUT)), m_u)
        return acc, den_u, m_u

    def store(g, start, stride, acc, den_u, m_u):
        idx = pl.ds(start, BLK, stride=stride)
        for s in range(A_OUT // LANES):
            sl = slice(s * LANES, (s + 1) * LANES)
            acc_s[g, s, idx, :] = acc[:, sl]
            den_s[g, s, idx, :] = den_u[:, sl]
            m_s[g, s, idx, :] = m_u[:, sl]

    def classes_body(i, carry):
        r = lax.div(i, per_class2)
        sb = i - r * per_class2
        s0 = pl.multiple_of(sb * BLK, BLK)
        res = block_a(q2_ref[r, pl.ds(s0, BLK)], window(k2_ref, (r,), s0), window(v2_ref, (r,), s0),
                      with_prev(0, sb > 0))
        store(0, r + dil2 * s0, dil2, *res)
        res = block_a(q3_ref[i], k3_ref[i], v3_ref[i], causal)
        store(1, i, dil3, *res)
        return carry

    lax.fori_loop(0, n_blk, classes_body, 0, unroll=True)

    lo = lane_head_b == 0
    sink_lanes = sinkl_ref[...]

    def tokens_body(b, carry):
        r0 = pl.multiple_of(b * BLK, BLK)
        acc1, den1, m1 = block_a(q1_ref[pl.ds(r0, BLK)], window(k1_ref, (), r0),
                                 window(v1_ref, (), r0), with_prev(0, b > 0))
        cols = []
        for s in range(A_OUT // LANES):
            sl = slice(s * LANES, (s + 1) * LANES)
            rows = pl.ds(r0, BLK)
            ms = [m1[:, sl], m_s[0, s, rows, :], m_s[1, s, rows, :]]
            dens = [den1[:, sl], den_s[0, s, rows, :], den_s[1, s, rows, :]]
            accs = [acc1[:, sl], acc_s[0, s, rows, :], acc_s[1, s, rows, :]]
            top = jnp.maximum(jnp.maximum(ms[0], ms[1]), ms[2])
            ws = [jnp.exp2(mm - top) for mm in ms]
            num = ws[0] * accs[0] + ws[1] * accs[1] + ws[2] * accs[2]
            dn = ws[0] * dens[0] + ws[1] * dens[1] + ws[2] * dens[2]
            cols.append(num / dn)
        oa_ref[pl.ds(r0, BLK), :] = jnp.concatenate(cols, axis=1).astype(oa_ref.dtype)

        q = qb_ref[pl.ds(r0, BLK)]
        zero = jnp.zeros((BLK, LANES), q.dtype)
        parts = []
        for c in range(B_Q // LANES):
            qc = q[:, c * LANES:(c + 1) * LANES]
            parts += [jnp.where(head_b_bf == 0, qc, zero), jnp.where(head_b_bf == 1, qc, zero)]
        o, den, m = softmax_block(jnp.concatenate(parts, axis=0), window(kb_ref, (), r0),
                                  window(vb_ref, (), r0), with_prev(1, b > 0), N_HEADS_B)
        cols = []
        for c in range(B_Q // LANES):
            ra = slice(2 * c * BLK, (2 * c + 1) * BLK)
            rb = slice((2 * c + 1) * BLK, (2 * c + 2) * BLK)
            o_c = jnp.where(lo, o[ra], o[rb])
            den_c = jnp.where(lo, jnp.broadcast_to(den[ra], (BLK, LANES)),
                              jnp.broadcast_to(den[rb], (BLK, LANES)))
            m_c = jnp.where(lo, jnp.broadcast_to(m[ra], (BLK, LANES)),
                            jnp.broadcast_to(m[rb], (BLK, LANES)))
            sink_c = sink_lanes[:, c * LANES:(c + 1) * LANES]
            cols.append(o_c / (den_c + jnp.exp2(sink_c - m_c)))
        ob_ref[pl.ds(r0, BLK), :] = jnp.concatenate(cols, axis=1).astype(ob_ref.dtype)
        return carry

    lax.fori_loop(0, n_blk, tokens_body, 0, unroll=True)


def _attn_prompt(sinks_p, qkv, n_batch, seq):
    (q1, k1, v1, q2, k2, v2, q3, k3, v3, qb, kb, vb) = qkv
    tok = lambda w: pl.BlockSpec((seq, w), lambda n: (n, 0))
    cls = lambda dil: pl.BlockSpec((None, dil, seq // dil, A_OUT), lambda n: (n, 0, 0, 0))
    return pl.pallas_call(
        _attn_prompt_kernel,
        out_shape=[jax.ShapeDtypeStruct((n_batch * seq, A_OUT), _BF16),
                   jax.ShapeDtypeStruct((n_batch * seq, B_Q), _BF16)],
        grid=(n_batch,),
        in_specs=[_resident((1, B_Q))]
        + [tok(A_OUT)] * 3 + [cls(4)] * 3 + [cls(16)] * 3 + [tok(B_Q), tok(B_KV), tok(B_KV)],
        out_specs=[tok(A_OUT), tok(B_Q)],
        scratch_shapes=[pltpu.VMEM((N_GROUPS_A - 1, A_OUT // LANES, seq, LANES), _F32)] * 3
        + [pltpu.VMEM((4, BLK, 2 * BLK), _F32)],
        compiler_params=_cparams(1),
        name="attn_prompt",
    )(jnp.repeat(sinks_p * LOG2_E, HEAD_DIM)[None, :], q1, k1, v1, q2, k2, v2, q3, k3, v3,
      qb, kb, vb)


def _attn_sample_body(sink_ref, q_ref, kvt_ref, cache_refs, oa_ref, ob_ref, new_refs, n, t_new):
    c1_ref, c2_ref, c3_ref, cb_ref = cache_refs
    n1_ref, n2_ref, n3_ref, nb_ref = new_refs
    per_tile = LANES // t_new
    first_new = LANES - t_new
    shift = first_new - (n & (per_tile - 1)) * t_new
    new_t = pltpu.roll(kvt_ref[...], shift, 1)
    q = q_ref[...]

    def new_cache(old_ref, new_rows, out_ref):
        lb = old_ref.shape[1]
        rolled = pltpu.roll(old_ref[...], lb - t_new, 1)
        if lb > LANES:
            out_ref[:, :lb - LANES] = rolled[:, :lb - LANES]
        lane = lax.broadcasted_iota(jnp.int32, (old_ref.shape[0], LANES), 1)
        out_ref[:, lb - LANES:] = jnp.where(lane >= first_new, new_rows, rolled[:, lb - LANES:])

    def masks(rows, lb, window, dil, strict):
        t_old = lax.broadcasted_iota(jnp.int32, (rows, lb), 0) & (t_new - 1)
        dist = lb + t_old - lax.broadcasted_iota(jnp.int32, (rows, lb), 1)
        hi = (dist < window) if strict else (dist <= window)
        valid_old = (dist >= 0) & hi & ((dist & (dil - 1)) == 0)
        t_nw = lax.broadcasted_iota(jnp.int32, (rows, LANES), 0) & (t_new - 1)
        c_nw = lax.broadcasted_iota(jnp.int32, (rows, LANES), 1) - first_new
        dn = t_nw - c_nw
        hin = (dn < window) if strict else (dn <= window)
        valid_new = (c_nw >= 0) & (dn >= 0) & hin & ((dn & (dil - 1)) == 0)
        return valid_old, valid_new

    def attend(qs, k_old, v_old, k_new, v_new, valid_old, valid_new):
        k = jnp.concatenate([k_old, k_new], axis=1)
        v = jnp.concatenate([v_old, v_new], axis=1)
        s = jnp.where(jnp.concatenate([valid_old, valid_new], axis=1), _dot(qs, k), NEG)
        m = jnp.max(s, axis=-1, keepdims=True)
        e = jnp.exp(s - m)
        den = jnp.sum(e, axis=-1, keepdims=True)
        return _dot_nt(e.astype(_BF16), v), den, m

    def classes_reduce(x, dil, op):
        sh = dil
        while sh < LANES:
            x = op(x, pltpu.roll(x, sh, 1))
            sh *= 2
        return x

    def fold_tiles(x, op):
        acc = x[:, :LANES]
        for jt in range(1, x.shape[1] // LANES):
            acc = op(acc, x[:, jt * LANES:(jt + 1) * LANES])
        return acc

    def attend_lane_classes(old_ref, new_g, q_t, window, dil):
        lb = old_ref.shape[1]
        n_tiles = lb // LANES
        assert t_new <= dil and lb % dil == 0 and LANES % dil == 0 and lb <= window
        lane_q = lax.broadcasted_iota(jnp.int32, (A_OUT, LANES), 1)
        q_new = jnp.where(lane_q >= first_new, q_t, 0.0)
        q_cls = classes_reduce(pltpu.roll(q_new, t_new, 1), dil, jnp.add)
        lane_1 = lax.broadcasted_iota(jnp.int32, (1, LANES), 1)
        cls_ok = jnp.concatenate([(lane_1 & (dil - 1)) < t_new] * n_tiles, axis=1)
        o_rows, lse_rows = [], []
        for hh in range(HEADS_A):
            rows = slice(hh * HEAD_DIM, (hh + 1) * HEAD_DIM)
            q_h = jnp.concatenate([q_cls[rows]] * n_tiles, axis=1)
            s_old = jnp.sum(old_ref[rows, :] * q_h, axis=0, keepdims=True)
            s_old = jnp.where(cls_ok, s_old, NEG)
            s_new = jnp.sum(new_g[rows] * q_new[rows], axis=0, keepdims=True)
            s_new = jnp.where(lane_1 < t_new, pltpu.roll(s_new, t_new, 1), NEG)
            m = classes_reduce(jnp.maximum(fold_tiles(s_old, jnp.maximum), s_new), dil, jnp.maximum)
            e_old = jnp.where(cls_ok, jnp.exp(s_old - jnp.concatenate([m] * n_tiles, axis=1)), 0.0)
            e_new = jnp.where(lane_1 < t_new, jnp.exp(s_new - m), 0.0)
            den = classes_reduce(fold_tiles(e_old, jnp.add) + e_new, dil, jnp.add)
            den = jnp.where((lane_1 & (dil - 1)) < t_new, den, 1.0)
            v_rows = slice(A_OUT + hh * HEAD_DIM, A_OUT + (hh + 1) * HEAD_DIM)
            acc = fold_tiles(old_ref[v_rows, :] * e_old, jnp.add)
            acc = acc + pltpu.roll(new_g[v_rows], t_new, 1) * e_new
            acc = classes_reduce(acc, dil, jnp.add)
            o_rows.append(acc / den)
            lse_rows.append(jnp.broadcast_to(m + jnp.log(den), (HEAD_DIM, LANES)))
        out_t = jnp.concatenate(o_rows, axis=0)
        lse_t = jnp.concatenate(lse_rows, axis=0)
        return out_t.T[:t_new], lse_t.T[:t_new]

    outs, lses = [], []
    for g, (old_ref, out_ref) in enumerate(((c1_ref, n1_ref), (c2_ref, n2_ref), (c3_ref, n3_ref))):
        window, dil = DIL_PAIRS[g]
        lb = old_ref.shape[1]
        new_g = new_t[g * KV_ROWS_A:(g + 1) * KV_ROWS_A]
        if g in VPU_GROUPS:
            q_row0 = KVT_ROWS + VPU_GROUPS.index(g) * A_OUT
            out, lse = attend_lane_classes(old_ref, new_g, new_t[q_row0:q_row0 + A_OUT],
                                           window, dil)
            outs.append(out)
            lses.append(lse)
            new_cache(old_ref, new_g, out_ref)
            continue
        q0 = MXU_GROUPS.index(g) * A_OUT
        qs = _head_stack_a(q[:, q0:q0 + A_OUT]).astype(_BF16)
        valid_old, valid_new = masks(HEADS_A * t_new, lb, window, dil, False)
        o, den, m = attend(qs, old_ref[0:A_OUT, :].astype(_BF16), old_ref[A_OUT:, :].astype(_BF16),
                           new_g[:A_OUT].astype(_BF16), new_g[A_OUT:].astype(_BF16),
                           valid_old, valid_new)
        outs.append(_head_unstack_a(o, 1.0 / den, t_new))
        lses.append(_head_unstack_a(None, m + jnp.log(den), t_new))
        new_cache(old_ref, new_g, out_ref)
    m = jnp.maximum(jnp.maximum(lses[0], lses[1]), lses[2])
    ws = [jnp.exp(l - m) for l in lses]
    oa_ref[...] = (ws[0] * outs[0] + ws[1] * outs[1] + ws[2] * outs[2]) / (ws[0] + ws[1] + ws[2])

    lb = cb_ref.shape[1]
    new_b = new_t[N_GROUPS_A * KV_ROWS_A:KVT_ROWS]
    qs = _head_stack_b(q[:, len(MXU_GROUPS) * A_OUT:]).astype(_BF16)
    valid_old, valid_new = masks(N_HEADS_B * t_new, lb, WINDOW_B, 1, True)
    o, den, m = attend(qs, cb_ref[0:B_KV, :].astype(_BF16), cb_ref[B_KV:, :].astype(_BF16),
                       new_b[:B_KV].astype(_BF16), new_b[B_KV:].astype(_BF16),
                       valid_old, valid_new)
    scale = _sigmoid(m + jnp.log(den) - _sink_rows(sink_ref, t_new)) / den
    ob_ref[...] = _head_unstack_b(o, scale, t_new)
    new_cache(cb_ref, new_b, nb_ref)


def _mix_out_tile(x, g_ref, oa_ref, ob_ref, wga_ref, wgb_ref, wba_ref, wbb_ref, wout_ref):
    h = _rms(x, g_ref[...]).astype(_BF16)
    gate_a = _dot(h, wga_ref[...])
    gate_b = _dot(h, wgb_ref[...])
    ya = _dot(oa_ref[...].astype(_BF16), wba_ref[...])
    yb = _dot(ob_ref[...].astype(_BF16), wbb_ref[...])
    merged = _sigmoid(gate_a) * ya + _sigmoid(gate_b) * yb
    return x + _dot(merged.astype(_BF16), wout_ref[...])


def _mix_out_kernel(x_ref, *refs, n_mix, bounds, final_norm):
    y = _mix_out_tile(x_ref[...], *refs[:n_mix])
    if bounds is not None:
        y = _ffn_tile(y, *refs[n_mix:n_mix + 5], bounds, final_norm)
    refs[-1][...] = y


def _mix_out(x, layer, g, oa, ob, w_in, w_ba, w_bb, w_out, tm, ffn=None, final_norm=False):
    n_tok, d = x.shape
    assert QKV_COLS % d == 0 and w_in.shape[2] == QKV_COLS + 2 * d
    gate_specs = [_layer_block(w_in, layer, (d, d), (0, QKV_COLS // d + k)) for k in range(2)]
    tok = lambda w: pl.BlockSpec((tm, w), lambda i: (i, 0))
    mix_specs = ([_layer_block(g, layer), tok(A_OUT), tok(B_Q)] + gate_specs
                 + [_layer_block(a, layer) for a in (w_ba, w_bb, w_out)])
    mix_ops = [g, oa, ob, w_in, w_in, w_ba, w_bb, w_out]
    ffn_specs, ffn_ops, bounds = [], [], None
    if ffn is not None:
        ffn_specs, ffn_ops, bounds = _ffn_specs(ffn, layer, d), list(ffn), _ffn_bounds(ffn[1].shape[-1])
    return pl.pallas_call(
        functools.partial(_mix_out_kernel, n_mix=len(mix_ops), bounds=bounds,
                          final_norm=final_norm),
        out_shape=jax.ShapeDtypeStruct((n_tok, d), _F32),
        grid=(n_tok // tm,),
        in_specs=[tok(d)] + mix_specs + ffn_specs,
        out_specs=tok(d),
        compiler_params=_cparams(1),
        name="mix_out" if ffn is None else "sample_out",
    )(x, *mix_ops, *ffn_ops)


def _cache_view(c):
    dep, nb, rows = c.shape[:3]
    return jnp.transpose(c, (0, 1, 3, 4, 5, 2)).reshape(dep, nb, -1, rows)


def _cache_unview(c, heads):
    dep, nb, _, rows = c.shape
    return jnp.transpose(c.reshape(dep, nb, 2, heads, HEAD_DIM, rows), (0, 1, 5, 2, 3, 4))


def kernel(x_prompt, x_sample, cache_a1, cache_a2, cache_a3, cache_b, norm_ffn1, ffn1_gate, ffn1_up, ffn1_down, norm_mix, w_in, sinks, w_branch_a, w_branch_b, w_out, norm_ffn2, ffn2_gate, ffn2_up, ffn2_down, norm_final):
    n_batch, seq, d = x_prompt.shape
    dec_batch, t_new, _ = x_sample.shape
    depth = w_in.shape[0]
    n_sample = dec_batch * t_new
    tm_s = min(TM, n_sample)
    hosts_per_layer = 2
    n_hosted = dec_batch // hosts_per_layer
    tm_p = n_batch * seq // n_hosted
    assert seq == DIL_PAIRS[2][0] and LANES % t_new == 0 and n_sample % tm_s == 0
    assert tm_s % t_new == 0 and dec_batch % (LANES // t_new) == 0
    assert dec_batch % hosts_per_layer == 0 and (n_batch * seq) % n_hosted == 0 and tm_p % 8 == 0

    perm = jnp.array(B_HEAD_PERM)
    col_b = 3 * N_GROUPS_A * A_OUT

    cos_p, sin_p = _rope_tables(jnp.arange(seq))
    cos_s, sin_s = _rope_tables(PAST_LEN + jnp.arange(t_new))
    cos_s, sin_s = jnp.tile(cos_s, (tm_s // t_new, 1)), jnp.tile(sin_s, (tm_s // t_new, 1))
    nat_p, nat_s = _nat_tables(cos_p, sin_p), _nat_tables(cos_s, sin_s)
    t_p, t_s = (cos_p.T, sin_p.T), (cos_s.T, sin_s.T)

    caches = [_cache_view(c) for c in (cache_a1, cache_a2, cache_a3, cache_b)]

    bf = lambda w: w.astype(_BF16)
    w_in_bf = bf(w_in)
    cols = lambda lo, hi: w_in_bf[:, :, lo:hi]
    bq = cols(col_b, col_b + B_Q).reshape(depth, d, N_HEADS_B, HEAD_DIM)[:, :, perm]
    bq = bq.reshape(depth, d, B_Q)
    q_cols = [cols(3 * g * A_OUT, (3 * g + 1) * A_OUT) for g in range(N_GROUPS_A)]
    kv_cols = [cols((3 * g + 1) * A_OUT, (3 * g + 3) * A_OUT) for g in range(N_GROUPS_A)]
    kv_cols.append(cols(col_b + B_Q, QKV_COLS))
    w_kvt = jnp.swapaxes(jnp.concatenate(kv_cols + [q_cols[g] for g in VPU_GROUPS], axis=2), 1, 2)
    w_q = jnp.concatenate([q_cols[g] for g in MXU_GROUPS] + [bq], axis=2)
    w_bb = bf(w_branch_b).reshape(depth, N_HEADS_B, HEAD_DIM, d)[:, perm].reshape(depth, B_Q, d)
    w_ba, w_o = bf(w_branch_a), bf(w_out)
    sinks_perm = sinks[:, perm]
    row = lambda v: v.reshape(-1, 1, d)
    gf = norm_final.reshape(1, d)
    f1 = (row(norm_ffn1), bf(ffn1_gate), bf(ffn1_up), bf(ffn1_down), gf)
    f2 = (row(norm_ffn2), bf(ffn2_gate), bf(ffn2_up), bf(ffn2_down), gf)
    g_mix = row(norm_mix)

    xp = x_prompt.reshape(n_batch * seq, d)
    xs = x_sample.reshape(dec_batch * t_new, d)
    prompt_bufs = []
    sample_bufs = None
    for l in range(depth):
        last = l == depth - 1
        sinks_p = sinks_perm[l]

        xs, q_nat, kvt_new = _sample_in(xs, l, f1, g_mix, w_q, w_kvt, nat_s, t_s, tm_s)
        host = lambda n_base: dict(n_base=n_base, t_new=t_new, sinks_p=sinks_p, q_nat=q_nat,
                                   kvt_new=kvt_new, caches=caches, prev_out=sample_bufs)

        xp, oa_0, ob_0, *sample_bufs = _ffn(xp, l, f1, final_norm=False, tm=tm_p, **host(0))
        outs = _mix_in_prompt(xp, l, g_mix, w_in_bf, bq, w_kvt, nat_p, t_p, n_batch, seq,
                              prompt_bufs)
        prompt_bufs = list(outs[12:])
        oa_p, ob_p = _attn_prompt(sinks_p, outs[:12], n_batch, seq)
        xp = _mix_out(xp, l, g_mix, oa_p, ob_p, w_in_bf, w_ba, w_bb, w_o, 2 * TM)
        xp, oa_1, ob_1, *sample_bufs = _ffn(xp, l, f2, final_norm=last, tm=tm_p,
                                            **host(n_hosted))

        oa_s = jnp.concatenate([oa_0, oa_1], axis=0)
        ob_s = jnp.concatenate([ob_0, ob_1], axis=0)
        xs = _mix_out(xs, l, g_mix, oa_s, ob_s, w_in_bf, w_ba, w_bb, w_o, tm_s, ffn=f2,
                      final_norm=last)

    heads = (HEADS_A, HEADS_A, HEADS_A, N_KV_B)
    new_p = [_cache_unview(prompt_bufs[i], heads[i]) for i in range(4)]
    new_s = [_cache_unview(sample_bufs[i], heads[i]) for i in range(4)]
    return (xp.reshape(n_batch, seq, d), xs.reshape(dec_batch, t_new, d), *new_p, *new_s)
```
